```python
import jax, jax.numpy as jnp
from jax import lax
import numpy as np

D_MODEL = 2048
BATCH = 4
SEQ = 4096
DEPTH = 2

GRID_W = 64
CTX_LEN = 256
HEAD_DIM = 128
N_HEADS = D_MODEL // HEAD_DIM
N_KV_HEADS = N_HEADS // 4
GROUPS = N_HEADS // N_KV_HEADS
ATTN_WIDTH = N_HEADS * HEAD_DIM
KV_WIDTH = N_KV_HEADS * HEAD_DIM
CONV_WIDTH = D_MODEL
CONV_SIZE = 3
Q_BLOCK = 128
ROPE_THETA = 10000.0
AXIS_DIM = HEAD_DIM // 2
PEER_HEADS = 8
PEER_KEY_DIM = 256
PEER_HALF = PEER_KEY_DIM // 2
N_KEYS = 128
N_EXPERTS = N_KEYS * N_KEYS
PEER_TOPK = 16
PEER_CHUNK = 128
N_MOD = 6
EPS = 1e-6

Q_END = ATTN_WIDTH
K_END = Q_END + KV_WIDTH
V_END = K_END + KV_WIDTH
U_END = V_END + CONV_WIDTH
B_END = U_END + CONV_WIDTH
C_END = B_END + CONV_WIDTH
GA_END = C_END + D_MODEL
IN_WIDTH = GA_END + D_MODEL
SPLITS = (Q_END, K_END, V_END, U_END, B_END, C_END, GA_END)

kernel_name = "hybrid_gqa_shortconv_peer_dit"


def _rms_norm(x, g):
    xf = x.astype(jnp.float32)
    y = xf * lax.rsqrt(jnp.mean(xf * xf, axis=-1, keepdims=True) + EPS)
    return y.astype(x.dtype) * g


def _modulate(h, shift, scale):
    return h * (1 + scale) + shift


def _adaln(cvec, w, b):
    return jnp.split(jax.nn.silu(cvec) @ w + b, N_MOD, axis=-1)


def _axial_rope_tables(n):
    rows = n // GRID_W
    row = jnp.repeat(jnp.arange(rows, dtype=jnp.float32), GRID_W)
    col = jnp.tile(jnp.arange(GRID_W, dtype=jnp.float32), rows)
    inv = ROPE_THETA ** (-jnp.arange(0, AXIS_DIM, 2, dtype=jnp.float32) / AXIS_DIM)
    ang_r = row[:, None] * inv[None, :]
    ang_c = col[:, None] * inv[None, :]
    return (jnp.cos(ang_r), jnp.sin(ang_r), jnp.cos(ang_c), jnp.sin(ang_c))


def _rotate(x, cos, sin):
    x1, x2 = jnp.split(x, 2, axis=-1)
    return jnp.concatenate([x1 * cos - x2 * sin, x2 * cos + x1 * sin], axis=-1)


def _apply_rope(x, tabs):
    cr, sr, cc, sc = (t[None, :, None, :] for t in tabs)
    xf = x.astype(jnp.float32)
    out = jnp.concatenate([_rotate(xf[..., :AXIS_DIM], cr, sr),
                           _rotate(xf[..., AXIS_DIM:], cc, sc)], axis=-1)
    return out.astype(x.dtype)


def _block_attention(q, k, v):
    b, t = q.shape[0], q.shape[1]
    nb = t // Q_BLOCK
    qb = q.reshape(b, nb, Q_BLOCK, N_KV_HEADS, GROUPS, HEAD_DIM).transpose(1, 0, 2, 3, 4, 5)
    scale = HEAD_DIM ** -0.5

    def one_block(q_blk):
        s = jnp.einsum('bqkgd,bskd->bkgqs', q_blk, k).astype(jnp.float32) * scale
        p = jax.nn.softmax(s, axis=-1).astype(v.dtype)
        return jnp.einsum('bkgqs,bskd->bqkgd', p, v)

    o = lax.map(one_block, qb)
    return o.transpose(1, 0, 2, 3, 4, 5).reshape(b, t, ATTN_WIDTH)


def _short_conv(u, w):
    up = jnp.pad(u, ((0, 0), (1, 1), (0, 0)))
    return w[0] * up[:, :-2] + w[1] * up[:, 1:-1] + w[2] * up[:, 2:]


def _ctx_kv(p_kv, k_g):
    b, t, _ = p_kv.shape
    k = _rms_norm(p_kv[..., :KV_WIDTH].reshape(b, t, N_KV_HEADS, HEAD_DIM), k_g)
    v = p_kv[..., KV_WIDTH:].reshape(b, t, N_KV_HEADS, HEAD_DIM)
    return k, v


def _token_mixer(p, q_g, k_g, conv_w, b_gate, w_ao, w_co, w_o, rope, k_pre, v_pre):
    b, t, _ = p.shape
    q, k, v, u, gate_b, gate_c, g_att, g_conv = jnp.split(p, SPLITS, axis=-1)
    q = _rms_norm(q.reshape(b, t, N_HEADS, HEAD_DIM), q_g)
    k = _rms_norm(k.reshape(b, t, N_KV_HEADS, HEAD_DIM), k_g)
    v = v.reshape(b, t, N_KV_HEADS, HEAD_DIM)
    if rope is not None:
        q = _apply_rope(q, rope)
        k = _apply_rope(k, rope)
    if k_pre is not None:
        k = jnp.concatenate([k, k_pre], axis=1)
        v = jnp.concatenate([v, v_pre], axis=1)
    attn = _block_attention(q.reshape(b, t, N_KV_HEADS, GROUPS, HEAD_DIM), k, v) @ w_ao
    conv = (gate_b * _short_conv(gate_c * u, conv_w)) @ w_co
    merged = (jax.nn.sigmoid(g_att + b_gate[:D_MODEL]) * attn
              + jax.nn.sigmoid(g_conv + b_gate[D_MODEL:]) * conv)
    return merged @ w_o


def _peer(h, wq, k1, k2, u, v):
    b, t, d = h.shape
    q = (h @ wq).reshape(b, t, PEER_HEADS, 2, PEER_HALF)
    s1 = jnp.einsum('bthd,hkd->bthk', q[..., 0, :], k1)
    s2 = jnp.einsum('bthd,hkd->bthk', q[..., 1, :], k2)
    v1, i1 = lax.top_k(s1, PEER_TOPK)
    v2, i2 = lax.top_k(s2, PEER_TOPK)
    cand_s = (v1[..., :, None] + v2[..., None, :]).reshape(b, t, PEER_HEADS, PEER_TOPK * PEER_TOPK)
    cand_i = (i1[..., :, None] * N_KEYS + i2[..., None, :]).reshape(b, t, PEER_HEADS, PEER_TOPK * PEER_TOPK)
    top_s, top_j = lax.top_k(cand_s, PEER_TOPK)
    eidx = jnp.take_along_axis(cand_i, top_j, axis=-1)
    gw = jax.nn.softmax(top_s.astype(jnp.float32), axis=-1).astype(h.dtype)
    n = b * t
    nc = n // PEER_CHUNK
    hf = h.reshape(nc, PEER_CHUNK, d)
    ef = eidx.reshape(nc, PEER_CHUNK, PEER_HEADS * PEER_TOPK)
    gf = gw.reshape(nc, PEER_CHUNK, PEER_HEADS * PEER_TOPK)

    def one_chunk(args):
        hc, ec, gc = args
        act = jax.nn.gelu(jnp.einsum('tkd,td->tk', u[ec], hc), approximate=False)
        return jnp.einsum('tk,tkd->td', gc * act, v[ec])

    out = lax.map(one_chunk, (hf, ef, gf))
    return out.reshape(b, t, d)


def setup_inputs(seed: int = 0) -> dict:
    key = jax.random.key(seed)
    ks = jax.random.split(key, 24)
    f32 = jnp.float32
    nrm = lambda k, shape, s: jax.random.normal(k, shape, f32) * s
    D = D_MODEL
    return {
        "x": nrm(ks[0], (BATCH, SEQ, D), 1.0),
        "c": nrm(ks[1], (BATCH, D), 1.0),
        "ctx": nrm(ks[2], (BATCH, CTX_LEN, D), 1.0),
        "c_ctx": nrm(ks[3], (D,), 1.0),
        "ada_w": nrm(ks[4], (DEPTH, D, N_MOD * D), 0.5 * D ** -0.5),
        "ada_b": nrm(ks[5], (DEPTH, N_MOD * D), 0.01),
        "norm1_g": 1.0 + nrm(ks[6], (DEPTH, D), 0.02),
        "norm2_g": 1.0 + nrm(ks[7], (DEPTH, D), 0.02),
        "w_in": nrm(ks[8], (DEPTH, D, IN_WIDTH), D ** -0.5),
        "b_gate": nrm(ks[9], (DEPTH, 2 * D), 0.02),
        "q_norm_g": 1.0 + nrm(ks[10], (DEPTH, HEAD_DIM), 0.02),
        "k_norm_g": 1.0 + nrm(ks[11], (DEPTH, HEAD_DIM), 0.02),
        "conv_w": nrm(ks[12], (DEPTH, CONV_SIZE, CONV_WIDTH), CONV_SIZE ** -0.5),
        "w_attn_out": nrm(ks[13], (DEPTH, ATTN_WIDTH, D), ATTN_WIDTH ** -0.5),
        "w_conv_out": nrm(ks[14], (DEPTH, CONV_WIDTH, D), CONV_WIDTH ** -0.5),
        "w_o": nrm(ks[15], (DEPTH, D, D), D ** -0.5),
        "peer_wq": nrm(ks[16], (DEPTH, D, PEER_HEADS * PEER_KEY_DIM), D ** -0.5),
        "peer_k1": nrm(ks[17], (DEPTH, PEER_HEADS, N_KEYS, PEER_HALF), PEER_HALF ** -0.5),
        "peer_k2": nrm(ks[18], (DEPTH, PEER_HEADS, N_KEYS, PEER_HALF), PEER_HALF ** -0.5),
        "peer_u": nrm(ks[19], (DEPTH, N_EXPERTS, D), D ** -0.5),
        "peer_v": nrm(ks[20], (DEPTH, N_EXPERTS, D), 1.0),
        "final_g": 1.0 + nrm(ks[21], (D,), 0.02),
    }


def reference(x, c, ctx, c_ctx, ada_w, ada_b, norm1_g, norm2_g, w_in, b_gate,
              q_norm_g, k_norm_g, conv_w, w_attn_out, w_conv_out, w_o,
              peer_wq, peer_k1, peer_k2, peer_u, peer_v, final_g):
    rope = _axial_rope_tables(x.shape[1])
    xc = ctx
    for l in range(DEPTH):
        last = l == DEPTH - 1
        sh1, sc1, g1, sh2, sc2, g2 = [m[:, None, :] for m in _adaln(c, ada_w[l], ada_b[l])]
        csh1, csc1, cg1, csh2, csc2, cg2 = _adaln(c_ctx, ada_w[l], ada_b[l])
        mix_args = (q_norm_g[l], k_norm_g[l], conv_w[l], b_gate[l],
                    w_attn_out[l], w_conv_out[l], w_o[l])

        h = _modulate(_rms_norm(x, norm1_g[l]), sh1, sc1)
        hc = _modulate(_rms_norm(xc, norm1_g[l]), csh1, csc1)
        if last:
            kc, vc = _ctx_kv(hc @ w_in[l][:, Q_END:V_END], k_norm_g[l])
        else:
            pc = hc @ w_in[l]
            kc, vc = _ctx_kv(pc[..., Q_END:V_END], k_norm_g[l])
            xc = xc + cg1 * _token_mixer(pc, *mix_args, None, None, None)
        x = x + g1 * _token_mixer(h @ w_in[l], *mix_args, rope, kc, vc)

        peer_args = (peer_wq[l], peer_k1[l], peer_k2[l], peer_u[l], peer_v[l])
        h2 = _modulate(_rms_norm(x, norm2_g[l]), sh2, sc2)
        x = x + g2 * _peer(h2, *peer_args)
        if not last:
            hc2 = _modulate(_rms_norm(xc, norm2_g[l]), csh2, csc2)
            xc = xc + cg2 * _peer(hc2, *peer_args)
    return _rms_norm(x, final_g)
```

```python
import functools

import jax
import jax.numpy as jnp
from jax import lax
from jax.experimental import pallas as pl
from jax.experimental.pallas import tpu as pltpu

F32 = jnp.float32
BF16 = jnp.bfloat16

HEAD_DIM = 128
KV_GROUPS = 4
GRID_W = 64
ROPE_THETA = 10000.0
PEER_HEADS = 8
N_KEYS = 128
PEER_TOPK = 16
N_MOD = 6
EPS = 1e-6
LANES = 128
BF16_SUBLANES = 16
VMEM_LIMIT_BYTES = 56 * 1024 * 1024

_NT = (((1,), (1,)), ((), ()))


def _cparams(*sem):
    return pltpu.CompilerParams(dimension_semantics=sem, vmem_limit_bytes=VMEM_LIMIT_BYTES)


def _mod_row(row0, n_lat, seq, n_batch):
    return jnp.where(row0 < n_lat, row0 // seq, n_batch)


def _mod_vec(mod_ref, b, k, d):
    return mod_ref[pl.ds(b, 1), k * d:(k + 1) * d]


def _adaln_kernel(c_ref, w_ref, b_ref, o_ref):
    cv = c_ref[...]
    s = cv * jax.nn.sigmoid(cv)
    o_ref[0] = jnp.dot(s, w_ref[0], preferred_element_type=F32) + b_ref[0]


def _adaln(cvec, ada_w, ada_b, tn=1024):
    depth, d, n = ada_w.shape
    rows = cvec.shape[0]
    return pl.pallas_call(
        _adaln_kernel,
        grid=(depth, n // tn),
        in_specs=[pl.BlockSpec((rows, d), lambda l, j: (0, 0)),
                  pl.BlockSpec((1, d, tn), lambda l, j: (l, 0, j)),
                  pl.BlockSpec((1, 1, tn), lambda l, j: (l, 0, j))],
        out_specs=pl.BlockSpec((1, rows, tn), lambda l, j: (l, 0, j)),
        out_shape=jax.ShapeDtypeStruct((depth, rows, n), F32),
        compiler_params=_cparams("arbitrary", "arbitrary"),
        name="adaln",
    )(cvec, ada_w, ada_b.reshape(depth, 1, n))


def _norm_mod_kernel(x_ref, g_ref, mod_ref, o_ref, *, tm, n_lat, seq, n_batch, k_shift, k_scale):
    d = x_ref.shape[1]
    b = _mod_row(pl.program_id(0) * tm, n_lat, seq, n_batch)
    xf = x_ref[...]
    y = xf * lax.rsqrt(jnp.mean(xf * xf, axis=-1, keepdims=True) + EPS)
    h = y * g_ref[...]
    shift = _mod_vec(mod_ref, b, k_shift, d)
    scale = _mod_vec(mod_ref, b, k_scale, d)
    o_ref[...] = (h * (1.0 + scale) + shift).astype(o_ref.dtype)


def _norm_mod(x, g, mod, *, rows, n_lat, seq, n_batch, k_shift, k_scale, tm=512):
    d = x.shape[1]
    kern = functools.partial(_norm_mod_kernel, tm=tm, n_lat=n_lat, seq=seq, n_batch=n_batch,
                             k_shift=k_shift, k_scale=k_scale)
    return pl.pallas_call(
        kern,
        grid=(rows // tm,),
        in_specs=[pl.BlockSpec((tm, d), lambda i: (i, 0)),
                  pl.BlockSpec((1, d), lambda i: (0, 0)),
                  pl.BlockSpec(mod.shape, lambda i: (0, 0))],
        out_specs=pl.BlockSpec((tm, d), lambda i: (i, 0)),
        out_shape=jax.ShapeDtypeStruct((rows, d), BF16),
        compiler_params=_cparams("parallel"),
        name="norm_mod",
    )(x, g.reshape(1, d), mod)


def _head_norm_rope(a, g, cos, sin, out_scale):
    y = a * lax.rsqrt(jnp.mean(a * a, axis=-1, keepdims=True) + EPS) * g
    lane = lax.broadcasted_iota(jnp.int32, y.shape, 1)
    lower = (lane & (HEAD_DIM // 2 - 1)) < HEAD_DIM // 4
    partner = jnp.where(lower, pltpu.roll(y, HEAD_DIM - HEAD_DIM // 4, 1),
                        pltpu.roll(y, HEAD_DIM // 4, 1))
    r = y * cos + partner * sin
    if out_scale != 1.0:
        r = r * out_scale
    return r


def _q_kernel(h_ref, w_ref, g_ref, cos_ref, sin_ref, o_ref, *, out_scale):
    acc = jnp.dot(h_ref[...], w_ref[...], preferred_element_type=F32)
    g, cos, sin = g_ref[...], cos_ref[...], sin_ref[...]
    for hh in range(acc.shape[1] // HEAD_DIM):
        sl = slice(hh * HEAD_DIM, (hh + 1) * HEAD_DIM)
        o_ref[:, sl] = _head_norm_rope(acc[:, sl], g, cos, sin, out_scale).astype(o_ref.dtype)


def _kv_kernel(h_ref, wk_ref, wv_ref, g_ref, cos_ref, sin_ref, k_ref, v_ref):
    h = h_ref[...]
    acc = jnp.dot(h, wk_ref[...], preferred_element_type=F32)
    g, cos, sin = g_ref[...], cos_ref[...], sin_ref[...]
    for hh in range(acc.shape[1] // HEAD_DIM):
        sl = slice(hh * HEAD_DIM, (hh + 1) * HEAD_DIM)
        k_ref[:, sl] = _head_norm_rope(acc[:, sl], g, cos, sin, 1.0).astype(k_ref.dtype)
    v_ref[...] = jnp.dot(h, wv_ref[...], preferred_element_type=F32).astype(v_ref.dtype)


def _conv_in_kernel(h_ref, wu_ref, wb_ref, wc_ref, z_ref, gb_ref):
    h = h_ref[...]
    u = jnp.dot(h, wu_ref[...], preferred_element_type=F32)
    gc = jnp.dot(h, wc_ref[...], preferred_element_type=F32)
    z_ref[...] = (gc * u).astype(z_ref.dtype)
    gb_ref[...] = jnp.dot(h, wb_ref[...], preferred_element_type=F32).astype(gb_ref.dtype)


def _gate_kernel(h_ref, w_ref, b_ref, o_ref):
    acc = jnp.dot(h_ref[...], w_ref[...], preferred_element_type=F32)
    o_ref[...] = jax.nn.sigmoid(acc + b_ref[...]).astype(o_ref.dtype)


def _rope_block(i, tm, n_lat, seq):
    return jnp.where(i * tm < n_lat, (i * tm % seq) // tm, seq // tm)


def _in_proj(h, w_in, q_g, k_g, b_gate, cos_t, sin_t, *, rows_q, rows_kv, n_lat, seq, d,
             tm=512, tn=512):
    kd = h.shape[1]
    kvw = d // KV_GROUPS
    q_end, k_end, v_end = d, d + kvw, d + 2 * kvw
    u_end, b_end, c_end = v_end + d, v_end + 2 * d, v_end + 3 * d
    hspec = pl.BlockSpec((tm, kd), lambda i, j: (i, 0))
    hd_spec = pl.BlockSpec((1, HEAD_DIM), lambda i, j: (0, 0))
    tab_spec = pl.BlockSpec((tm, HEAD_DIM), lambda i, j: (_rope_block(i, tm, n_lat, seq), 0))

    def wspec(col0):
        return pl.BlockSpec((kd, tn), lambda i, j: (0, col0 // tn + j))

    def ospec():
        return pl.BlockSpec((tm, tn), lambda i, j: (i, j))

    q = pl.pallas_call(
        functools.partial(_q_kernel, out_scale=HEAD_DIM ** -0.5),
        grid=(rows_q // tm, d // tn),
        in_specs=[hspec, wspec(0), hd_spec, tab_spec, tab_spec],
        out_specs=ospec(),
        out_shape=jax.ShapeDtypeStruct((rows_q, d), BF16),
        compiler_params=_cparams("parallel", "arbitrary"),
        name="proj_q",
    )(h, w_in, q_g.reshape(1, HEAD_DIM), cos_t, sin_t)

    assert kvw == tn
    k, v = pl.pallas_call(
        _kv_kernel,
        grid=(rows_kv // tm, 1),
        in_specs=[hspec, wspec(q_end), wspec(k_end), hd_spec, tab_spec, tab_spec],
        out_specs=[ospec(), ospec()],
        out_shape=[jax.ShapeDtypeStruct((rows_kv, kvw), BF16)] * 2,
        compiler_params=_cparams("parallel", "arbitrary"),
        name="proj_kv",
    )(h, w_in, w_in, k_g.reshape(1, HEAD_DIM), cos_t, sin_t)

    z, gb = pl.pallas_call(
        _conv_in_kernel,
        grid=(rows_q // tm, d // tn),
        in_specs=[hspec, wspec(v_end), wspec(u_end), wspec(b_end)],
        out_specs=[ospec(), ospec()],
        out_shape=[jax.ShapeDtypeStruct((rows_q, d), BF16)] * 2,
        compiler_params=_cparams("parallel", "arbitrary"),
        name="proj_conv",
    )(h, w_in, w_in, w_in)

    gates = pl.pallas_call(
        _gate_kernel,
        grid=(rows_q // tm, 2 * d // tn),
        in_specs=[hspec, wspec(c_end), pl.BlockSpec((1, tn), lambda i, j: (0, j))],
        out_specs=ospec(),
        out_shape=jax.ShapeDtypeStruct((rows_q, 2 * d), BF16),
        compiler_params=_cparams("parallel", "arbitrary"),
        name="proj_gates",
    )(h, w_in, b_gate.reshape(1, 2 * d))
    return q, k, v, z, gb, gates


def _attn_kernel(q_ref, kl_ref, vl_ref, kc_ref, vc_ref, o_ref, *, tq, n_lat):
    is_lat = pl.program_id(1) * tq < n_lat

    def run(with_lat):
        kc, vc = kc_ref[...], vc_ref[...]
        for hh in range(KV_GROUPS):
            sl = slice(hh * HEAD_DIM, (hh + 1) * HEAD_DIM)
            q = q_ref[:, sl]
            sc = lax.dot_general(q, kc, _NT, preferred_element_type=F32)
            m = jnp.max(sc, axis=-1, keepdims=True)
            if with_lat:
                s_lat = lax.dot_general(q, kl_ref[...], _NT, preferred_element_type=F32)
                m = jnp.maximum(m, jnp.max(s_lat, axis=-1, keepdims=True))
            pc = jnp.exp(sc - m)
            den = jnp.sum(pc, axis=-1, keepdims=True)
            o = jnp.dot(pc.astype(vc.dtype), vc, preferred_element_type=F32)
            if with_lat:
                p_lat = jnp.exp(s_lat - m)
                den = den + jnp.sum(p_lat, axis=-1, keepdims=True)
                o = o + jnp.dot(p_lat.astype(vc.dtype), vl_ref[...], preferred_element_type=F32)
            o_ref[:, sl] = (o / den).astype(o_ref.dtype)

    @pl.when(is_lat)
    def _():
        run(True)

    @pl.when(jnp.logical_not(is_lat))
    def _():
        run(False)


def _attention(q, k, v, *, rows_q, n_lat, seq, ctx_len, tq=256):
    assert ctx_len % tq == 0 or tq % ctx_len == 0
    assert tq <= ctx_len
    gw = KV_GROUPS * HEAD_DIM
    n_kv = k.shape[1] // HEAD_DIM

    def batch(i):
        return jnp.where(i * tq < n_lat, i * tq // seq, (i * tq - n_lat) // ctx_len)

    qspec = pl.BlockSpec((tq, gw), lambda g, i: (i, g))
    lat_spec = pl.BlockSpec((seq, HEAD_DIM), lambda g, i: (batch(i), g))
    ctx_spec = pl.BlockSpec((ctx_len, HEAD_DIM), lambda g, i: (n_lat // ctx_len + batch(i), g))
    return pl.pallas_call(
        functools.partial(_attn_kernel, tq=tq, n_lat=n_lat),
        grid=(n_kv, rows_q // tq),
        in_specs=[qspec, lat_spec, lat_spec, ctx_spec, ctx_spec],
        out_specs=qspec,
        out_shape=jax.ShapeDtypeStruct((rows_q, q.shape[1]), BF16),
        compiler_params=_cparams("parallel", "arbitrary"),
        name="attention",
    )(q, k, v, k, v)


def _merge_kernel(attn_ref, z_ref, zp_ref, zn_ref, gb_ref, ga_ref, gc_ref, cw_ref,
                  wao_ref, wco_ref, o_ref, y_scr, *, tm, n_lat, seq, ctx_len):
    @pl.when(pl.program_id(1) == 0)
    def _():
        z = z_ref[...].astype(F32)
        loc = lax.broadcasted_iota(jnp.int32, (tm, 1), 0)
        row = pl.program_id(0) * tm + loc
        seqlen = jnp.where(row < n_lat, seq, ctx_len)
        pos = row & (seqlen - 1)
        halo_prev = zp_ref[...].astype(F32)[BF16_SUBLANES - 1:BF16_SUBLANES, :]
        halo_next = zn_ref[...].astype(F32)[0:1, :]
        z_prev = jnp.where(loc == 0, halo_prev, pltpu.roll(z, 1, 0))
        z_prev = jnp.where(pos == 0, 0.0, z_prev)
        z_next = jnp.where(loc == tm - 1, halo_next, pltpu.roll(z, tm - 1, 0))
        z_next = jnp.where(pos == seqlen - 1, 0.0, z_next)
        cw = cw_ref[...]
        conv = cw[0:1, :] * z_prev + cw[1:2, :] * z + cw[2:3, :] * z_next
        y_scr[...] = (gb_ref[...].astype(F32) * conv).astype(y_scr.dtype)

    acc_a = jnp.dot(attn_ref[...], wao_ref[...], preferred_element_type=F32)
    acc_c = jnp.dot(y_scr[...], wco_ref[...], preferred_element_type=F32)
    o_ref[...] = (ga_ref[...].astype(F32) * acc_a + gc_ref[...].astype(F32) * acc_c).astype(o_ref.dtype)


def _merge(attn, z, gb, gates, conv_w, w_ao, w_co, *, rows, n_lat, seq, ctx_len, tm=512, tn=512):
    d = attn.shape[1]
    hb = BF16_SUBLANES
    n_hblk = z.shape[0] // hb
    row_spec = pl.BlockSpec((tm, d), lambda i, j: (i, 0))
    prev_spec = pl.BlockSpec((hb, d), lambda i, j: (jnp.maximum(i * (tm // hb) - 1, 0), 0))
    next_spec = pl.BlockSpec((hb, d), lambda i, j: (jnp.minimum((i + 1) * (tm // hb), n_hblk - 1), 0))
    wspec = pl.BlockSpec((d, tn), lambda i, j: (0, j))
    return pl.pallas_call(
        functools.partial(_merge_kernel, tm=tm, n_lat=n_lat, seq=seq, ctx_len=ctx_len),
        grid=(rows // tm, d // tn),
        in_specs=[row_spec, row_spec, prev_spec, next_spec, row_spec,
                  pl.BlockSpec((tm, tn), lambda i, j: (i, j)),
                  pl.BlockSpec((tm, tn), lambda i, j: (i, d // tn + j)),
                  pl.BlockSpec(conv_w.shape, lambda i, j: (0, 0)),
                  wspec, wspec],
        out_specs=pl.BlockSpec((tm, tn), lambda i, j: (i, j)),
        out_shape=jax.ShapeDtypeStruct((rows, d), BF16),
        scratch_shapes=[pltpu.VMEM((tm, d), BF16)],
        compiler_params=_cparams("parallel", "arbitrary"),
        name="merge",
    )(attn, z, z, z, gb, gates, gates, conv_w, w_ao, w_co)


def _resid_proj_kernel(a_ref, w_ref, x_ref, mod_ref, o_ref, *, tm, tn, n_lat, seq, n_batch, k_gate, d):
    b = _mod_row(pl.program_id(0) * tm, n_lat, seq, n_batch)
    col0 = pl.multiple_of(k_gate * d + pl.program_id(1) * tn, LANES)
    gate = mod_ref[pl.ds(b, 1), pl.ds(col0, tn)]
    acc = jnp.dot(a_ref[...], w_ref[...], preferred_element_type=F32)
    o_ref[...] = x_ref[...] + gate * acc


def _resid_proj(a, w, x, mod, *, rows, n_lat, seq, n_batch, k_gate, tm=512, tn=512):
    d = a.shape[1]
    return pl.pallas_call(
        functools.partial(_resid_proj_kernel, tm=tm, tn=tn, n_lat=n_lat, seq=seq, n_batch=n_batch,
                          k_gate=k_gate, d=d),
        grid=(rows // tm, d // tn),
        in_specs=[pl.BlockSpec((tm, d), lambda i, j: (i, 0)),
                  pl.BlockSpec((d, tn), lambda i, j: (0, j)),
                  pl.BlockSpec((tm, tn), lambda i, j: (i, j)),
                  pl.BlockSpec(mod.shape, lambda i, j: (0, 0))],
        out_specs=pl.BlockSpec((tm, tn), lambda i, j: (i, j)),
        out_shape=jax.ShapeDtypeStruct((rows, d), F32),
        compiler_params=_cparams("parallel", "arbitrary"),
        name="resid_proj",
    )(a, w, x, mod)


N_CAND = 80


def _top_values(s, k, out_ref):
    cur = s
    mx = None
    for r in range(k):
        mx = jnp.max(cur, axis=0, keepdims=True)
        if out_ref is not None:
            out_ref[r:r + 1, :] = mx
        cur = jnp.where(cur == mx, -jnp.inf, cur)
    return mx


def _candidate_sums(v1_ref, v2_ref, cand_ref):
    cand_ref[0:16, :] = v1_ref[0:1, :] + v2_ref[...]
    for a in range(1, 8):
        cand_ref[8 + 8 * a:16 + 8 * a, :] = v1_ref[a:a + 1, :] + v2_ref[0:8, :]
    cand_ref[72:80, :] = v1_ref[8:16, :] + v2_ref[0:1, :]


def _peer_score_kernel(h_ref, wq_ref, k1_ref, k2_ref, s1_ref, s2_ref, st_ref, v1_scr, v2_scr, cand_scr):
    qp = jnp.dot(h_ref[...], wq_ref[...], preferred_element_type=F32)
    for h in range(PEER_HEADS):
        q1 = qp[:, (2 * h) * N_KEYS:(2 * h + 1) * N_KEYS]
        q2 = qp[:, (2 * h + 1) * N_KEYS:(2 * h + 2) * N_KEYS]
        s1 = lax.dot_general(k1_ref[h], q1, _NT, preferred_element_type=F32)
        s2 = lax.dot_general(k2_ref[h], q2, _NT, preferred_element_type=F32)
        s1_ref[h] = s1
        s2_ref[h] = s2
        _top_values(s1, PEER_TOPK, v1_scr)
        _top_values(s2, PEER_TOPK, v2_scr)
        _candidate_sums(v1_scr, v2_scr, cand_scr)
        cand = cand_scr[...]
        tau = _top_values(cand, PEER_TOPK, None)
        top = v1_scr[0:1, :] + v2_scr[0:1, :]
        z = jnp.sum(jnp.where(cand >= tau, jnp.exp(cand - top), 0.0), axis=0, keepdims=True)
        st_ref[h:h + 1, :] = tau
        st_ref[PEER_HEADS + h:PEER_HEADS + h + 1, :] = top + jnp.log(z)


def _peer_scores(h2, wq, k1, k2, *, rows, tm=256):
    d = h2.shape[1]
    sshape = jax.ShapeDtypeStruct((PEER_HEADS, N_KEYS, rows), F32)
    sspec = pl.BlockSpec((PEER_HEADS, N_KEYS, tm), lambda i: (0, 0, i))
    return pl.pallas_call(
        _peer_score_kernel,
        grid=(rows // tm,),
        in_specs=[pl.BlockSpec((tm, d), lambda i: (i, 0)),
                  pl.BlockSpec(wq.shape, lambda i: (0, 0)),
                  pl.BlockSpec(k1.shape, lambda i: (0, 0, 0)),
                  pl.BlockSpec(k2.shape, lambda i: (0, 0, 0))],
        out_specs=[sspec, sspec, pl.BlockSpec((2 * PEER_HEADS, tm), lambda i: (0, i))],
        out_shape=[sshape, sshape, jax.ShapeDtypeStruct((2 * PEER_HEADS, rows), F32)],
        scratch_shapes=[pltpu.VMEM((PEER_TOPK, tm), F32), pltpu.VMEM((PEER_TOPK, tm), F32),
                        pltpu.VMEM((N_CAND, tm), F32)],
        compiler_params=_cparams("parallel"),
        name="peer_scores",
    )(h2, wq, k1, k2)


def _gelu(x):
    return 0.5 * x * (1.0 + lax.erf(x * (0.5 ** 0.5)))


def _peer_mix_kernel(h_ref, u_ref, vt_ref, s1_ref, s2_ref, st_ref, x_ref, mod_ref, *rest,
                     tt, te, n_lat, seq, n_batch, k_gate, final_norm):
    if final_norm:
        fg_ref, o_ref, acc_ref, a_scr, w_scr = rest
    else:
        o_ref, acc_ref, a_scr, w_scr = rest
    e = pl.program_id(1)
    d = x_ref.shape[1]

    @pl.when(e == 0)
    def _():
        acc_ref[...] = jnp.zeros_like(acc_ref)

    a_scr[...] = lax.dot_general(u_ref[...], h_ref[...], _NT, preferred_element_type=F32)

    n_i = te // N_KEYS
    n_var = s1_ref.shape[1] // n_i

    def gate_and_mix(var):
        for ii in range(n_i):
            i = var * n_i + ii
            rs = slice(ii * N_KEYS, (ii + 1) * N_KEYS)
            for tc in range(tt // LANES):
                cs = slice(tc * LANES, (tc + 1) * LANES)
                gate = jnp.zeros((N_KEYS, LANES), F32)
                for h in range(PEER_HEADS):
                    s = s2_ref[h, :, cs] + s1_ref[h, i:i + 1, cs]
                    tau = st_ref[h:h + 1, cs]
                    lognorm = st_ref[PEER_HEADS + h:PEER_HEADS + h + 1, cs]
                    gate = gate + jnp.where(s >= tau, jnp.exp(s - lognorm), 0.0)
                act = _gelu(a_scr[rs, cs])
                w_scr[rs, cs] = (gate * act).astype(w_scr.dtype)

    for var in range(n_var):
        pl.when(e % n_var == var)(functools.partial(gate_and_mix, var))

    acc_ref[...] += jnp.dot(vt_ref[...], w_scr[...], preferred_element_type=F32)

    @pl.when(e == pl.num_programs(1) - 1)
    def _():
        b = _mod_row(pl.program_id(0) * tt, n_lat, seq, n_batch)
        xn = x_ref[...] + _mod_vec(mod_ref, b, k_gate, d) * acc_ref[...].T
        if final_norm:
            xn = xn * lax.rsqrt(jnp.mean(xn * xn, axis=-1, keepdims=True) + EPS) * fg_ref[...]
        o_ref[...] = xn


def _peer_mix(h2, u, vt, s1, s2, st, x, mod, final_g, *, rows, n_lat, seq, n_batch, k_gate,
              tt=512, te=512):
    d = h2.shape[1]
    n_exp = u.shape[0]
    final_norm = final_g is not None
    n_i = te // N_KEYS
    s1_rows = max(n_i, 8)
    n_var = s1_rows // n_i
    in_specs = [pl.BlockSpec((tt, d), lambda t, e: (t, 0)),
                pl.BlockSpec((te, d), lambda t, e: (e, 0)),
                pl.BlockSpec((d, te), lambda t, e: (0, e)),
                pl.BlockSpec((PEER_HEADS, s1_rows, tt), lambda t, e: (0, e // n_var, t)),
                pl.BlockSpec((PEER_HEADS, N_KEYS, tt), lambda t, e: (0, 0, t)),
                pl.BlockSpec((2 * PEER_HEADS, tt), lambda t, e: (0, t)),
                pl.BlockSpec((tt, d), lambda t, e: (t, 0)),
                pl.BlockSpec(mod.shape, lambda t, e: (0, 0))]
    args = [h2, u, vt, s1, s2, st, x, mod]
    if final_norm:
        in_specs.append(pl.BlockSpec((1, d), lambda t, e: (0, 0)))
        args.append(final_g.reshape(1, d))
    return pl.pallas_call(
        functools.partial(_peer_mix_kernel, tt=tt, te=te, n_lat=n_lat, seq=seq, n_batch=n_batch,
                          k_gate=k_gate, final_norm=final_norm),
        grid=(rows // tt, n_exp // te),
        in_specs=in_specs,
        out_specs=pl.BlockSpec((tt, d), lambda t, e: (t, 0)),
        out_shape=jax.ShapeDtypeStruct((rows, d), F32),
        scratch_shapes=[pltpu.VMEM((d, tt), F32), pltpu.VMEM((te, tt), F32), pltpu.VMEM((te, tt), BF16)],
        compiler_params=_cparams("parallel", "arbitrary"),
        name="peer_mix",
    )(*args)


def _rope_tables(seq, pad_rows):
    half = HEAD_DIM // 2
    t = jnp.arange(seq)
    row = (t // GRID_W).astype(F32)
    col = (t % GRID_W).astype(F32)
    inv = ROPE_THETA ** (-jnp.arange(0, half, 2, dtype=F32) / half)
    ang_r = row[:, None] * inv[None, :]
    ang_c = col[:, None] * inv[None, :]
    cos = jnp.concatenate([jnp.cos(ang_r), jnp.cos(ang_r), jnp.cos(ang_c), jnp.cos(ang_c)], axis=-1)
    sin = jnp.concatenate([-jnp.sin(ang_r), jnp.sin(ang_r), -jnp.sin(ang_c), jnp.sin(ang_c)], axis=-1)
    cos = jnp.concatenate([cos, jnp.ones((pad_rows, HEAD_DIM), F32)], axis=0)
    sin = jnp.concatenate([sin, jnp.zeros((pad_rows, HEAD_DIM), F32)], axis=0)
    return cos, sin


def kernel(x, c, ctx, c_ctx, ada_w, ada_b, norm1_g, norm2_g, w_in, b_gate, q_norm_g, k_norm_g,
           conv_w, w_attn_out, w_conv_out, w_o, peer_wq, peer_k1, peer_k2, peer_u, peer_v, final_g):
    n_batch, seq, d = x.shape
    ctx_len = ctx.shape[1]
    depth = ada_w.shape[0]
    n_lat = n_batch * seq
    n_all = n_lat + n_batch * ctx_len
    assert seq & (seq - 1) == 0 and ctx_len & (ctx_len - 1) == 0
    tm = 512

    xa = jnp.concatenate([x.reshape(n_lat, d), ctx.reshape(n_batch * ctx_len, d)], axis=0)
    pad = (-(n_batch + 1)) % 8
    cvec = jnp.concatenate([c, c_ctx[None, :], jnp.zeros((pad, d), F32)], axis=0)
    mod_all = _adaln(cvec, ada_w, ada_b)
    cos_t, sin_t = _rope_tables(seq, tm)
    geom = dict(n_lat=n_lat, seq=seq, n_batch=n_batch)

    for l in range(depth):
        last = l == depth - 1
        rows = n_lat if last else n_all
        mod = mod_all[l]
        w_in_l = w_in[l].astype(BF16)

        h = _norm_mod(xa, norm1_g[l], mod, rows=n_all, k_shift=0, k_scale=1, **geom)
        q, k, v, z, gb, gates = _in_proj(h, w_in_l, q_norm_g[l], k_norm_g[l], b_gate[l], cos_t, sin_t,
                                         rows_q=rows, rows_kv=n_all, n_lat=n_lat, seq=seq, d=d, tm=tm)
        attn = _attention(q, k, v, rows_q=rows, n_lat=n_lat, seq=seq, ctx_len=ctx_len)
        merged = _merge(attn, z, gb, gates, conv_w[l], w_attn_out[l].astype(BF16),
                        w_conv_out[l].astype(BF16), rows=rows, n_lat=n_lat, seq=seq, ctx_len=ctx_len)
        xa = _resid_proj(merged, w_o[l].astype(BF16), xa, mod, rows=rows, k_gate=2, **geom)

        h2 = _norm_mod(xa, norm2_g[l], mod, rows=rows, k_shift=3, k_scale=4, **geom)
        s1, s2, st = _peer_scores(h2, peer_wq[l].astype(BF16), peer_k1[l], peer_k2[l], rows=rows)
        xa = _peer_mix(h2, peer_u[l].astype(BF16), peer_v[l].T.astype(BF16), s1, s2, st, xa, mod,
                       final_g if last else None, rows=rows, k_gate=5, **geom)
    return xa.reshape(n_batch, seq, d)
```

```python
import functools

import jax
import jax.numpy as jnp
from jax import lax
from jax.experimental import pallas as pl
from jax.experimental.pallas import tpu as pltpu

F32 = jnp.float32
BF16 = jnp.bfloat16

HEAD_DIM = 128
KV_GROUPS = 4
GRID_W = 64
ROPE_THETA = 10000.0
PEER_HEADS = 8
N_KEYS = 128
PEER_TOPK = 16
N_MOD = 6
EPS = 1e-6
LOG2_E = 1.4426950408889634
LANES = 128
BF16_SUBLANES = 16
VMEM_LIMIT_BYTES = 56 * 1024 * 1024

_NT = (((1,), (1,)), ((), ()))


def _cparams(*sem, flags=None):
    return pltpu.CompilerParams(dimension_semantics=sem, vmem_limit_bytes=VMEM_LIMIT_BYTES, flags=flags)


def _mod_row(row0, n_lat, seq, n_batch):
    return jnp.where(row0 < n_lat, row0 // seq, n_batch)


def _mod_vec(mod_ref, b, k, d):
    return mod_ref[pl.ds(b, 1), k * d:(k + 1) * d]


def _adaln_kernel(c_ref, w_ref, b_ref, o_ref):
    cv = c_ref[...]
    s = cv * jax.nn.sigmoid(cv)
    o_ref[0] = jnp.dot(s, w_ref[0], preferred_element_type=F32) + b_ref[0]


def _adaln(cvec, ada_w, ada_b, tn=1024):
    depth, d, n = ada_w.shape
    rows = cvec.shape[0]
    return pl.pallas_call(
        _adaln_kernel,
        grid=(depth, n // tn),
        in_specs=[pl.BlockSpec((rows, d), lambda l, j: (0, 0)),
                  pl.BlockSpec((1, d, tn), lambda l, j: (l, 0, j)),
                  pl.BlockSpec((1, 1, tn), lambda l, j: (l, 0, j))],
        out_specs=pl.BlockSpec((1, rows, tn), lambda l, j: (l, 0, j)),
        out_shape=jax.ShapeDtypeStruct((depth, rows, n), F32),
        compiler_params=_cparams("arbitrary", "arbitrary"),
        name="adaln",
    )(cvec, ada_w, ada_b.reshape(depth, 1, n))


def _norm_mod_kernel(x_ref, g_ref, mod_ref, o_ref, *maybe_ot_ref, tm, n_lat, seq, n_batch, k_shift, k_scale):
    d = x_ref.shape[1]
    b = _mod_row(pl.program_id(0) * tm, n_lat, seq, n_batch)
    xf = x_ref[...]
    y = xf * lax.rsqrt(jnp.mean(xf * xf, axis=-1, keepdims=True) + EPS)
    h = y * g_ref[...]
    shift = _mod_vec(mod_ref, b, k_shift, d)
    scale = _mod_vec(mod_ref, b, k_scale, d)
    h = h * (1.0 + scale) + shift
    o_ref[...] = h.astype(o_ref.dtype)
    for ot_ref in maybe_ot_ref:
        ot_ref[...] = h.T.astype(ot_ref.dtype)


def _norm_mod(x, g, mod, *, rows, n_lat, seq, n_batch, k_shift, k_scale, with_transpose=False, tm=512):
    d = x.shape[1]
    kern = functools.partial(_norm_mod_kernel, tm=tm, n_lat=n_lat, seq=seq, n_batch=n_batch,
                             k_shift=k_shift, k_scale=k_scale)
    out_specs = [pl.BlockSpec((tm, d), lambda i: (i, 0))]
    out_shape = [jax.ShapeDtypeStruct((rows, d), BF16)]
    if with_transpose:
        out_specs.append(pl.BlockSpec((d, tm), lambda i: (0, i)))
        out_shape.append(jax.ShapeDtypeStruct((d, rows), BF16))
    out = pl.pallas_call(
        kern,
        grid=(rows // tm,),
        in_specs=[pl.BlockSpec((tm, d), lambda i: (i, 0)),
                  pl.BlockSpec((1, d), lambda i: (0, 0)),
                  pl.BlockSpec(mod.shape, lambda i: (0, 0))],
        out_specs=out_specs,
        out_shape=out_shape,
        compiler_params=_cparams("parallel"),
        name="norm_mod",
    )(x, g.reshape(1, d), mod)
    return out if with_transpose else out[0]


def _head_norm_rope(a, g, cos, sin, out_scale):
    y = a * lax.rsqrt(jnp.mean(a * a, axis=-1, keepdims=True) + EPS) * g
    lane = lax.broadcasted_iota(jnp.int32, y.shape, 1)
    lower = (lane & (HEAD_DIM // 2 - 1)) < HEAD_DIM // 4
    partner = jnp.where(lower, pltpu.roll(y, HEAD_DIM - HEAD_DIM // 4, 1),
                        pltpu.roll(y, HEAD_DIM // 4, 1))
    r = y * cos + partner * sin
    if out_scale != 1.0:
        r = r * out_scale
    return r


def _q_kernel(h_ref, w_ref, g_ref, cos_ref, sin_ref, o_ref, *, out_scale):
    acc = jnp.dot(h_ref[...], w_ref[...], preferred_element_type=F32)
    g, cos, sin = g_ref[...], cos_ref[...], sin_ref[...]
    for hh in range(acc.shape[1] // HEAD_DIM):
        sl = slice(hh * HEAD_DIM, (hh + 1) * HEAD_DIM)
        o_ref[:, sl] = _head_norm_rope(acc[:, sl], g, cos, sin, out_scale).astype(o_ref.dtype)


def _kv_kernel(h_ref, wk_ref, wv_ref, g_ref, cos_ref, sin_ref, k_ref, v_ref):
    h = h_ref[...]
    acc = jnp.dot(h, wk_ref[...], preferred_element_type=F32)
    g, cos, sin = g_ref[...], cos_ref[...], sin_ref[...]
    for hh in range(acc.shape[1] // HEAD_DIM):
        sl = slice(hh * HEAD_DIM, (hh + 1) * HEAD_DIM)
        k_ref[:, sl] = _head_norm_rope(acc[:, sl], g, cos, sin, 1.0).astype(k_ref.dtype)
    v_ref[...] = jnp.dot(h, wv_ref[...], preferred_element_type=F32).astype(v_ref.dtype)


def _conv_in_kernel(h_ref, wu_ref, wb_ref, wc_ref, z_ref, gb_ref):
    h = h_ref[...]
    u = jnp.dot(h, wu_ref[...], preferred_element_type=F32)
    gc = jnp.dot(h, wc_ref[...], preferred_element_type=F32)
    z_ref[...] = (gc * u).astype(z_ref.dtype)
    gb_ref[...] = jnp.dot(h, wb_ref[...], preferred_element_type=F32).astype(gb_ref.dtype)


def _gate_kernel(h_ref, w_ref, b_ref, o_ref):
    acc = jnp.dot(h_ref[...], w_ref[...], preferred_element_type=F32)
    o_ref[...] = jax.nn.sigmoid(acc + b_ref[...]).astype(o_ref.dtype)


def _rope_block(i, tm, n_lat, seq):
    return jnp.where(i * tm < n_lat, (i * tm % seq) // tm, seq // tm)


def _in_proj(h, w_in, q_g, k_g, b_gate, cos_t, sin_t, *, rows_q, rows_kv, n_lat, seq, d,
             tm=512, tn=512):
    kd = h.shape[1]
    kvw = d // KV_GROUPS
    q_end, k_end, v_end = d, d + kvw, d + 2 * kvw
    u_end, b_end, c_end = v_end + d, v_end + 2 * d, v_end + 3 * d
    hspec = pl.BlockSpec((tm, kd), lambda i, j: (i, 0))
    hd_spec = pl.BlockSpec((1, HEAD_DIM), lambda i, j: (0, 0))
    tab_spec = pl.BlockSpec((tm, HEAD_DIM), lambda i, j: (_rope_block(i, tm, n_lat, seq), 0))

    def wspec(col0):
        return pl.BlockSpec((kd, tn), lambda i, j: (0, col0 // tn + j))

    def ospec():
        return pl.BlockSpec((tm, tn), lambda i, j: (i, j))

    q = pl.pallas_call(
        functools.partial(_q_kernel, out_scale=HEAD_DIM ** -0.5),
        grid=(rows_q // tm, d // tn),
        in_specs=[hspec, wspec(0), hd_spec, tab_spec, tab_spec],
        out_specs=ospec(),
        out_shape=jax.ShapeDtypeStruct((rows_q, d), BF16),
        compiler_params=_cparams("parallel", "arbitrary"),
        name="proj_q",
    )(h, w_in, q_g.reshape(1, HEAD_DIM), cos_t, sin_t)

    assert kvw == tn
    k, v = pl.pallas_call(
        _kv_kernel,
        grid=(rows_kv // tm, 1),
        in_specs=[hspec, wspec(q_end), wspec(k_end), hd_spec, tab_spec, tab_spec],
        out_specs=[ospec(), ospec()],
        out_shape=[jax.ShapeDtypeStruct((rows_kv, kvw), BF16)] * 2,
        compiler_params=_cparams("parallel", "arbitrary"),
        name="proj_kv",
    )(h, w_in, w_in, k_g.reshape(1, HEAD_DIM), cos_t, sin_t)

    z, gb = pl.pallas_call(
        _conv_in_kernel,
        grid=(rows_q // tm, d // tn),
        in_specs=[hspec, wspec(v_end), wspec(u_end), wspec(b_end)],
        out_specs=[ospec(), ospec()],
        out_shape=[jax.ShapeDtypeStruct((rows_q, d), BF16)] * 2,
        compiler_params=_cparams("parallel", "arbitrary"),
        name="proj_conv",
    )(h, w_in, w_in, w_in)

    gates = pl.pallas_call(
        _gate_kernel,
        grid=(rows_q // tm, 2 * d // tn),
        in_specs=[hspec, wspec(c_end), pl.BlockSpec((1, tn), lambda i, j: (0, j))],
        out_specs=ospec(),
        out_shape=jax.ShapeDtypeStruct((rows_q, 2 * d), BF16),
        compiler_params=_cparams("parallel", "arbitrary"),
        name="proj_gates",
    )(h, w_in, b_gate.reshape(1, 2 * d))
    return q, k, v, z, gb, gates


def _attn_kernel(q_ref, kl_ref, vl_ref, kc_ref, vc_ref, o_ref, *, tq, n_lat):
    is_lat = pl.program_id(1) * tq < n_lat

    def run(with_lat):
        kc, vc = kc_ref[...], vc_ref[...]
        for hh in range(KV_GROUPS):
            sl = slice(hh * HEAD_DIM, (hh + 1) * HEAD_DIM)
            q = q_ref[:, sl]
            sc = lax.dot_general(q, kc, _NT, preferred_element_type=F32)
            m = jnp.max(sc, axis=-1, keepdims=True)
            if with_lat:
                s_lat = lax.dot_general(q, kl_ref[...], _NT, preferred_element_type=F32)
                m = jnp.maximum(m, jnp.max(s_lat, axis=-1, keepdims=True))
            pc = jnp.exp(sc - m)
            den = jnp.sum(pc, axis=-1, keepdims=True)
            o = jnp.dot(pc.astype(vc.dtype), vc, preferred_element_type=F32)
            if with_lat:
                p_lat = jnp.exp(s_lat - m)
                den = den + jnp.sum(p_lat, axis=-1, keepdims=True)
                o = o + jnp.dot(p_lat.astype(vc.dtype), vl_ref[...], preferred_element_type=F32)
            o_ref[:, sl] = (o / den).astype(o_ref.dtype)

    @pl.when(is_lat)
    def _():
        run(True)

    @pl.when(jnp.logical_not(is_lat))
    def _():
        run(False)


def _attention(q, k, v, *, rows_q, n_lat, seq, ctx_len, tq=256):
    assert ctx_len % tq == 0 or tq % ctx_len == 0
    assert tq <= ctx_len
    gw = KV_GROUPS * HEAD_DIM
    n_kv = k.shape[1] // HEAD_DIM

    def batch(i):
        return jnp.where(i * tq < n_lat, i * tq // seq, (i * tq - n_lat) // ctx_len)

    qspec = pl.BlockSpec((tq, gw), lambda g, i: (i, g))
    lat_spec = pl.BlockSpec((seq, HEAD_DIM), lambda g, i: (batch(i), g))
    ctx_spec = pl.BlockSpec((ctx_len, HEAD_DIM), lambda g, i: (n_lat // ctx_len + batch(i), g))
    return pl.pallas_call(
        functools.partial(_attn_kernel, tq=tq, n_lat=n_lat),
        grid=(n_kv, rows_q // tq),
        in_specs=[qspec, lat_spec, lat_spec, ctx_spec, ctx_spec],
        out_specs=qspec,
        out_shape=jax.ShapeDtypeStruct((rows_q, q.shape[1]), BF16),
        compiler_params=_cparams("parallel", "arbitrary"),
        name="attention",
    )(q, k, v, k, v)


def _merge_kernel(attn_ref, z_ref, zp_ref, zn_ref, gb_ref, ga_ref, gc_ref, cw_ref,
                  wao_ref, wco_ref, o_ref, y_scr, *, tm, n_lat, seq, ctx_len):
    @pl.when(pl.program_id(1) == 0)
    def _():
        z = z_ref[...].astype(F32)
        loc = lax.broadcasted_iota(jnp.int32, (tm, 1), 0)
        row = pl.program_id(0) * tm + loc
        seqlen = jnp.where(row < n_lat, seq, ctx_len)
        pos = row & (seqlen - 1)
        halo_prev = zp_ref[...].astype(F32)[BF16_SUBLANES - 1:BF16_SUBLANES, :]
        halo_next = zn_ref[...].astype(F32)[0:1, :]
        z_prev = jnp.where(loc == 0, halo_prev, pltpu.roll(z, 1, 0))
        z_prev = jnp.where(pos == 0, 0.0, z_prev)
        z_next = jnp.where(loc == tm - 1, halo_next, pltpu.roll(z, tm - 1, 0))
        z_next = jnp.where(pos == seqlen - 1, 0.0, z_next)
        cw = cw_ref[...]
        conv = cw[0:1, :] * z_prev + cw[1:2, :] * z + cw[2:3, :] * z_next
        y_scr[...] = (gb_ref[...].astype(F32) * conv).astype(y_scr.dtype)

    acc_a = jnp.dot(attn_ref[...], wao_ref[...], preferred_element_type=F32)
    acc_c = jnp.dot(y_scr[...], wco_ref[...], preferred_element_type=F32)
    o_ref[...] = (ga_ref[...].astype(F32) * acc_a + gc_ref[...].astype(F32) * acc_c).astype(o_ref.dtype)


def _merge(attn, z, gb, gates, conv_w, w_ao, w_co, *, rows, n_lat, seq, ctx_len, tm=512, tn=512):
    d = attn.shape[1]
    hb = BF16_SUBLANES
    n_hblk = z.shape[0] // hb
    row_spec = pl.BlockSpec((tm, d), lambda i, j: (i, 0))
    prev_spec = pl.BlockSpec((hb, d), lambda i, j: (jnp.maximum(i * (tm // hb) - 1, 0), 0))
    next_spec = pl.BlockSpec((hb, d), lambda i, j: (jnp.minimum((i + 1) * (tm // hb), n_hblk - 1), 0))
    wspec = pl.BlockSpec((d, tn), lambda i, j: (0, j))
    return pl.pallas_call(
        functools.partial(_merge_kernel, tm=tm, n_lat=n_lat, seq=seq, ctx_len=ctx_len),
        grid=(rows // tm, d // tn),
        in_specs=[row_spec, row_spec, prev_spec, next_spec, row_spec,
                  pl.BlockSpec((tm, tn), lambda i, j: (i, j)),
                  pl.BlockSpec((tm, tn), lambda i, j: (i, d // tn + j)),
                  pl.BlockSpec(conv_w.shape, lambda i, j: (0, 0)),
                  wspec, wspec],
        out_specs=pl.BlockSpec((tm, tn), lambda i, j: (i, j)),
        out_shape=jax.ShapeDtypeStruct((rows, d), BF16),
        scratch_shapes=[pltpu.VMEM((tm, d), BF16)],
        compiler_params=_cparams("parallel", "arbitrary"),
        name="merge",
    )(attn, z, z, z, gb, gates, gates, conv_w, w_ao, w_co)


def _resid_proj_kernel(a_ref, w_ref, x_ref, mod_ref, o_ref, *, tm, tn, n_lat, seq, n_batch, k_gate, d):
    b = _mod_row(pl.program_id(0) * tm, n_lat, seq, n_batch)
    col0 = pl.multiple_of(k_gate * d + pl.program_id(1) * tn, LANES)
    gate = mod_ref[pl.ds(b, 1), pl.ds(col0, tn)]
    acc = jnp.dot(a_ref[...], w_ref[...], preferred_element_type=F32)
    o_ref[...] = x_ref[...] + gate * acc


def _resid_proj(a, w, x, mod, *, rows, n_lat, seq, n_batch, k_gate, tm=512, tn=512):
    d = a.shape[1]
    return pl.pallas_call(
        functools.partial(_resid_proj_kernel, tm=tm, tn=tn, n_lat=n_lat, seq=seq, n_batch=n_batch,
                          k_gate=k_gate, d=d),
        grid=(rows // tm, d // tn),
        in_specs=[pl.BlockSpec((tm, d), lambda i, j: (i, 0)),
                  pl.BlockSpec((d, tn), lambda i, j: (0, j)),
                  pl.BlockSpec((tm, tn), lambda i, j: (i, j)),
                  pl.BlockSpec(mod.shape, lambda i, j: (0, 0))],
        out_specs=pl.BlockSpec((tm, tn), lambda i, j: (i, j)),
        out_shape=jax.ShapeDtypeStruct((rows, d), F32),
        compiler_params=_cparams("parallel", "arbitrary"),
        name="resid_proj",
    )(a, w, x, mod)


N_CAND = 80


def _top_values(s, k, out_ref):
    cur = s
    mx = None
    for r in range(k):
        mx = jnp.max(cur, axis=0, keepdims=True)
        if out_ref is not None:
            out_ref[r:r + 1, :] = mx
        cur = jnp.where(cur == mx, -jnp.inf, cur)
    return mx


def _candidate_sums(v1_ref, v2_ref, cand_ref):
    cand_ref[0:16, :] = v1_ref[0:1, :] + v2_ref[...]
    for a in range(1, 8):
        cand_ref[8 + 8 * a:16 + 8 * a, :] = v1_ref[a:a + 1, :] + v2_ref[0:8, :]
    cand_ref[72:80, :] = v1_ref[8:16, :] + v2_ref[0:1, :]


def _peer_score_kernel(h_ref, wq_ref, k1_ref, k2_ref, s1_ref, s2_ref, st_ref, v1_scr, v2_scr, cand_scr,
                       *, n_i):
    qp = jnp.dot(h_ref[...], wq_ref[...], preferred_element_type=F32)
    for h in range(PEER_HEADS):
        q1 = qp[:, (2 * h) * N_KEYS:(2 * h + 1) * N_KEYS]
        q2 = qp[:, (2 * h + 1) * N_KEYS:(2 * h + 2) * N_KEYS]
        s1 = lax.dot_general(k1_ref[h], q1, _NT, preferred_element_type=F32) * LOG2_E
        s2 = lax.dot_general(k2_ref[h], q2, _NT, preferred_element_type=F32) * LOG2_E
        for g in range(N_KEYS // n_i):
            s1_ref[g, h * n_i:(h + 1) * n_i, :] = s1[g * n_i:(g + 1) * n_i, :]
        s2_ref[h] = s2
        _top_values(s1, PEER_TOPK, v1_scr)
        _top_values(s2, PEER_TOPK, v2_scr)
        _candidate_sums(v1_scr, v2_scr, cand_scr)
        cand = cand_scr[...]
        tau = _top_values(cand, PEER_TOPK, None)
        top = v1_scr[0:1, :] + v2_scr[0:1, :]
        z = jnp.sum(jnp.where(cand >= tau, jnp.exp2(cand - top), 0.0), axis=0, keepdims=True)
        st_ref[h:h + 1, :] = tau
        st_ref[PEER_HEADS + h:PEER_HEADS + h + 1, :] = top + jnp.log2(z)


def _peer_scores(h2, wq, k1, k2, *, rows, n_i, tm=256):
    d = h2.shape[1]
    n_grp = N_KEYS // n_i
    return pl.pallas_call(
        functools.partial(_peer_score_kernel, n_i=n_i),
        grid=(rows // tm,),
        in_specs=[pl.BlockSpec((tm, d), lambda i: (i, 0)),
                  pl.BlockSpec(wq.shape, lambda i: (0, 0)),
                  pl.BlockSpec(k1.shape, lambda i: (0, 0, 0)),
                  pl.BlockSpec(k2.shape, lambda i: (0, 0, 0))],
        out_specs=[pl.BlockSpec((n_grp, PEER_HEADS * n_i, tm), lambda i: (0, 0, i)),
                   pl.BlockSpec((PEER_HEADS, N_KEYS, tm), lambda i: (0, 0, i)),
                   pl.BlockSpec((2 * PEER_HEADS, tm), lambda i: (0, i))],
        out_shape=[jax.ShapeDtypeStruct((n_grp, PEER_HEADS * n_i, rows), F32),
                   jax.ShapeDtypeStruct((PEER_HEADS, N_KEYS, rows), F32),
                   jax.ShapeDtypeStruct((2 * PEER_HEADS, rows), F32)],
        scratch_shapes=[pltpu.VMEM((PEER_TOPK, tm), F32), pltpu.VMEM((PEER_TOPK, tm), F32),
                        pltpu.VMEM((N_CAND, tm), F32)],
        compiler_params=_cparams("parallel"),
        name="peer_scores",
    )(h2, wq, k1, k2)


def _gelu(x):
    return (0.5 * x) * (1.0 + lax.erf(x * (0.5 ** 0.5)))


def _peer_mix_kernel(ht_ref, u_ref, vt_ref, s1_ref, s2_ref, st_ref, x_ref, mod_ref, *rest,
                     tt, te, n_lat, seq, n_batch, k_gate, final_norm):
    if final_norm:
        fg_ref, o_ref, acc_ref, act_scr, g_scr, w_scr = rest
    else:
        o_ref, acc_ref, act_scr, g_scr, w_scr = rest
    j = pl.program_id(1)
    last = pl.num_programs(1) - 1
    d = x_ref.shape[1]
    n_i = te // N_KEYS

    @pl.when(jnp.logical_and(pl.program_id(0) == 0, j == 0))
    def _():
        acc_ref[...] = jnp.zeros_like(acc_ref)

    act_scr[j % 2] = _gelu(jnp.dot(u_ref[...], ht_ref[...], preferred_element_type=F32))

    for ii in range(n_i):
        rs = slice(ii * N_KEYS, (ii + 1) * N_KEYS)
        for tc in range(tt // LANES):
            cs = slice(tc * LANES, (tc + 1) * LANES)
            gate = None
            for h in range(PEER_HEADS):
                s = s2_ref[h, :, cs] + s1_ref[0, h * n_i + ii:h * n_i + ii + 1, cs]
                tau = st_ref[h:h + 1, cs]
                lognorm = st_ref[PEER_HEADS + h:PEER_HEADS + h + 1, cs]
                wgt = jnp.where(s >= tau, jnp.exp2(s - lognorm), 0.0)
                gate = wgt if gate is None else gate + wgt
            g_scr[rs, cs] = gate

    prev = act_scr[(j + 1) % 2]
    w_scr[...] = jnp.where(j >= 1, g_scr[...] * prev, 0.0).astype(w_scr.dtype)
    acc_ref[...] += jnp.dot(vt_ref[...], w_scr[...], preferred_element_type=F32)

    @pl.when(j == last)
    def _():
        b = _mod_row(pl.program_id(0) * tt, n_lat, seq, n_batch)
        xn = x_ref[...] + _mod_vec(mod_ref, b, k_gate, d) * acc_ref[...].T
        if final_norm:
            xn = xn * lax.rsqrt(jnp.mean(xn * xn, axis=-1, keepdims=True) + EPS) * fg_ref[...]
        o_ref[...] = xn
        acc_ref[...] = jnp.zeros_like(acc_ref)


PEER_TE = 512


def _peer_mix(h2t, u, vt, s1, s2, st, x, mod, final_g, *, rows, n_lat, seq, n_batch, k_gate,
              tt=512, te=PEER_TE):
    d = x.shape[1]
    n_tiles = u.shape[0] // te
    final_norm = final_g is not None
    n_i = te // N_KEYS

    def cur(j):
        return jnp.minimum(j, n_tiles - 1)

    def prv(j):
        return jnp.maximum(j - 1, 0)

    in_specs = [pl.BlockSpec((d, tt), lambda t, j: (0, t)),
                pl.BlockSpec((te, d), lambda t, j: (cur(j), 0)),
                pl.BlockSpec((d, te), lambda t, j: (0, prv(j))),
                pl.BlockSpec((1, PEER_HEADS * n_i, tt), lambda t, j: (prv(j), 0, t)),
                pl.BlockSpec((PEER_HEADS, N_KEYS, tt), lambda t, j: (0, 0, t)),
                pl.BlockSpec((2 * PEER_HEADS, tt), lambda t, j: (0, t)),
                pl.BlockSpec((tt, d), lambda t, j: (t, 0)),
                pl.BlockSpec(mod.shape, lambda t, j: (0, 0))]
    args = [h2t, u, vt, s1, s2, st, x, mod]
    if final_norm:
        in_specs.append(pl.BlockSpec((1, d), lambda t, j: (0, 0)))
        args.append(final_g.reshape(1, d))
    return pl.pallas_call(
        functools.partial(_peer_mix_kernel, tt=tt, te=te, n_lat=n_lat, seq=seq, n_batch=n_batch,
                          k_gate=k_gate, final_norm=final_norm),
        grid=(rows // tt, n_tiles + 1),
        in_specs=in_specs,
        out_specs=pl.BlockSpec((tt, d), lambda t, j: (t, 0)),
        out_shape=jax.ShapeDtypeStruct((rows, d), F32),
        scratch_shapes=[pltpu.VMEM((d, tt), F32), pltpu.VMEM((2, te, tt), F32), pltpu.VMEM((te, tt), F32),
                        pltpu.VMEM((te, tt), BF16)],
        compiler_params=_cparams("arbitrary", "arbitrary"),
        name="peer_mix",
    )(*args)


def _rope_tables(seq, pad_rows):
    half = HEAD_DIM // 2
    t = jnp.arange(seq)
    row = (t // GRID_W).astype(F32)
    col = (t % GRID_W).astype(F32)
    inv = ROPE_THETA ** (-jnp.arange(0, half, 2, dtype=F32) / half)
    ang_r = row[:, None] * inv[None, :]
    ang_c = col[:, None] * inv[None, :]
    cos = jnp.concatenate([jnp.cos(ang_r), jnp.cos(ang_r), jnp.cos(ang_c), jnp.cos(ang_c)], axis=-1)
    sin = jnp.concatenate([-jnp.sin(ang_r), jnp.sin(ang_r), -jnp.sin(ang_c), jnp.sin(ang_c)], axis=-1)
    cos = jnp.concatenate([cos, jnp.ones((pad_rows, HEAD_DIM), F32)], axis=0)
    sin = jnp.concatenate([sin, jnp.zeros((pad_rows, HEAD_DIM), F32)], axis=0)
    return cos, sin


def kernel(x, c, ctx, c_ctx, ada_w, ada_b, norm1_g, norm2_g, w_in, b_gate, q_norm_g, k_norm_g,
           conv_w, w_attn_out, w_conv_out, w_o, peer_wq, peer_k1, peer_k2, peer_u, peer_v, final_g):
    n_batch, seq, d = x.shape
    ctx_len = ctx.shape[1]
    depth = ada_w.shape[0]
    n_lat = n_batch * seq
    n_all = n_lat + n_batch * ctx_len
    assert seq & (seq - 1) == 0 and ctx_len & (ctx_len - 1) == 0
    tm = 512

    xa = jnp.concatenate([x.reshape(n_lat, d), ctx.reshape(n_batch * ctx_len, d)], axis=0)
    pad = (-(n_batch + 1)) % 8
    cvec = jnp.concatenate([c, c_ctx[None, :], jnp.zeros((pad, d), F32)], axis=0)
    mod_all = _adaln(cvec, ada_w, ada_b)
    cos_t, sin_t = _rope_tables(seq, tm)
    geom = dict(n_lat=n_lat, seq=seq, n_batch=n_batch)

    for l in range(depth):
        last = l == depth - 1
        rows = n_lat if last else n_all
        mod = mod_all[l]
        w_in_l = w_in[l].astype(BF16)

        h = _norm_mod(xa, norm1_g[l], mod, rows=n_all, k_shift=0, k_scale=1, **geom)
        q, k, v, z, gb, gates = _in_proj(h, w_in_l, q_norm_g[l], k_norm_g[l], b_gate[l], cos_t, sin_t,
                                         rows_q=rows, rows_kv=n_all, n_lat=n_lat, seq=seq, d=d, tm=tm)
        attn = _attention(q, k, v, rows_q=rows, n_lat=n_lat, seq=seq, ctx_len=ctx_len)
        merged = _merge(attn, z, gb, gates, conv_w[l], w_attn_out[l].astype(BF16),
                        w_conv_out[l].astype(BF16), rows=rows, n_lat=n_lat, seq=seq, ctx_len=ctx_len)
        xa = _resid_proj(merged, w_o[l].astype(BF16), xa, mod, rows=rows, k_gate=2, **geom)

        h2, h2t = _norm_mod(xa, norm2_g[l], mod, rows=rows, k_shift=3, k_scale=4, with_transpose=True, **geom)
        s1, s2, st = _peer_scores(h2, peer_wq[l].astype(BF16), peer_k1[l], peer_k2[l], rows=rows,
                                  n_i=PEER_TE // N_KEYS)
        xa = _peer_mix(h2t, peer_u[l].astype(BF16), peer_v[l].T.astype(BF16), s1, s2, st, xa, mod,
                       final_g if last else None, rows=rows, k_gate=5, **geom)
    return xa.reshape(n_batch, seq, d)
```

```python
import functools

import jax
import jax.numpy as jnp
from jax import lax
from jax.experimental import pallas as pl
from jax.experimental.pallas import tpu as pltpu

F32 = jnp.float32
BF16 = jnp.bfloat16

HEAD_DIM = 128
KV_GROUPS = 4
GRID_W = 64
ROPE_THETA = 10000.0
PEER_HEADS = 8
N_KEYS = 128
PEER_TOPK = 16
N_MOD = 6
EPS = 1e-6
LOG2_E = 1.4426950408889634
LANES = 128
BF16_SUBLANES = 16
VMEM_LIMIT_BYTES = 56 * 1024 * 1024

_NT = (((1,), (1,)), ((), ()))


def _cparams(*sem, flags=None):
    return pltpu.CompilerParams(dimension_semantics=sem, vmem_limit_bytes=VMEM_LIMIT_BYTES, flags=flags)


def _mod_row(row0, n_lat, seq, n_batch):
    return jnp.where(row0 < n_lat, row0 // seq, n_batch)


def _mod_vec(mod_ref, b, k, d):
    return mod_ref[pl.ds(b, 1), k * d:(k + 1) * d]


def _adaln_kernel(c_ref, w_ref, b_ref, o_ref):
    cv = c_ref[...]
    s = cv * jax.nn.sigmoid(cv)
    o_ref[0] = jnp.dot(s, w_ref[0], preferred_element_type=F32) + b_ref[0]


def _adaln(cvec, ada_w, ada_b, tn=1024):
    depth, d, n = ada_w.shape
    rows = cvec.shape[0]
    return pl.pallas_call(
        _adaln_kernel,
        grid=(depth, n // tn),
        in_specs=[pl.BlockSpec((rows, d), lambda l, j: (0, 0)),
                  pl.BlockSpec((1, d, tn), lambda l, j: (l, 0, j)),
                  pl.BlockSpec((1, 1, tn), lambda l, j: (l, 0, j))],
        out_specs=pl.BlockSpec((1, rows, tn), lambda l, j: (l, 0, j)),
        out_shape=jax.ShapeDtypeStruct((depth, rows, n), F32),
        compiler_params=_cparams("arbitrary", "arbitrary"),
        name="adaln",
    )(cvec, ada_w, ada_b.reshape(depth, 1, n))


def _norm_mod_kernel(x_ref, g_ref, mod_ref, o_ref, *maybe_ot_ref, tm, n_lat, seq, n_batch, k_shift, k_scale):
    d = x_ref.shape[1]
    b = _mod_row(pl.program_id(0) * tm, n_lat, seq, n_batch)
    xf = x_ref[...]
    y = xf * lax.rsqrt(jnp.mean(xf * xf, axis=-1, keepdims=True) + EPS)
    h = y * g_ref[...]
    shift = _mod_vec(mod_ref, b, k_shift, d)
    scale = _mod_vec(mod_ref, b, k_scale, d)
    h = h * (1.0 + scale) + shift
    o_ref[...] = h.astype(o_ref.dtype)
    for ot_ref in maybe_ot_ref:
        ot_ref[...] = h.T.astype(ot_ref.dtype)


def _norm_mod(x, g, mod, *, rows, n_lat, seq, n_batch, k_shift, k_scale, with_transpose=False, tm=512):
    d = x.shape[1]
    kern = functools.partial(_norm_mod_kernel, tm=tm, n_lat=n_lat, seq=seq, n_batch=n_batch,
                             k_shift=k_shift, k_scale=k_scale)
    out_specs = [pl.BlockSpec((tm, d), lambda i: (i, 0))]
    out_shape = [jax.ShapeDtypeStruct((rows, d), BF16)]
    if with_transpose:
        out_specs.append(pl.BlockSpec((d, tm), lambda i: (0, i)))
        out_shape.append(jax.ShapeDtypeStruct((d, rows), BF16))
    out = pl.pallas_call(
        kern,
        grid=(rows // tm,),
        in_specs=[pl.BlockSpec((tm, d), lambda i: (i, 0)),
                  pl.BlockSpec((1, d), lambda i: (0, 0)),
                  pl.BlockSpec(mod.shape, lambda i: (0, 0))],
        out_specs=out_specs,
        out_shape=out_shape,
        compiler_params=_cparams("parallel"),
        name="norm_mod",
    )(x, g.reshape(1, d), mod)
    return out if with_transpose else out[0]


def _head_norm_rope(a, g, cos, sin, out_scale):
    y = a * lax.rsqrt(jnp.mean(a * a, axis=-1, keepdims=True) + EPS) * g
    lane = lax.broadcasted_iota(jnp.int32, y.shape, 1)
    lower = (lane & (HEAD_DIM // 2 - 1)) < HEAD_DIM // 4
    partner = jnp.where(lower, pltpu.roll(y, HEAD_DIM - HEAD_DIM // 4, 1),
                        pltpu.roll(y, HEAD_DIM // 4, 1))
    r = y * cos + partner * sin
    if out_scale != 1.0:
        r = r * out_scale
    return r


def _q_kernel(h_ref, w_ref, g_ref, cos_ref, sin_ref, o_ref, *, out_scale):
    acc = jnp.dot(h_ref[...], w_ref[...], preferred_element_type=F32)
    g, cos, sin = g_ref[...], cos_ref[...], sin_ref[...]
    for hh in range(acc.shape[1] // HEAD_DIM):
        sl = slice(hh * HEAD_DIM, (hh + 1) * HEAD_DIM)
        o_ref[:, sl] = _head_norm_rope(acc[:, sl], g, cos, sin, out_scale).astype(o_ref.dtype)


def _kv_kernel(h_ref, wk_ref, wv_ref, g_ref, cos_ref, sin_ref, k_ref, v_ref):
    h = h_ref[...]
    acc = jnp.dot(h, wk_ref[...], preferred_element_type=F32)
    g, cos, sin = g_ref[...], cos_ref[...], sin_ref[...]
    for hh in range(acc.shape[1] // HEAD_DIM):
        sl = slice(hh * HEAD_DIM, (hh + 1) * HEAD_DIM)
        k_ref[:, sl] = _head_norm_rope(acc[:, sl], g, cos, sin, 1.0).astype(k_ref.dtype)
    vacc = jnp.dot(h, wv_ref[...], preferred_element_type=F32).astype(v_ref.dtype)
    ones = jnp.ones((vacc.shape[0], HEAD_DIM), v_ref.dtype)
    for g in range(vacc.shape[1] // HEAD_DIM):
        v_ref[:, 2 * g * HEAD_DIM:(2 * g + 1) * HEAD_DIM] = vacc[:, g * HEAD_DIM:(g + 1) * HEAD_DIM]
        v_ref[:, (2 * g + 1) * HEAD_DIM:(2 * g + 2) * HEAD_DIM] = ones


def _conv_in_kernel(h_ref, wu_ref, wb_ref, wc_ref, z_ref, gb_ref):
    h = h_ref[...]
    u = jnp.dot(h, wu_ref[...], preferred_element_type=F32)
    gc = jnp.dot(h, wc_ref[...], preferred_element_type=F32)
    z_ref[...] = (gc * u).astype(z_ref.dtype)
    gb_ref[...] = jnp.dot(h, wb_ref[...], preferred_element_type=F32).astype(gb_ref.dtype)


def _gate_kernel(h_ref, w_ref, b_ref, o_ref):
    acc = jnp.dot(h_ref[...], w_ref[...], preferred_element_type=F32)
    o_ref[...] = jax.nn.sigmoid(acc + b_ref[...]).astype(o_ref.dtype)


def _rope_block(i, tm, n_lat, seq):
    return jnp.where(i * tm < n_lat, (i * tm % seq) // tm, seq // tm)


def _in_proj(h, w_in, q_g, k_g, b_gate, cos_t, sin_t, *, rows_q, rows_kv, n_lat, seq, d,
             tm=512, tn=512):
    kd = h.shape[1]
    kvw = d // KV_GROUPS
    q_end, k_end, v_end = d, d + kvw, d + 2 * kvw
    u_end, b_end, c_end = v_end + d, v_end + 2 * d, v_end + 3 * d
    hspec = pl.BlockSpec((tm, kd), lambda i, j: (i, 0))
    hd_spec = pl.BlockSpec((1, HEAD_DIM), lambda i, j: (0, 0))
    tab_spec = pl.BlockSpec((tm, HEAD_DIM), lambda i, j: (_rope_block(i, tm, n_lat, seq), 0))

    def wspec(col0):
        return pl.BlockSpec((kd, tn), lambda i, j: (0, col0 // tn + j))

    def ospec():
        return pl.BlockSpec((tm, tn), lambda i, j: (i, j))

    q = pl.pallas_call(
        functools.partial(_q_kernel, out_scale=HEAD_DIM ** -0.5 * LOG2_E),
        grid=(rows_q // tm, d // tn),
        in_specs=[hspec, wspec(0), hd_spec, tab_spec, tab_spec],
        out_specs=ospec(),
        out_shape=jax.ShapeDtypeStruct((rows_q, d), BF16),
        compiler_params=_cparams("parallel", "arbitrary"),
        name="proj_q",
    )(h, w_in, q_g.reshape(1, HEAD_DIM), cos_t, sin_t)

    assert kvw == tn
    k, v = pl.pallas_call(
        _kv_kernel,
        grid=(rows_kv // tm, 1),
        in_specs=[hspec, wspec(q_end), wspec(k_end), hd_spec, tab_spec, tab_spec],
        out_specs=[ospec(), pl.BlockSpec((tm, 2 * tn), lambda i, j: (i, j))],
        out_shape=[jax.ShapeDtypeStruct((rows_kv, kvw), BF16), jax.ShapeDtypeStruct((rows_kv, 2 * kvw), BF16)],
        compiler_params=_cparams("parallel", "arbitrary"),
        name="proj_kv",
    )(h, w_in, w_in, k_g.reshape(1, HEAD_DIM), cos_t, sin_t)

    z, gb = pl.pallas_call(
        _conv_in_kernel,
        grid=(rows_q // tm, d // tn),
        in_specs=[hspec, wspec(v_end), wspec(u_end), wspec(b_end)],
        out_specs=[ospec(), ospec()],
        out_shape=[jax.ShapeDtypeStruct((rows_q, d), BF16)] * 2,
        compiler_params=_cparams("parallel", "arbitrary"),
        name="proj_conv",
    )(h, w_in, w_in, w_in)

    gates = pl.pallas_call(
        _gate_kernel,
        grid=(rows_q // tm, 2 * d // tn),
        in_specs=[hspec, wspec(c_end), pl.BlockSpec((1, tn), lambda i, j: (0, j))],
        out_specs=ospec(),
        out_shape=jax.ShapeDtypeStruct((rows_q, 2 * d), BF16),
        compiler_params=_cparams("parallel", "arbitrary"),
        name="proj_gates",
    )(h, w_in, b_gate.reshape(1, 2 * d))
    return q, k, v, z, gb, gates


def _attn_kernel(q_ref, kl_ref, vl_ref, kc_ref, vc_ref, o_ref, *, tq, n_lat):
    is_lat = pl.program_id(1) * tq < n_lat

    def run(with_lat):
        kc, vc = kc_ref[...], vc_ref[...]
        for hh in range(KV_GROUPS):
            sl = slice(hh * HEAD_DIM, (hh + 1) * HEAD_DIM)
            q = q_ref[:, sl]
            sc = lax.dot_general(q, kc, _NT, preferred_element_type=F32)
            m = jnp.max(sc, axis=-1, keepdims=True)
            if with_lat:
                s_lat = lax.dot_general(q, kl_ref[...], _NT, preferred_element_type=F32)
                m = jnp.maximum(m, jnp.max(s_lat, axis=-1, keepdims=True))
            o = jnp.dot(jnp.exp2(sc - m).astype(vc.dtype), vc, preferred_element_type=F32)
            if with_lat:
                o = o + jnp.dot(jnp.exp2(s_lat - m).astype(vc.dtype), vl_ref[...],
                                preferred_element_type=F32)
            o_ref[:, sl] = (o[:, :HEAD_DIM] / o[:, HEAD_DIM:]).astype(o_ref.dtype)

    @pl.when(is_lat)
    def _():
        run(True)

    @pl.when(jnp.logical_not(is_lat))
    def _():
        run(False)


def _attention(q, k, v, *, rows_q, n_lat, seq, ctx_len, tq=256):
    assert ctx_len % tq == 0 or tq % ctx_len == 0
    assert tq <= ctx_len
    gw = KV_GROUPS * HEAD_DIM
    n_kv = k.shape[1] // HEAD_DIM

    def batch(i):
        return jnp.where(i * tq < n_lat, i * tq // seq, (i * tq - n_lat) // ctx_len)

    qspec = pl.BlockSpec((tq, gw), lambda g, i: (i, g))

    def lat_spec(width):
        return pl.BlockSpec((seq, width), lambda g, i: (batch(i), g))

    def ctx_spec(width):
        return pl.BlockSpec((ctx_len, width), lambda g, i: (n_lat // ctx_len + batch(i), g))

    return pl.pallas_call(
        functools.partial(_attn_kernel, tq=tq, n_lat=n_lat),
        grid=(n_kv, rows_q // tq),
        in_specs=[qspec, lat_spec(HEAD_DIM), lat_spec(2 * HEAD_DIM), ctx_spec(HEAD_DIM), ctx_spec(2 * HEAD_DIM)],
        out_specs=qspec,
        out_shape=jax.ShapeDtypeStruct((rows_q, q.shape[1]), BF16),
        compiler_params=_cparams("parallel", "arbitrary"),
        name="attention",
    )(q, k, v, k, v)


def _merge_kernel(attn_ref, z_ref, zp_ref, zn_ref, gb_ref, ga_ref, gc_ref, cw_ref,
                  wao_ref, wco_ref, o_ref, y_scr, *, tm, n_lat, seq, ctx_len):
    @pl.when(pl.program_id(1) == 0)
    def _():
        z = z_ref[...].astype(F32)
        loc = lax.broadcasted_iota(jnp.int32, (tm, 1), 0)
        row = pl.program_id(0) * tm + loc
        seqlen = jnp.where(row < n_lat, seq, ctx_len)
        pos = row & (seqlen - 1)
        halo_prev = zp_ref[...].astype(F32)[BF16_SUBLANES - 1:BF16_SUBLANES, :]
        halo_next = zn_ref[...].astype(F32)[0:1, :]
        z_prev = jnp.where(loc == 0, halo_prev, pltpu.roll(z, 1, 0))
        z_prev = jnp.where(pos == 0, 0.0, z_prev)
        z_next = jnp.where(loc == tm - 1, halo_next, pltpu.roll(z, tm - 1, 0))
        z_next = jnp.where(pos == seqlen - 1, 0.0, z_next)
        cw = cw_ref[...]
        conv = cw[0:1, :] * z_prev + cw[1:2, :] * z + cw[2:3, :] * z_next
        y_scr[...] = (gb_ref[...].astype(F32) * conv).astype(y_scr.dtype)

    acc_a = jnp.dot(attn_ref[...], wao_ref[...], preferred_element_type=F32)
    acc_c = jnp.dot(y_scr[...], wco_ref[...], preferred_element_type=F32)
    o_ref[...] = (ga_ref[...].astype(F32) * acc_a + gc_ref[...].astype(F32) * acc_c).astype(o_ref.dtype)


def _merge(attn, z, gb, gates, conv_w, w_ao, w_co, *, rows, n_lat, seq, ctx_len, tm=512, tn=512):
    d = attn.shape[1]
    hb = BF16_SUBLANES
    n_hblk = z.shape[0] // hb
    row_spec = pl.BlockSpec((tm, d), lambda i, j: (i, 0))
    prev_spec = pl.BlockSpec((hb, d), lambda i, j: (jnp.maximum(i * (tm // hb) - 1, 0), 0))
    next_spec = pl.BlockSpec((hb, d), lambda i, j: (jnp.minimum((i + 1) * (tm // hb), n_hblk - 1), 0))
    wspec = pl.BlockSpec((d, tn), lambda i, j: (0, j))
    return pl.pallas_call(
        functools.partial(_merge_kernel, tm=tm, n_lat=n_lat, seq=seq, ctx_len=ctx_len),
        grid=(rows // tm, d // tn),
        in_specs=[row_spec, row_spec, prev_spec, next_spec, row_spec,
                  pl.BlockSpec((tm, tn), lambda i, j: (i, j)),
                  pl.BlockSpec((tm, tn), lambda i, j: (i, d // tn + j)),
                  pl.BlockSpec(conv_w.shape, lambda i, j: (0, 0)),
                  wspec, wspec],
        out_specs=pl.BlockSpec((tm, tn), lambda i, j: (i, j)),
        out_shape=jax.ShapeDtypeStruct((rows, d), BF16),
        scratch_shapes=[pltpu.VMEM((tm, d), BF16)],
        compiler_params=_cparams("parallel", "arbitrary"),
        name="merge",
    )(attn, z, z, z, gb, gates, gates, conv_w, w_ao, w_co)


def _resid_proj_kernel(a_ref, w_ref, x_ref, mod_ref, o_ref, *, tm, tn, n_lat, seq, n_batch, k_gate, d):
    b = _mod_row(pl.program_id(0) * tm, n_lat, seq, n_batch)
    col0 = pl.multiple_of(k_gate * d + pl.program_id(1) * tn, LANES)
    gate = mod_ref[pl.ds(b, 1), pl.ds(col0, tn)]
    acc = jnp.dot(a_ref[...], w_ref[...], preferred_element_type=F32)
    o_ref[...] = x_ref[...] + gate * acc


def _resid_proj(a, w, x, mod, *, rows, n_lat, seq, n_batch, k_gate, tm=512, tn=512):
    d = a.shape[1]
    return pl.pallas_call(
        functools.partial(_resid_proj_kernel, tm=tm, tn=tn, n_lat=n_lat, seq=seq, n_batch=n_batch,
                          k_gate=k_gate, d=d),
        grid=(rows // tm, d // tn),
        in_specs=[pl.BlockSpec((tm, d), lambda i, j: (i, 0)),
                  pl.BlockSpec((d, tn), lambda i, j: (0, j)),
                  pl.BlockSpec((tm, tn), lambda i, j: (i, j)),
                  pl.BlockSpec(mod.shape, lambda i, j: (0, 0))],
        out_specs=pl.BlockSpec((tm, tn), lambda i, j: (i, j)),
        out_shape=jax.ShapeDtypeStruct((rows, d), F32),
        compiler_params=_cparams("parallel", "arbitrary"),
        name="resid_proj",
    )(a, w, x, mod)


N_TOP = PEER_TOPK + 1
TOP_ROWS = 24
N_CAND = 96


def _top_values(s, k, out_ref):
    cur = s
    prev = mx = None
    for r in range(k):
        prev = mx
        mx = jnp.max(cur, axis=0, keepdims=True)
        if out_ref is not None:
            out_ref[r:r + 1, :] = mx
        cur = jnp.where(cur == mx, -jnp.inf, cur)
    return prev, mx


def _candidate_sums(v1_ref, v2_ref, cand_ref):
    cand_ref[0:TOP_ROWS, :] = v1_ref[0:1, :] + v2_ref[...]
    for a in range(1, 8):
        cand_ref[16 + 8 * a:24 + 8 * a, :] = v1_ref[a:a + 1, :] + v2_ref[0:8, :]
    cand_ref[80:96, :] = v1_ref[8:TOP_ROWS, :] + v2_ref[0:1, :]


def _peer_score_kernel(h_ref, wq_ref, k1_ref, k2_ref, th_ref, e1_ref, e2_ref,
                       v1_scr, v2_scr, cand_scr, *, n_i):
    qp = jnp.dot(h_ref[...], wq_ref[...], preferred_element_type=F32)
    pad = jnp.full((TOP_ROWS - N_TOP, v1_scr.shape[1]), -jnp.inf, F32)
    v1_scr[N_TOP:TOP_ROWS, :] = pad
    v2_scr[N_TOP:TOP_ROWS, :] = pad
    for h in range(PEER_HEADS):
        q1 = qp[:, (2 * h) * N_KEYS:(2 * h + 1) * N_KEYS]
        q2 = qp[:, (2 * h + 1) * N_KEYS:(2 * h + 2) * N_KEYS]
        s1 = lax.dot_general(k1_ref[h], q1, _NT, preferred_element_type=F32) * LOG2_E
        s2 = lax.dot_general(k2_ref[h], q2, _NT, preferred_element_type=F32) * LOG2_E
        _top_values(s1, N_TOP, v1_scr)
        _top_values(s2, N_TOP, v2_scr)
        _candidate_sums(v1_scr, v2_scr, cand_scr)
        cand = cand_scr[...]
        c16, c17 = _top_values(cand, N_TOP, None)
        tau = 0.5 * (c16 + c17)
        top1 = v1_scr[0:1, :]
        top2 = v2_scr[0:1, :]
        z = jnp.sum(jnp.where(cand >= c16, jnp.exp2(cand - (top1 + top2)), 0.0), axis=0, keepdims=True)
        theta = jnp.exp2(tau - s1 - top2)
        e1 = jnp.exp2(s1 - (top1 + jnp.log2(z)))
        for g in range(N_KEYS // n_i):
            th_ref[g, h * n_i:(h + 1) * n_i, :] = theta[g * n_i:(g + 1) * n_i, :]
            e1_ref[g, h * n_i:(h + 1) * n_i, :] = e1[g * n_i:(g + 1) * n_i, :]
        e2_ref[h] = jnp.exp2(s2 - top2)


def _peer_scores(h2, wq, k1, k2, *, rows, n_i, tm=256):
    d = h2.shape[1]
    n_grp = N_KEYS // n_i
    grouped = pl.BlockSpec((n_grp, PEER_HEADS * n_i, tm), lambda i: (0, 0, i))
    by_head = pl.BlockSpec((PEER_HEADS, N_KEYS, tm), lambda i: (0, 0, i))
    grouped_shape = jax.ShapeDtypeStruct((n_grp, PEER_HEADS * n_i, rows), F32)
    by_head_shape = jax.ShapeDtypeStruct((PEER_HEADS, N_KEYS, rows), F32)
    return pl.pallas_call(
        functools.partial(_peer_score_kernel, n_i=n_i),
        grid=(rows // tm,),
        in_specs=[pl.BlockSpec((tm, d), lambda i: (i, 0)),
                  pl.BlockSpec(wq.shape, lambda i: (0, 0)),
                  pl.BlockSpec(k1.shape, lambda i: (0, 0, 0)),
                  pl.BlockSpec(k2.shape, lambda i: (0, 0, 0))],
        out_specs=[grouped, grouped, by_head],
        out_shape=[grouped_shape, grouped_shape, by_head_shape],
        scratch_shapes=[pltpu.VMEM((TOP_ROWS, tm), F32), pltpu.VMEM((TOP_ROWS, tm), F32),
                        pltpu.VMEM((N_CAND, tm), F32)],
        compiler_params=_cparams("parallel"),
        name="peer_scores",
    )(h2, wq, k1, k2)


def _gelu(x):
    return (0.5 * x) * (1.0 + lax.erf(x * (0.5 ** 0.5)))


def _peer_mix_kernel(ht_ref, u_ref, vt_ref, th_ref, e1_ref, e2_ref, x_ref, mod_ref, *rest,
                     tt, te, n_lat, seq, n_batch, k_gate, final_norm):
    if final_norm:
        fg_ref, o_ref, acc_ref, act_scr, g_scr, w_scr = rest
    else:
        o_ref, acc_ref, act_scr, g_scr, w_scr = rest
    j = pl.program_id(1)
    last = pl.num_programs(1) - 1
    d = x_ref.shape[1]
    n_i = te // N_KEYS

    @pl.when(jnp.logical_and(pl.program_id(0) == 0, j == 0))
    def _():
        acc_ref[...] = jnp.zeros_like(acc_ref)

    act_scr[j % 2] = _gelu(jnp.dot(u_ref[...], ht_ref[...], preferred_element_type=F32))

    for ii in range(n_i):
        rs = slice(ii * N_KEYS, (ii + 1) * N_KEYS)
        for tc in range(tt // LANES):
            cs = slice(tc * LANES, (tc + 1) * LANES)
            gate = None
            for h in range(PEER_HEADS):
                row = slice(h * n_i + ii, h * n_i + ii + 1)
                e2 = e2_ref[h, :, cs]
                wgt = jnp.where(e2 >= th_ref[0, row, cs], e2, 0.0) * e1_ref[0, row, cs]
                gate = wgt if gate is None else gate + wgt
            g_scr[rs, cs] = gate

    prev = act_scr[(j + 1) % 2]
    w_scr[...] = jnp.where(j >= 1, g_scr[...] * prev, 0.0).astype(w_scr.dtype)
    acc_ref[...] += jnp.dot(vt_ref[...], w_scr[...], preferred_element_type=F32)

    @pl.when(j == last)
    def _():
        b = _mod_row(pl.program_id(0) * tt, n_lat, seq, n_batch)
        xn = x_ref[...] + _mod_vec(mod_ref, b, k_gate, d) * acc_ref[...].T
        if final_norm:
            xn = xn * lax.rsqrt(jnp.mean(xn * xn, axis=-1, keepdims=True) + EPS) * fg_ref[...]
        o_ref[...] = xn
        acc_ref[...] = jnp.zeros_like(acc_ref)


PEER_TE = 512


def _peer_mix(h2t, u, vt, th, e1, e2, x, mod, final_g, *, rows, n_lat, seq, n_batch, k_gate,
              tt=512, te=PEER_TE):
    d = x.shape[1]
    n_tiles = u.shape[0] // te
    final_norm = final_g is not None
    n_i = te // N_KEYS

    def cur(j):
        return jnp.minimum(j, n_tiles - 1)

    def prv(j):
        return jnp.maximum(j - 1, 0)

    grouped = pl.BlockSpec((1, PEER_HEADS * n_i, tt), lambda t, j: (prv(j), 0, t))
    by_head = pl.BlockSpec((PEER_HEADS, N_KEYS, tt), lambda t, j: (0, 0, t))
    in_specs = [pl.BlockSpec((d, tt), lambda t, j: (0, t)),
                pl.BlockSpec((te, d), lambda t, j: (cur(j), 0)),
                pl.BlockSpec((d, te), lambda t, j: (0, prv(j))),
                grouped, grouped, by_head,
                pl.BlockSpec((tt, d), lambda t, j: (t, 0)),
                pl.BlockSpec(mod.shape, lambda t, j: (0, 0))]
    args = [h2t, u, vt, th, e1, e2, x, mod]
    if final_norm:
        in_specs.append(pl.BlockSpec((1, d), lambda t, j: (0, 0)))
        args.append(final_g.reshape(1, d))
    return pl.pallas_call(
        functools.partial(_peer_mix_kernel, tt=tt, te=te, n_lat=n_lat, seq=seq, n_batch=n_batch,
                          k_gate=k_gate, final_norm=final_norm),
        grid=(rows // tt, n_tiles + 1),
        in_specs=in_specs,
        out_specs=pl.BlockSpec((tt, d), lambda t, j: (t, 0)),
        out_shape=jax.ShapeDtypeStruct((rows, d), F32),
        scratch_shapes=[pltpu.VMEM((d, tt), F32), pltpu.VMEM((2, te, tt), F32), pltpu.VMEM((te, tt), F32),
                        pltpu.VMEM((te, tt), BF16)],
        compiler_params=_cparams("arbitrary", "arbitrary"),
        name="peer_mix",
    )(*args)


def _rope_tables(seq, pad_rows):
    half = HEAD_DIM // 2
    t = jnp.arange(seq)
    row = (t // GRID_W).astype(F32)
    col = (t % GRID_W).astype(F32)
    inv = ROPE_THETA ** (-jnp.arange(0, half, 2, dtype=F32) / half)
    ang_r = row[:, None] * inv[None, :]
    ang_c = col[:, None] * inv[None, :]
    cos = jnp.concatenate([jnp.cos(ang_r), jnp.cos(ang_r), jnp.cos(ang_c), jnp.cos(ang_c)], axis=-1)
    sin = jnp.concatenate([-jnp.sin(ang_r), jnp.sin(ang_r), -jnp.sin(ang_c), jnp.sin(ang_c)], axis=-1)
    cos = jnp.concatenate([cos, jnp.ones((pad_rows, HEAD_DIM), F32)], axis=0)
    sin = jnp.concatenate([sin, jnp.zeros((pad_rows, HEAD_DIM), F32)], axis=0)
    return cos, sin


def kernel(x, c, ctx, c_ctx, ada_w, ada_b, norm1_g, norm2_g, w_in, b_gate, q_norm_g, k_norm_g,
           conv_w, w_attn_out, w_conv_out, w_o, peer_wq, peer_k1, peer_k2, peer_u, peer_v, final_g):
    n_batch, seq, d = x.shape
    ctx_len = ctx.shape[1]
    depth = ada_w.shape[0]
    n_lat = n_batch * seq
    n_all = n_lat + n_batch * ctx_len
    assert seq & (seq - 1) == 0 and ctx_len & (ctx_len - 1) == 0
    tm = 512

    xa = jnp.concatenate([x.reshape(n_lat, d), ctx.reshape(n_batch * ctx_len, d)], axis=0)
    pad = (-(n_batch + 1)) % 8
    cvec = jnp.concatenate([c, c_ctx[None, :], jnp.zeros((pad, d), F32)], axis=0)
    mod_all = _adaln(cvec, ada_w, ada_b)
    cos_t, sin_t = _rope_tables(seq, tm)
    geom = dict(n_lat=n_lat, seq=seq, n_batch=n_batch)

    for l in range(depth):
        last = l == depth - 1
        rows = n_lat if last else n_all
        mod = mod_all[l]
        w_in_l = w_in[l].astype(BF16)

        h = _norm_mod(xa, norm1_g[l], mod, rows=n_all, k_shift=0, k_scale=1, **geom)
        q, k, v, z, gb, gates = _in_proj(h, w_in_l, q_norm_g[l], k_norm_g[l], b_gate[l], cos_t, sin_t,
                                         rows_q=rows, rows_kv=n_all, n_lat=n_lat, seq=seq, d=d, tm=tm)
        attn = _attention(q, k, v, rows_q=rows, n_lat=n_lat, seq=seq, ctx_len=ctx_len)
        merged = _merge(attn, z, gb, gates, conv_w[l], w_attn_out[l].astype(BF16),
                        w_conv_out[l].astype(BF16), rows=rows, n_lat=n_lat, seq=seq, ctx_len=ctx_len)
        xa = _resid_proj(merged, w_o[l].astype(BF16), xa, mod, rows=rows, k_gate=2, **geom)

        h2, h2t = _norm_mod(xa, norm2_g[l], mod, rows=rows, k_shift=3, k_scale=4, with_transpose=True, **geom)
        th, e1, e2 = _peer_scores(h2, peer_wq[l].astype(BF16), peer_k1[l], peer_k2[l], rows=rows,
                                  n_i=PEER_TE // N_KEYS)
        xa = _peer_mix(h2t, peer_u[l].astype(BF16), peer_v[l].T.astype(BF16), th, e1, e2, xa, mod,
                       final_g if last else None, rows=rows, k_gate=5, **geom)
    return xa.reshape(n_batch, seq, d)
```

```python
import functools

import jax
import jax.numpy as jnp
from jax import lax
from jax.experimental import pallas as pl
from jax.experimental.pallas import tpu as pltpu

F32 = jnp.float32
BF16 = jnp.bfloat16

HEAD_DIM = 128
KV_GROUPS = 4
GRID_W = 64
ROPE_THETA = 10000.0
PEER_HEADS = 8
N_KEYS = 128
PEER_TOPK = 16
N_MOD = 6
EPS = 1e-6
LOG2_E = 1.4426950408889634
LANES = 128
BF16_SUBLANES = 16
VMEM_LIMIT_BYTES = 56 * 1024 * 1024

_NT = (((1,), (1,)), ((), ()))


def _cparams(*sem, flags=None):
    return pltpu.CompilerParams(dimension_semantics=sem, vmem_limit_bytes=VMEM_LIMIT_BYTES, flags=flags)


def _mod_row(row0, n_lat, seq, n_batch):
    return jnp.where(row0 < n_lat, row0 // seq, n_batch)


def _mod_vec(mod_ref, b, k, d):
    return mod_ref[pl.ds(b, 1), k * d:(k + 1) * d]


def _adaln_kernel(c_ref, w_ref, b_ref, o_ref):
    cv = c_ref[...]
    s = cv * jax.nn.sigmoid(cv)
    o_ref[0] = jnp.dot(s, w_ref[0], preferred_element_type=F32) + b_ref[0]


def _adaln(cvec, ada_w, ada_b, tn=1024):
    depth, d, n = ada_w.shape
    rows = cvec.shape[0]
    return pl.pallas_call(
        _adaln_kernel,
        grid=(depth, n // tn),
        in_specs=[pl.BlockSpec((rows, d), lambda l, j: (0, 0)),
                  pl.BlockSpec((1, d, tn), lambda l, j: (l, 0, j)),
                  pl.BlockSpec((1, 1, tn), lambda l, j: (l, 0, j))],
        out_specs=pl.BlockSpec((1, rows, tn), lambda l, j: (l, 0, j)),
        out_shape=jax.ShapeDtypeStruct((depth, rows, n), F32),
        compiler_params=_cparams("arbitrary", "arbitrary"),
        name="adaln",
    )(cvec, ada_w, ada_b.reshape(depth, 1, n))


def _norm_mod_kernel(x_ref, g_ref, mod_ref, o_ref, *maybe_ot_ref, tm, n_lat, seq, n_batch, k_shift, k_scale):
    d = x_ref.shape[1]
    b = _mod_row(pl.program_id(0) * tm, n_lat, seq, n_batch)
    xf = x_ref[...]
    y = xf * lax.rsqrt(jnp.mean(xf * xf, axis=-1, keepdims=True) + EPS)
    h = y * g_ref[...]
    shift = _mod_vec(mod_ref, b, k_shift, d)
    scale = _mod_vec(mod_ref, b, k_scale, d)
    h = h * (1.0 + scale) + shift
    o_ref[...] = h.astype(o_ref.dtype)
    for ot_ref in maybe_ot_ref:
        ot_ref[...] = h.T.astype(ot_ref.dtype)


def _norm_mod(x, g, mod, *, rows, n_lat, seq, n_batch, k_shift, k_scale, with_transpose=False, tm=512):
    d = x.shape[1]
    kern = functools.partial(_norm_mod_kernel, tm=tm, n_lat=n_lat, seq=seq, n_batch=n_batch,
                             k_shift=k_shift, k_scale=k_scale)
    out_specs = [pl.BlockSpec((tm, d), lambda i: (i, 0))]
    out_shape = [jax.ShapeDtypeStruct((rows, d), BF16)]
    if with_transpose:
        out_specs.append(pl.BlockSpec((d, tm), lambda i: (0, i)))
        out_shape.append(jax.ShapeDtypeStruct((d, rows), BF16))
    out = pl.pallas_call(
        kern,
        grid=(rows // tm,),
        in_specs=[pl.BlockSpec((tm, d), lambda i: (i, 0)),
                  pl.BlockSpec((1, d), lambda i: (0, 0)),
                  pl.BlockSpec(mod.shape, lambda i: (0, 0))],
        out_specs=out_specs,
        out_shape=out_shape,
        compiler_params=_cparams("parallel"),
        name="norm_mod",
    )(x, g.reshape(1, d), mod)
    return out if with_transpose else out[0]


def _head_norm_rope(a, g, cos, sin, out_scale):
    y = a * lax.rsqrt(jnp.mean(a * a, axis=-1, keepdims=True) + EPS) * g
    lane = lax.broadcasted_iota(jnp.int32, y.shape, 1)
    lower = (lane & (HEAD_DIM // 2 - 1)) < HEAD_DIM // 4
    partner = jnp.where(lower, pltpu.roll(y, HEAD_DIM - HEAD_DIM // 4, 1),
                        pltpu.roll(y, HEAD_DIM // 4, 1))
    r = y * cos + partner * sin
    if out_scale != 1.0:
        r = r * out_scale
    return r


def _q_kernel(h_ref, w_ref, g_ref, cos_ref, sin_ref, o_ref, *, out_scale):
    acc = jnp.dot(h_ref[...], w_ref[...], preferred_element_type=F32)
    g, cos, sin = g_ref[...], cos_ref[...], sin_ref[...]
    for hh in range(acc.shape[1] // HEAD_DIM):
        sl = slice(hh * HEAD_DIM, (hh + 1) * HEAD_DIM)
        o_ref[:, sl] = _head_norm_rope(acc[:, sl], g, cos, sin, out_scale).astype(o_ref.dtype)


def _kv_kernel(h_ref, wk_ref, wv_ref, g_ref, cos_ref, sin_ref, k_ref, v_ref):
    h = h_ref[...]
    acc = jnp.dot(h, wk_ref[...], preferred_element_type=F32)
    g, cos, sin = g_ref[...], cos_ref[...], sin_ref[...]
    for hh in range(acc.shape[1] // HEAD_DIM):
        sl = slice(hh * HEAD_DIM, (hh + 1) * HEAD_DIM)
        k_ref[:, sl] = _head_norm_rope(acc[:, sl], g, cos, sin, 1.0).astype(k_ref.dtype)
    vacc = jnp.dot(h, wv_ref[...], preferred_element_type=F32).astype(v_ref.dtype)
    ones = jnp.ones((vacc.shape[0], HEAD_DIM), v_ref.dtype)
    for g in range(vacc.shape[1] // HEAD_DIM):
        v_ref[:, 2 * g * HEAD_DIM:(2 * g + 1) * HEAD_DIM] = vacc[:, g * HEAD_DIM:(g + 1) * HEAD_DIM]
        v_ref[:, (2 * g + 1) * HEAD_DIM:(2 * g + 2) * HEAD_DIM] = ones


def _conv_in_kernel(h_ref, wu_ref, wb_ref, wc_ref, z_ref, gb_ref):
    h = h_ref[...]
    u = jnp.dot(h, wu_ref[...], preferred_element_type=F32)
    gc = jnp.dot(h, wc_ref[...], preferred_element_type=F32)
    z_ref[...] = (gc * u).astype(z_ref.dtype)
    gb_ref[...] = jnp.dot(h, wb_ref[...], preferred_element_type=F32).astype(gb_ref.dtype)


def _gate_kernel(h_ref, w_ref, b_ref, o_ref):
    acc = jnp.dot(h_ref[...], w_ref[...], preferred_element_type=F32)
    o_ref[...] = jax.nn.sigmoid(acc + b_ref[...]).astype(o_ref.dtype)


def _rope_block(i, tm, n_lat, seq):
    return jnp.where(i * tm < n_lat, (i * tm % seq) // tm, seq // tm)


def _in_proj(h, w_in, q_g, k_g, b_gate, cos_t, sin_t, *, rows_q, rows_kv, n_lat, seq, d,
             tm=512, tn=512):
    kd = h.shape[1]
    kvw = d // KV_GROUPS
    q_end, k_end, v_end = d, d + kvw, d + 2 * kvw
    u_end, b_end, c_end = v_end + d, v_end + 2 * d, v_end + 3 * d
    hspec = pl.BlockSpec((tm, kd), lambda i, j: (i, 0))
    hd_spec = pl.BlockSpec((1, HEAD_DIM), lambda i, j: (0, 0))
    tab_spec = pl.BlockSpec((tm, HEAD_DIM), lambda i, j: (_rope_block(i, tm, n_lat, seq), 0))

    def wspec(col0):
        return pl.BlockSpec((kd, tn), lambda i, j: (0, col0 // tn + j))

    def ospec():
        return pl.BlockSpec((tm, tn), lambda i, j: (i, j))

    q = pl.pallas_call(
        functools.partial(_q_kernel, out_scale=HEAD_DIM ** -0.5 * LOG2_E),
        grid=(rows_q // tm, d // tn),
        in_specs=[hspec, wspec(0), hd_spec, tab_spec, tab_spec],
        out_specs=ospec(),
        out_shape=jax.ShapeDtypeStruct((rows_q, d), BF16),
        compiler_params=_cparams("parallel", "arbitrary"),
        name="proj_q",
    )(h, w_in, q_g.reshape(1, HEAD_DIM), cos_t, sin_t)

    assert kvw == tn
    k, v = pl.pallas_call(
        _kv_kernel,
        grid=(rows_kv // tm, 1),
        in_specs=[hspec, wspec(q_end), wspec(k_end), hd_spec, tab_spec, tab_spec],
        out_specs=[ospec(), pl.BlockSpec((tm, 2 * tn), lambda i, j: (i, j))],
        out_shape=[jax.ShapeDtypeStruct((rows_kv, kvw), BF16), jax.ShapeDtypeStruct((rows_kv, 2 * kvw), BF16)],
        compiler_params=_cparams("parallel", "arbitrary"),
        name="proj_kv",
    )(h, w_in, w_in, k_g.reshape(1, HEAD_DIM), cos_t, sin_t)

    z, gb = pl.pallas_call(
        _conv_in_kernel,
        grid=(rows_q // tm, d // tn),
        in_specs=[hspec, wspec(v_end), wspec(u_end), wspec(b_end)],
        out_specs=[ospec(), ospec()],
        out_shape=[jax.ShapeDtypeStruct((rows_q, d), BF16)] * 2,
        compiler_params=_cparams("parallel", "arbitrary"),
        name="proj_conv",
    )(h, w_in, w_in, w_in)

    gates = pl.pallas_call(
        _gate_kernel,
        grid=(rows_q // tm, 2 * d // tn),
        in_specs=[hspec, wspec(c_end), pl.BlockSpec((1, tn), lambda i, j: (0, j))],
        out_specs=ospec(),
        out_shape=jax.ShapeDtypeStruct((rows_q, 2 * d), BF16),
        compiler_params=_cparams("parallel", "arbitrary"),
        name="proj_gates",
    )(h, w_in, b_gate.reshape(1, 2 * d))
    return q, k, v, z, gb, gates


ATTN_KEY_CHUNK = 256


def _attn_kernel(q_ref, kl_ref, vl_ref, kc_ref, vc_ref, o_ref, *, tq, n_lat):
    is_lat = pl.program_id(1) * tq < n_lat
    q = jnp.concatenate([q_ref[:, hh * HEAD_DIM:(hh + 1) * HEAD_DIM] for hh in range(KV_GROUPS)], axis=0)

    def step(carry, k, v):
        s = lax.dot_general(q, k, _NT, preferred_element_type=F32)
        m_new = jnp.max(s, axis=-1, keepdims=True)
        pv = lambda m: jnp.dot(jnp.exp2(s - m).astype(v.dtype), v, preferred_element_type=F32)
        if carry is None:
            return m_new, pv(m_new)
        m_old, acc = carry
        m_new = jnp.maximum(m_old, m_new)
        return m_new, acc * jnp.exp2(m_old - m_new) + pv(m_new)

    def run(with_lat):
        carry = step(None, kc_ref[...], vc_ref[...])
        if with_lat:
            for c in range(kl_ref.shape[0] // ATTN_KEY_CHUNK):
                rows = slice(c * ATTN_KEY_CHUNK, (c + 1) * ATTN_KEY_CHUNK)
                carry = step(carry, kl_ref[rows, :], vl_ref[rows, :])
        acc = carry[1]
        out = (acc[:, :HEAD_DIM] / acc[:, HEAD_DIM:]).astype(o_ref.dtype)
        for hh in range(KV_GROUPS):
            o_ref[:, hh * HEAD_DIM:(hh + 1) * HEAD_DIM] = out[hh * tq:(hh + 1) * tq, :]

    @pl.when(is_lat)
    def _():
        run(True)

    @pl.when(jnp.logical_not(is_lat))
    def _():
        run(False)


def _attention(q, k, v, *, rows_q, n_lat, seq, ctx_len, tq=256):
    assert ctx_len % tq == 0 or tq % ctx_len == 0
    assert tq <= ctx_len
    gw = KV_GROUPS * HEAD_DIM
    n_kv = k.shape[1] // HEAD_DIM

    def batch(i):
        return jnp.where(i * tq < n_lat, i * tq // seq, (i * tq - n_lat) // ctx_len)

    qspec = pl.BlockSpec((tq, gw), lambda g, i: (i, g))

    def lat_spec(width):
        return pl.BlockSpec((seq, width), lambda g, i: (batch(i), g))

    def ctx_spec(width):
        return pl.BlockSpec((ctx_len, width), lambda g, i: (n_lat // ctx_len + batch(i), g))

    return pl.pallas_call(
        functools.partial(_attn_kernel, tq=tq, n_lat=n_lat),
        grid=(n_kv, rows_q // tq),
        in_specs=[qspec, lat_spec(HEAD_DIM), lat_spec(2 * HEAD_DIM), ctx_spec(HEAD_DIM), ctx_spec(2 * HEAD_DIM)],
        out_specs=qspec,
        out_shape=jax.ShapeDtypeStruct((rows_q, q.shape[1]), BF16),
        compiler_params=_cparams("parallel", "arbitrary"),
        name="attention",
    )(q, k, v, k, v)


def _merge_kernel(attn_ref, z_ref, zp_ref, zn_ref, gb_ref, ga_ref, gc_ref, cw_ref,
                  wao_ref, wco_ref, o_ref, y_scr, *, tm, n_lat, seq, ctx_len):
    @pl.when(pl.program_id(1) == 0)
    def _():
        z = z_ref[...].astype(F32)
        loc = lax.broadcasted_iota(jnp.int32, (tm, 1), 0)
        row = pl.program_id(0) * tm + loc
        seqlen = jnp.where(row < n_lat, seq, ctx_len)
        pos = row & (seqlen - 1)
        halo_prev = zp_ref[...].astype(F32)[BF16_SUBLANES - 1:BF16_SUBLANES, :]
        halo_next = zn_ref[...].astype(F32)[0:1, :]
        z_prev = jnp.where(loc == 0, halo_prev, pltpu.roll(z, 1, 0))
        z_prev = jnp.where(pos == 0, 0.0, z_prev)
        z_next = jnp.where(loc == tm - 1, halo_next, pltpu.roll(z, tm - 1, 0))
        z_next = jnp.where(pos == seqlen - 1, 0.0, z_next)
        cw = cw_ref[...]
        conv = cw[0:1, :] * z_prev + cw[1:2, :] * z + cw[2:3, :] * z_next
        y_scr[...] = (gb_ref[...].astype(F32) * conv).astype(y_scr.dtype)

    acc_a = jnp.dot(attn_ref[...], wao_ref[...], preferred_element_type=F32)
    acc_c = jnp.dot(y_scr[...], wco_ref[...], preferred_element_type=F32)
    o_ref[...] = (ga_ref[...].astype(F32) * acc_a + gc_ref[...].astype(F32) * acc_c).astype(o_ref.dtype)


def _merge(attn, z, gb, gates, conv_w, w_ao, w_co, *, rows, n_lat, seq, ctx_len, tm=512, tn=512):
    d = attn.shape[1]
    hb = BF16_SUBLANES
    n_hblk = z.shape[0] // hb
    row_spec = pl.BlockSpec((tm, d), lambda i, j: (i, 0))
    prev_spec = pl.BlockSpec((hb, d), lambda i, j: (jnp.maximum(i * (tm // hb) - 1, 0), 0))
    next_spec = pl.BlockSpec((hb, d), lambda i, j: (jnp.minimum((i + 1) * (tm // hb), n_hblk - 1), 0))
    wspec = pl.BlockSpec((d, tn), lambda i, j: (0, j))
    return pl.pallas_call(
        functools.partial(_merge_kernel, tm=tm, n_lat=n_lat, seq=seq, ctx_len=ctx_len),
        grid=(rows // tm, d // tn),
        in_specs=[row_spec, row_spec, prev_spec, next_spec, row_spec,
                  pl.BlockSpec((tm, tn), lambda i, j: (i, j)),
                  pl.BlockSpec((tm, tn), lambda i, j: (i, d // tn + j)),
                  pl.BlockSpec(conv_w.shape, lambda i, j: (0, 0)),
                  wspec, wspec],
        out_specs=pl.BlockSpec((tm, tn), lambda i, j: (i, j)),
        out_shape=jax.ShapeDtypeStruct((rows, d), BF16),
        scratch_shapes=[pltpu.VMEM((tm, d), BF16)],
        compiler_params=_cparams("parallel", "arbitrary"),
        name="merge",
    )(attn, z, z, z, gb, gates, gates, conv_w, w_ao, w_co)


def _resid_proj_kernel(a_ref, w_ref, x_ref, mod_ref, o_ref, *, tm, tn, n_lat, seq, n_batch, k_gate, d):
    b = _mod_row(pl.program_id(0) * tm, n_lat, seq, n_batch)
    col0 = pl.multiple_of(k_gate * d + pl.program_id(1) * tn, LANES)
    gate = mod_ref[pl.ds(b, 1), pl.ds(col0, tn)]
    acc = jnp.dot(a_ref[...], w_ref[...], preferred_element_type=F32)
    o_ref[...] = x_ref[...] + gate * acc


def _resid_proj(a, w, x, mod, *, rows, n_lat, seq, n_batch, k_gate, tm=512, tn=512):
    d = a.shape[1]
    return pl.pallas_call(
        functools.partial(_resid_proj_kernel, tm=tm, tn=tn, n_lat=n_lat, seq=seq, n_batch=n_batch,
                          k_gate=k_gate, d=d),
        grid=(rows // tm, d // tn),
        in_specs=[pl.BlockSpec((tm, d), lambda i, j: (i, 0)),
                  pl.BlockSpec((d, tn), lambda i, j: (0, j)),
                  pl.BlockSpec((tm, tn), lambda i, j: (i, j)),
                  pl.BlockSpec(mod.shape, lambda i, j: (0, 0))],
        out_specs=pl.BlockSpec((tm, tn), lambda i, j: (i, j)),
        out_shape=jax.ShapeDtypeStruct((rows, d), F32),
        compiler_params=_cparams("parallel", "arbitrary"),
        name="resid_proj",
    )(a, w, x, mod)


N_TOP = PEER_TOPK + 1
TOP_ROWS = 24
N_CAND = 96


def _top_values(s, k, out_ref):
    cur = s
    prev = mx = None
    for r in range(k):
        prev = mx
        mx = jnp.max(cur, axis=0, keepdims=True)
        if out_ref is not None:
            out_ref[r:r + 1, :] = mx
        cur = jnp.where(cur == mx, -jnp.inf, cur)
    return prev, mx


def _candidate_sums(v1_ref, v2_ref, cand_ref):
    cand_ref[0:TOP_ROWS, :] = v1_ref[0:1, :] + v2_ref[...]
    for a in range(1, 8):
        cand_ref[16 + 8 * a:24 + 8 * a, :] = v1_ref[a:a + 1, :] + v2_ref[0:8, :]
    cand_ref[80:96, :] = v1_ref[8:TOP_ROWS, :] + v2_ref[0:1, :]


def _peer_score_kernel(h_ref, wq_ref, k1_ref, k2_ref, th_ref, e1_ref, e2_ref,
                       v1_scr, v2_scr, cand_scr, *, n_i):
    qp = jnp.dot(h_ref[...], wq_ref[...], preferred_element_type=F32)
    pad = jnp.full((TOP_ROWS - N_TOP, v1_scr.shape[1]), -jnp.inf, F32)
    v1_scr[N_TOP:TOP_ROWS, :] = pad
    v2_scr[N_TOP:TOP_ROWS, :] = pad
    for h in range(PEER_HEADS):
        q1 = qp[:, (2 * h) * N_KEYS:(2 * h + 1) * N_KEYS]
        q2 = qp[:, (2 * h + 1) * N_KEYS:(2 * h + 2) * N_KEYS]
        s1 = lax.dot_general(k1_ref[h], q1, _NT, preferred_element_type=F32) * LOG2_E
        s2 = lax.dot_general(k2_ref[h], q2, _NT, preferred_element_type=F32) * LOG2_E
        _top_values(s1, N_TOP, v1_scr)
        _top_values(s2, N_TOP, v2_scr)
        _candidate_sums(v1_scr, v2_scr, cand_scr)
        cand = cand_scr[...]
        c16, c17 = _top_values(cand, N_TOP, None)
        tau = 0.5 * (c16 + c17)
        top1 = v1_scr[0:1, :]
        top2 = v2_scr[0:1, :]
        z = jnp.sum(jnp.where(cand >= c16, jnp.exp2(cand - (top1 + top2)), 0.0), axis=0, keepdims=True)
        theta = jnp.exp2(tau - s1 - top2)
        e1 = jnp.exp2(s1 - (top1 + jnp.log2(z)))
        for g in range(N_KEYS // n_i):
            th_ref[g, h * n_i:(h + 1) * n_i, :] = theta[g * n_i:(g + 1) * n_i, :]
            e1_ref[g, h * n_i:(h + 1) * n_i, :] = e1[g * n_i:(g + 1) * n_i, :]
        e2_ref[h] = jnp.exp2(s2 - top2)


def _peer_scores(h2, wq, k1, k2, *, rows, n_i, tm=256):
    d = h2.shape[1]
    n_grp = N_KEYS // n_i
    grouped = pl.BlockSpec((n_grp, PEER_HEADS * n_i, tm), lambda i: (0, 0, i))
    by_head = pl.BlockSpec((PEER_HEADS, N_KEYS, tm), lambda i: (0, 0, i))
    grouped_shape = jax.ShapeDtypeStruct((n_grp, PEER_HEADS * n_i, rows), F32)
    by_head_shape = jax.ShapeDtypeStruct((PEER_HEADS, N_KEYS, rows), F32)
    return pl.pallas_call(
        functools.partial(_peer_score_kernel, n_i=n_i),
        grid=(rows // tm,),
        in_specs=[pl.BlockSpec((tm, d), lambda i: (i, 0)),
                  pl.BlockSpec(wq.shape, lambda i: (0, 0)),
                  pl.BlockSpec(k1.shape, lambda i: (0, 0, 0)),
                  pl.BlockSpec(k2.shape, lambda i: (0, 0, 0))],
        out_specs=[grouped, grouped, by_head],
        out_shape=[grouped_shape, grouped_shape, by_head_shape],
        scratch_shapes=[pltpu.VMEM((TOP_ROWS, tm), F32), pltpu.VMEM((TOP_ROWS, tm), F32),
                        pltpu.VMEM((N_CAND, tm), F32)],
        compiler_params=_cparams("parallel"),
        name="peer_scores",
    )(h2, wq, k1, k2)


def _gelu(x):
    return (0.5 * x) * (1.0 + lax.erf(x * (0.5 ** 0.5)))


def _peer_mix_kernel(ht_ref, u_ref, vt_ref, th_ref, e1_ref, e2_ref, x_ref, mod_ref, *rest,
                     tt, te, n_lat, seq, n_batch, k_gate, final_norm):
    if final_norm:
        fg_ref, o_ref, acc_ref, act_scr, g_scr, w_scr = rest
    else:
        o_ref, acc_ref, act_scr, g_scr, w_scr = rest
    j = pl.program_id(1)
    last = pl.num_programs(1) - 1
    d = x_ref.shape[1]
    n_i = te // N_KEYS

    @pl.when(jnp.logical_and(pl.program_id(0) == 0, j == 0))
    def _():
        acc_ref[...] = jnp.zeros_like(acc_ref)

    act_scr[j % 2] = _gelu(jnp.dot(u_ref[...], ht_ref[...], preferred_element_type=F32))

    for ii in range(n_i):
        rs = slice(ii * N_KEYS, (ii + 1) * N_KEYS)
        for tc in range(tt // LANES):
            cs = slice(tc * LANES, (tc + 1) * LANES)
            gate = None
            for h in range(PEER_HEADS):
                row = slice(h * n_i + ii, h * n_i + ii + 1)
                e2 = e2_ref[h, :, cs]
                wgt = jnp.where(e2 >= th_ref[0, row, cs], e2, 0.0) * e1_ref[0, row, cs]
                gate = wgt if gate is None else gate + wgt
            g_scr[rs, cs] = gate

    prev = act_scr[(j + 1) % 2]
    w_scr[...] = jnp.where(j >= 1, g_scr[...] * prev, 0.0).astype(w_scr.dtype)
    acc_ref[...] += jnp.dot(vt_ref[...], w_scr[...], preferred_element_type=F32)

    @pl.when(j == last)
    def _():
        b = _mod_row(pl.program_id(0) * tt, n_lat, seq, n_batch)
        xn = x_ref[...] + _mod_vec(mod_ref, b, k_gate, d) * acc_ref[...].T
        if final_norm:
            xn = xn * lax.rsqrt(jnp.mean(xn * xn, axis=-1, keepdims=True) + EPS) * fg_ref[...]
        o_ref[...] = xn
        acc_ref[...] = jnp.zeros_like(acc_ref)


PEER_TE = 512


def _peer_mix(h2t, u, vt, th, e1, e2, x, mod, final_g, *, rows, n_lat, seq, n_batch, k_gate,
              tt=512, te=PEER_TE):
    d = x.shape[1]
    n_tiles = u.shape[0] // te
    final_norm = final_g is not None
    n_i = te // N_KEYS

    def cur(j):
        return jnp.minimum(j, n_tiles - 1)

    def prv(j):
        return jnp.maximum(j - 1, 0)

    grouped = pl.BlockSpec((1, PEER_HEADS * n_i, tt), lambda t, j: (prv(j), 0, t))
    by_head = pl.BlockSpec((PEER_HEADS, N_KEYS, tt), lambda t, j: (0, 0, t))
    in_specs = [pl.BlockSpec((d, tt), lambda t, j: (0, t)),
                pl.BlockSpec((te, d), lambda t, j: (cur(j), 0)),
                pl.BlockSpec((d, te), lambda t, j: (0, prv(j))),
                grouped, grouped, by_head,
                pl.BlockSpec((tt, d), lambda t, j: (t, 0)),
                pl.BlockSpec(mod.shape, lambda t, j: (0, 0))]
    args = [h2t, u, vt, th, e1, e2, x, mod]
    if final_norm:
        in_specs.append(pl.BlockSpec((1, d), lambda t, j: (0, 0)))
        args.append(final_g.reshape(1, d))
    return pl.pallas_call(
        functools.partial(_peer_mix_kernel, tt=tt, te=te, n_lat=n_lat, seq=seq, n_batch=n_batch,
                          k_gate=k_gate, final_norm=final_norm),
        grid=(rows // tt, n_tiles + 1),
        in_specs=in_specs,
        out_specs=pl.BlockSpec((tt, d), lambda t, j: (t, 0)),
        out_shape=jax.ShapeDtypeStruct((rows, d), F32),
        scratch_shapes=[pltpu.VMEM((d, tt), F32), pltpu.VMEM((2, te, tt), F32), pltpu.VMEM((te, tt), F32),
                        pltpu.VMEM((te, tt), BF16)],
        compiler_params=_cparams("arbitrary", "arbitrary"),
        name="peer_mix",
    )(*args)


def _rope_tables(seq, pad_rows):
    half = HEAD_DIM // 2
    t = jnp.arange(seq)
    row = (t // GRID_W).astype(F32)
    col = (t % GRID_W).astype(F32)
    inv = ROPE_THETA ** (-jnp.arange(0, half, 2, dtype=F32) / half)
    ang_r = row[:, None] * inv[None, :]
    ang_c = col[:, None] * inv[None, :]
    cos = jnp.concatenate([jnp.cos(ang_r), jnp.cos(ang_r), jnp.cos(ang_c), jnp.cos(ang_c)], axis=-1)
    sin = jnp.concatenate([-jnp.sin(ang_r), jnp.sin(ang_r), -jnp.sin(ang_c), jnp.sin(ang_c)], axis=-1)
    cos = jnp.concatenate([cos, jnp.ones((pad_rows, HEAD_DIM), F32)], axis=0)
    sin = jnp.concatenate([sin, jnp.zeros((pad_rows, HEAD_DIM), F32)], axis=0)
    return cos, sin


def kernel(x, c, ctx, c_ctx, ada_w, ada_b, norm1_g, norm2_g, w_in, b_gate, q_norm_g, k_norm_g,
           conv_w, w_attn_out, w_conv_out, w_o, peer_wq, peer_k1, peer_k2, peer_u, peer_v, final_g):
    n_batch, seq, d = x.shape
    ctx_len = ctx.shape[1]
    depth = ada_w.shape[0]
    n_lat = n_batch * seq
    n_all = n_lat + n_batch * ctx_len
    assert seq & (seq - 1) == 0 and ctx_len & (ctx_len - 1) == 0
    tm = 512

    xa = jnp.concatenate([x.reshape(n_lat, d), ctx.reshape(n_batch * ctx_len, d)], axis=0)
    pad = (-(n_batch + 1)) % 8
    cvec = jnp.concatenate([c, c_ctx[None, :], jnp.zeros((pad, d), F32)], axis=0)
    mod_all = _adaln(cvec, ada_w, ada_b)
    cos_t, sin_t = _rope_tables(seq, tm)
    geom = dict(n_lat=n_lat, seq=seq, n_batch=n_batch)

    for l in range(depth):
        last = l == depth - 1
        rows = n_lat if last else n_all
        mod = mod_all[l]
        w_in_l = w_in[l].astype(BF16)

        h = _norm_mod(xa, norm1_g[l], mod, rows=n_all, k_shift=0, k_scale=1, **geom)
        q, k, v, z, gb, gates = _in_proj(h, w_in_l, q_norm_g[l], k_norm_g[l], b_gate[l], cos_t, sin_t,
                                         rows_q=rows, rows_kv=n_all, n_lat=n_lat, seq=seq, d=d, tm=tm)
        attn = _attention(q, k, v, rows_q=rows, n_lat=n_lat, seq=seq, ctx_len=ctx_len)
        merged = _merge(attn, z, gb, gates, conv_w[l], w_attn_out[l].astype(BF16),
                        w_conv_out[l].astype(BF16), rows=rows, n_lat=n_lat, seq=seq, ctx_len=ctx_len)
        xa = _resid_proj(merged, w_o[l].astype(BF16), xa, mod, rows=rows, k_gate=2, **geom)

        h2, h2t = _norm_mod(xa, norm2_g[l], mod, rows=rows, k_shift=3, k_scale=4, with_transpose=True, **geom)
        th, e1, e2 = _peer_scores(h2, peer_wq[l].astype(BF16), peer_k1[l], peer_k2[l], rows=rows,
                                  n_i=PEER_TE // N_KEYS)
        xa = _peer_mix(h2t, peer_u[l].astype(BF16), peer_v[l].T.astype(BF16), th, e1, e2, xa, mod,
                       final_g if last else None, rows=rows, k_gate=5, **geom)
    return xa.reshape(n_batch, seq, d)
```

```python
import functools

import jax
import jax.numpy as jnp
from jax import lax
from jax.experimental import pallas as pl
from jax.experimental.pallas import tpu as pltpu

F32 = jnp.float32
BF16 = jnp.bfloat16

HEAD_DIM = 128
KV_GROUPS = 4
GRID_W = 64
ROPE_THETA = 10000.0
PEER_HEADS = 8
N_KEYS = 128
PEER_TOPK = 16
N_MOD = 6
EPS = 1e-6
LOG2_E = 1.4426950408889634
LANES = 128
BF16_SUBLANES = 16
VMEM_LIMIT_BYTES = 56 * 1024 * 1024

_NT = (((1,), (1,)), ((), ()))


def _cparams(*sem, flags=None):
    return pltpu.CompilerParams(dimension_semantics=sem, vmem_limit_bytes=VMEM_LIMIT_BYTES, flags=flags)


def _mod_row(row0, n_lat, seq, n_batch):
    return jnp.where(row0 < n_lat, row0 // seq, n_batch)


def _mod_vec(mod_ref, b, k, d):
    return mod_ref[pl.ds(b, 1), k * d:(k + 1) * d]


def _adaln_kernel(c_ref, w_ref, b_ref, o_ref):
    cv = c_ref[...]
    s = cv * jax.nn.sigmoid(cv)
    o_ref[0] = jnp.dot(s, w_ref[0], preferred_element_type=F32) + b_ref[0]


def _adaln(cvec, ada_w, ada_b, tn=1024):
    depth, d, n = ada_w.shape
    rows = cvec.shape[0]
    return pl.pallas_call(
        _adaln_kernel,
        grid=(depth, n // tn),
        in_specs=[pl.BlockSpec((rows, d), lambda l, j: (0, 0)),
                  pl.BlockSpec((1, d, tn), lambda l, j: (l, 0, j)),
                  pl.BlockSpec((1, 1, tn), lambda l, j: (l, 0, j))],
        out_specs=pl.BlockSpec((1, rows, tn), lambda l, j: (l, 0, j)),
        out_shape=jax.ShapeDtypeStruct((depth, rows, n), F32),
        compiler_params=_cparams("arbitrary", "arbitrary"),
        name="adaln",
    )(cvec, ada_w, ada_b.reshape(depth, 1, n))


def _tile_of(lat_ref, ctx_ref, row0, n_lat):
    if ctx_ref is None:
        return lat_ref[...]
    return jnp.where(row0 < n_lat, lat_ref[...], ctx_ref[...])


def _row_specs(x_ctx, block, n_lat, col_of=None):
    n_lat_tiles = n_lat // block[0]
    col = (lambda *j: 0) if col_of is None else col_of
    if x_ctx is None:
        return [pl.BlockSpec(block, lambda i, *j: (i, col(*j)))]
    return [pl.BlockSpec(block, lambda i, *j: (jnp.minimum(i, n_lat_tiles - 1), col(*j))),
            pl.BlockSpec(block, lambda i, *j: (jnp.maximum(i - n_lat_tiles, 0), col(*j)))]


def _norm_mod_kernel(*refs, split, tm, n_lat, seq, n_batch, k_shift, k_scale):
    x_ref, xc_ref = (refs[0], refs[1]) if split else (refs[0], None)
    g_ref, mod_ref, o_ref, *maybe_ot_ref = refs[2 if split else 1:]
    d = x_ref.shape[1]
    row0 = pl.program_id(0) * tm
    b = _mod_row(row0, n_lat, seq, n_batch)
    xf = _tile_of(x_ref, xc_ref, row0, n_lat)
    y = xf * lax.rsqrt(jnp.mean(xf * xf, axis=-1, keepdims=True) + EPS)
    h = y * g_ref[...]
    shift = _mod_vec(mod_ref, b, k_shift, d)
    scale = _mod_vec(mod_ref, b, k_scale, d)
    h = h * (1.0 + scale) + shift
    o_ref[...] = h.astype(o_ref.dtype)
    for ot_ref in maybe_ot_ref:
        ot_ref[...] = h.T.astype(ot_ref.dtype)


def _norm_mod(x, g, mod, *, rows, n_lat, seq, n_batch, k_shift, k_scale, x_ctx=None,
              with_transpose=False, tm=512):
    d = x.shape[1]
    kern = functools.partial(_norm_mod_kernel, split=x_ctx is not None, tm=tm, n_lat=n_lat, seq=seq,
                             n_batch=n_batch, k_shift=k_shift, k_scale=k_scale)
    xs = [x] if x_ctx is None else [x, x_ctx]
    out_specs = [pl.BlockSpec((tm, d), lambda i: (i, 0))]
    out_shape = [jax.ShapeDtypeStruct((rows, d), BF16)]
    if with_transpose:
        out_specs.append(pl.BlockSpec((d, tm), lambda i: (0, i)))
        out_shape.append(jax.ShapeDtypeStruct((d, rows), BF16))
    out = pl.pallas_call(
        kern,
        grid=(rows // tm,),
        in_specs=_row_specs(x_ctx, (tm, d), n_lat) + [pl.BlockSpec((1, d), lambda i: (0, 0)),
                                                         pl.BlockSpec(mod.shape, lambda i: (0, 0))],
        out_specs=out_specs,
        out_shape=out_shape,
        compiler_params=_cparams("parallel"),
        name="norm_mod",
    )(*xs, g.reshape(1, d), mod)
    return out if with_transpose else out[0]


def _heads_norm_rope(acc, ones_bd, perm_bd, g, cos, sin, out_scale):
    n_heads = acc.shape[1] // HEAD_DIM
    ss = jnp.dot((acc * acc).astype(BF16), ones_bd, preferred_element_type=F32)
    y = acc * lax.rsqrt(ss * (1.0 / HEAD_DIM) + EPS) * g
    partner = jnp.dot(y.astype(BF16), perm_bd, preferred_element_type=F32)
    r = y * jnp.tile(cos, (1, n_heads)) + partner * jnp.tile(sin, (1, n_heads))
    if out_scale != 1.0:
        r = r * out_scale
    return r


def _q_kernel(h_ref, w_ref, ones_ref, perm_ref, g_ref, cos_ref, sin_ref, o_ref, *, out_scale):
    acc = jnp.dot(h_ref[...], w_ref[...], preferred_element_type=F32)
    r = _heads_norm_rope(acc, ones_ref[...], perm_ref[...], g_ref[...], cos_ref[...], sin_ref[...], out_scale)
    o_ref[...] = r.astype(o_ref.dtype)


def _kv_kernel(h_ref, wk_ref, wv_ref, ones_ref, perm_ref, g_ref, cos_ref, sin_ref, k_ref, v_ref):
    h = h_ref[...]
    acc = jnp.dot(h, wk_ref[...], preferred_element_type=F32)
    r = _heads_norm_rope(acc, ones_ref[...], perm_ref[...], g_ref[...], cos_ref[...], sin_ref[...], 1.0)
    k_ref[...] = r.astype(k_ref.dtype)
    vacc = jnp.dot(h, wv_ref[...], preferred_element_type=F32).astype(v_ref.dtype)
    ones = jnp.ones((vacc.shape[0], HEAD_DIM), v_ref.dtype)
    for g in range(vacc.shape[1] // HEAD_DIM):
        v_ref[:, 2 * g * HEAD_DIM:(2 * g + 1) * HEAD_DIM] = vacc[:, g * HEAD_DIM:(g + 1) * HEAD_DIM]
        v_ref[:, (2 * g + 1) * HEAD_DIM:(2 * g + 2) * HEAD_DIM] = ones


def _conv_in_kernel(h_ref, wu_ref, wb_ref, wc_ref, z_ref, gb_ref):
    h = h_ref[...]
    u = jnp.dot(h, wu_ref[...], preferred_element_type=F32)
    gc = jnp.dot(h, wc_ref[...], preferred_element_type=F32)
    z_ref[...] = (gc * u).astype(z_ref.dtype)
    gb_ref[...] = jnp.dot(h, wb_ref[...], preferred_element_type=F32).astype(gb_ref.dtype)


def _gate_kernel(h_ref, w_ref, b_ref, o_ref):
    acc = jnp.dot(h_ref[...], w_ref[...], preferred_element_type=F32)
    o_ref[...] = jax.nn.sigmoid(acc + b_ref[...]).astype(o_ref.dtype)


def _rope_block(i, tm, n_lat, seq):
    return jnp.where(i * tm < n_lat, (i * tm % seq) // tm, seq // tm)


def _in_proj(h, w_in, q_g, k_g, b_gate, cos_t, sin_t, *, rows_q, rows_kv, n_lat, seq, d,
             tm=512, tn=512):
    kd = h.shape[1]
    kvw = d // KV_GROUPS
    q_end, k_end, v_end = d, d + kvw, d + 2 * kvw
    u_end, b_end, c_end = v_end + d, v_end + 2 * d, v_end + 3 * d
    hspec = pl.BlockSpec((tm, kd), lambda i, j: (i, 0))
    hd_spec = pl.BlockSpec((1, tn), lambda i, j: (0, 0))
    bd_spec = pl.BlockSpec((tn, tn), lambda i, j: (0, 0))
    heads = tn // HEAD_DIM
    lane = jnp.arange(HEAD_DIM)
    partner = jnp.where(lane % (HEAD_DIM // 2) < HEAD_DIM // 4, lane + HEAD_DIM // 4, lane - HEAD_DIM // 4)
    perm = (lane[:, None] == partner[None, :]).astype(BF16)
    ones_bd = jnp.kron(jnp.eye(heads, dtype=BF16), jnp.ones((HEAD_DIM, HEAD_DIM), BF16))
    perm_bd = jnp.kron(jnp.eye(heads, dtype=BF16), perm)
    tab_spec = pl.BlockSpec((tm, HEAD_DIM), lambda i, j: (_rope_block(i, tm, n_lat, seq), 0))

    def wspec(col0):
        return pl.BlockSpec((kd, tn), lambda i, j: (0, col0 // tn + j))

    def ospec():
        return pl.BlockSpec((tm, tn), lambda i, j: (i, j))

    q = pl.pallas_call(
        functools.partial(_q_kernel, out_scale=HEAD_DIM ** -0.5 * LOG2_E),
        grid=(rows_q // tm, d // tn),
        in_specs=[hspec, wspec(0), bd_spec, bd_spec, hd_spec, tab_spec, tab_spec],
        out_specs=ospec(),
        out_shape=jax.ShapeDtypeStruct((rows_q, d), BF16),
        compiler_params=_cparams("parallel", "arbitrary"),
        name="proj_q",
    )(h, w_in, ones_bd, perm_bd, jnp.tile(q_g, heads).reshape(1, tn), cos_t, sin_t)

    assert kvw == tn
    k, v = pl.pallas_call(
        _kv_kernel,
        grid=(rows_kv // tm, 1),
        in_specs=[hspec, wspec(q_end), wspec(k_end), bd_spec, bd_spec, hd_spec, tab_spec, tab_spec],
        out_specs=[ospec(), pl.BlockSpec((tm, 2 * tn), lambda i, j: (i, j))],
        out_shape=[jax.ShapeDtypeStruct((rows_kv, kvw), BF16), jax.ShapeDtypeStruct((rows_kv, 2 * kvw), BF16)],
        compiler_params=_cparams("parallel", "arbitrary"),
        name="proj_kv",
    )(h, w_in, w_in, ones_bd, perm_bd, jnp.tile(k_g, heads).reshape(1, tn), cos_t, sin_t)

    z, gb = pl.pallas_call(
        _conv_in_kernel,
        grid=(rows_q // tm, d // tn),
        in_specs=[hspec, wspec(v_end), wspec(u_end), wspec(b_end)],
        out_specs=[ospec(), ospec()],
        out_shape=[jax.ShapeDtypeStruct((rows_q, d), BF16)] * 2,
        compiler_params=_cparams("parallel", "arbitrary"),
        name="proj_conv",
    )(h, w_in, w_in, w_in)

    gates = pl.pallas_call(
        _gate_kernel,
        grid=(rows_q // tm, 2 * d // tn),
        in_specs=[hspec, wspec(c_end), pl.BlockSpec((1, tn), lambda i, j: (0, j))],
        out_specs=ospec(),
        out_shape=jax.ShapeDtypeStruct((rows_q, 2 * d), BF16),
        compiler_params=_cparams("parallel", "arbitrary"),
        name="proj_gates",
    )(h, w_in, b_gate.reshape(1, 2 * d))
    return q, k, v, z, gb, gates


ATTN_KEY_CHUNK = 256


def _attn_kernel(q_ref, kl_ref, vl_ref, kc_ref, vc_ref, o_ref, *, tq, n_lat):
    is_lat = pl.program_id(1) * tq < n_lat
    q = jnp.concatenate([q_ref[:, hh * HEAD_DIM:(hh + 1) * HEAD_DIM] for hh in range(KV_GROUPS)], axis=0)

    def step(carry, k, v):
        s = lax.dot_general(q, k, _NT, preferred_element_type=F32)
        m_new = jnp.max(s, axis=-1, keepdims=True)
        pv = lambda m: jnp.dot(jnp.exp2(s - m).astype(v.dtype), v, preferred_element_type=F32)
        if carry is None:
            return m_new, pv(m_new)
        m_old, acc = carry
        m_new = jnp.maximum(m_old, m_new)
        return m_new, acc * jnp.exp2(m_old - m_new) + pv(m_new)

    def run(with_lat):
        carry = step(None, kc_ref[...], vc_ref[...])
        if with_lat:
            for c in range(kl_ref.shape[0] // ATTN_KEY_CHUNK):
                rows = slice(c * ATTN_KEY_CHUNK, (c + 1) * ATTN_KEY_CHUNK)
                carry = step(carry, kl_ref[rows, :], vl_ref[rows, :])
        acc = carry[1]
        out = (acc[:, :HEAD_DIM] / acc[:, HEAD_DIM:]).astype(o_ref.dtype)
        for hh in range(KV_GROUPS):
            o_ref[:, hh * HEAD_DIM:(hh + 1) * HEAD_DIM] = out[hh * tq:(hh + 1) * tq, :]

    @pl.when(is_lat)
    def _():
        run(True)

    @pl.when(jnp.logical_not(is_lat))
    def _():
        run(False)


def _attention(q, k, v, *, rows_q, n_lat, seq, ctx_len, tq=256):
    assert ctx_len % tq == 0 or tq % ctx_len == 0
    assert tq <= ctx_len
    gw = KV_GROUPS * HEAD_DIM
    n_kv = k.shape[1] // HEAD_DIM

    def batch(i):
        return jnp.where(i * tq < n_lat, i * tq // seq, (i * tq - n_lat) // ctx_len)

    qspec = pl.BlockSpec((tq, gw), lambda g, i: (i, g))

    def lat_spec(width):
        return pl.BlockSpec((seq, width), lambda g, i: (batch(i), g))

    def ctx_spec(width):
        return pl.BlockSpec((ctx_len, width), lambda g, i: (n_lat // ctx_len + batch(i), g))

    return pl.pallas_call(
        functools.partial(_attn_kernel, tq=tq, n_lat=n_lat),
        grid=(n_kv, rows_q // tq),
        in_specs=[qspec, lat_spec(HEAD_DIM), lat_spec(2 * HEAD_DIM), ctx_spec(HEAD_DIM), ctx_spec(2 * HEAD_DIM)],
        out_specs=qspec,
        out_shape=jax.ShapeDtypeStruct((rows_q, q.shape[1]), BF16),
        compiler_params=_cparams("parallel", "arbitrary"),
        name="attention",
    )(q, k, v, k, v)


def _merge_kernel(attn_ref, z_ref, zp_ref, zn_ref, gb_ref, ga_ref, gc_ref, cw_ref,
                  wao_ref, wco_ref, o_ref, y_scr, *, tm, n_lat, seq, ctx_len):
    @pl.when(pl.program_id(1) == 0)
    def _():
        z = z_ref[...].astype(F32)
        loc = lax.broadcasted_iota(jnp.int32, (tm, 1), 0)
        row = pl.program_id(0) * tm + loc
        seqlen = jnp.where(row < n_lat, seq, ctx_len)
        pos = row & (seqlen - 1)
        halo_prev = zp_ref[...].astype(F32)[BF16_SUBLANES - 1:BF16_SUBLANES, :]
        halo_next = zn_ref[...].astype(F32)[0:1, :]
        z_prev = jnp.where(loc == 0, halo_prev, pltpu.roll(z, 1, 0))
        z_prev = jnp.where(pos == 0, 0.0, z_prev)
        z_next = jnp.where(loc == tm - 1, halo_next, pltpu.roll(z, tm - 1, 0))
        z_next = jnp.where(pos == seqlen - 1, 0.0, z_next)
        cw = cw_ref[...]
        conv = cw[0:1, :] * z_prev + cw[1:2, :] * z + cw[2:3, :] * z_next
        y_scr[...] = (gb_ref[...].astype(F32) * conv).astype(y_scr.dtype)

    acc_a = jnp.dot(attn_ref[...], wao_ref[...], preferred_element_type=F32)
    acc_c = jnp.dot(y_scr[...], wco_ref[...], preferred_element_type=F32)
    o_ref[...] = (ga_ref[...].astype(F32) * acc_a + gc_ref[...].astype(F32) * acc_c).astype(o_ref.dtype)


def _merge(attn, z, gb, gates, conv_w, w_ao, w_co, *, rows, n_lat, seq, ctx_len, tm=512, tn=512):
    d = attn.shape[1]
    hb = BF16_SUBLANES
    n_hblk = z.shape[0] // hb
    row_spec = pl.BlockSpec((tm, d), lambda i, j: (i, 0))
    prev_spec = pl.BlockSpec((hb, d), lambda i, j: (jnp.maximum(i * (tm // hb) - 1, 0), 0))
    next_spec = pl.BlockSpec((hb, d), lambda i, j: (jnp.minimum((i + 1) * (tm // hb), n_hblk - 1), 0))
    wspec = pl.BlockSpec((d, tn), lambda i, j: (0, j))
    return pl.pallas_call(
        functools.partial(_merge_kernel, tm=tm, n_lat=n_lat, seq=seq, ctx_len=ctx_len),
        grid=(rows // tm, d // tn),
        in_specs=[row_spec, row_spec, prev_spec, next_spec, row_spec,
                  pl.BlockSpec((tm, tn), lambda i, j: (i, j)),
                  pl.BlockSpec((tm, tn), lambda i, j: (i, d // tn + j)),
                  pl.BlockSpec(conv_w.shape, lambda i, j: (0, 0)),
                  wspec, wspec],
        out_specs=pl.BlockSpec((tm, tn), lambda i, j: (i, j)),
        out_shape=jax.ShapeDtypeStruct((rows, d), BF16),
        scratch_shapes=[pltpu.VMEM((tm, d), BF16)],
        compiler_params=_cparams("parallel", "arbitrary"),
        name="merge",
    )(attn, z, z, z, gb, gates, gates, conv_w, w_ao, w_co)


def _resid_proj_kernel(a_ref, w_ref, *refs, split, tm, tn, n_lat, seq, n_batch, k_gate, d):
    x_ref, xc_ref = (refs[0], refs[1]) if split else (refs[0], None)
    mod_ref, o_ref = refs[2 if split else 1:]
    row0 = pl.program_id(0) * tm
    b = _mod_row(row0, n_lat, seq, n_batch)
    col0 = pl.multiple_of(k_gate * d + pl.program_id(1) * tn, LANES)
    gate = mod_ref[pl.ds(b, 1), pl.ds(col0, tn)]
    acc = jnp.dot(a_ref[...], w_ref[...], preferred_element_type=F32)
    o_ref[...] = _tile_of(x_ref, xc_ref, row0, n_lat) + gate * acc


def _resid_proj(a, w, x, mod, *, rows, n_lat, seq, n_batch, k_gate, x_ctx=None, tm=512, tn=512):
    d = a.shape[1]
    xs = [x] if x_ctx is None else [x, x_ctx]
    return pl.pallas_call(
        functools.partial(_resid_proj_kernel, split=x_ctx is not None, tm=tm, tn=tn, n_lat=n_lat, seq=seq,
                          n_batch=n_batch, k_gate=k_gate, d=d),
        grid=(rows // tm, d // tn),
        in_specs=[pl.BlockSpec((tm, d), lambda i, j: (i, 0)),
                  pl.BlockSpec((d, tn), lambda i, j: (0, j))]
        + _row_specs(x_ctx, (tm, tn), n_lat, col_of=lambda j: j)
        + [pl.BlockSpec(mod.shape, lambda i, j: (0, 0))],
        out_specs=pl.BlockSpec((tm, tn), lambda i, j: (i, j)),
        out_shape=jax.ShapeDtypeStruct((rows, d), F32),
        compiler_params=_cparams("parallel", "arbitrary"),
        name="resid_proj",
    )(a, w, *xs, mod)


N_TOP = PEER_TOPK + 1
TOP_ROWS = 24
N_CAND = 96


def _top_values(s, k, out_ref):
    cur = s
    prev = mx = None
    for r in range(k):
        prev = mx
        mx = jnp.max(cur, axis=0, keepdims=True)
        if out_ref is not None:
            out_ref[r:r + 1, :] = mx
        cur = jnp.where(cur == mx, -jnp.inf, cur)
    return prev, mx


def _candidate_sums(v1_ref, v2_ref, cand_ref):
    cand_ref[0:TOP_ROWS, :] = v1_ref[0:1, :] + v2_ref[...]
    for a in range(1, 8):
        cand_ref[16 + 8 * a:24 + 8 * a, :] = v1_ref[a:a + 1, :] + v2_ref[0:8, :]
    cand_ref[80:96, :] = v1_ref[8:TOP_ROWS, :] + v2_ref[0:1, :]


def _peer_score_kernel(h_ref, wq_ref, k1_ref, k2_ref, th_ref, e1_ref, e2_ref,
                       v1_scr, v2_scr, cand_scr, *, n_i):
    qp = jnp.dot(h_ref[...], wq_ref[...], preferred_element_type=F32)
    pad = jnp.full((TOP_ROWS - N_TOP, v1_scr.shape[1]), -jnp.inf, F32)
    v1_scr[N_TOP:TOP_ROWS, :] = pad
    v2_scr[N_TOP:TOP_ROWS, :] = pad
    for h in range(PEER_HEADS):
        q1 = qp[:, (2 * h) * N_KEYS:(2 * h + 1) * N_KEYS]
        q2 = qp[:, (2 * h + 1) * N_KEYS:(2 * h + 2) * N_KEYS]
        s1 = lax.dot_general(k1_ref[h], q1, _NT, preferred_element_type=F32) * LOG2_E
        s2 = lax.dot_general(k2_ref[h], q2, _NT, preferred_element_type=F32) * LOG2_E
        _top_values(s1, N_TOP, v1_scr)
        _top_values(s2, N_TOP, v2_scr)
        _candidate_sums(v1_scr, v2_scr, cand_scr)
        cand = cand_scr[...]
        c16, c17 = _top_values(cand, N_TOP, None)
        tau = 0.5 * (c16 + c17)
        top1 = v1_scr[0:1, :]
        top2 = v2_scr[0:1, :]
        z = jnp.sum(jnp.where(cand >= c16, jnp.exp2(cand - (top1 + top2)), 0.0), axis=0, keepdims=True)
        theta = jnp.exp2(tau - s1 - top2)
        e1 = jnp.exp2(s1 - (top1 + jnp.log2(z)))
        for g in range(N_KEYS // n_i):
            th_ref[g, h * n_i:(h + 1) * n_i, :] = theta[g * n_i:(g + 1) * n_i, :]
            e1_ref[g, h * n_i:(h + 1) * n_i, :] = e1[g * n_i:(g + 1) * n_i, :]
        e2_ref[h] = jnp.exp2(s2 - top2)


def _peer_scores(h2, wq, k1, k2, *, rows, n_i, tm=256):
    d = h2.shape[1]
    n_grp = N_KEYS // n_i
    grouped = pl.BlockSpec((n_grp, PEER_HEADS * n_i, tm), lambda i: (0, 0, i))
    by_head = pl.BlockSpec((PEER_HEADS, N_KEYS, tm), lambda i: (0, 0, i))
    grouped_shape = jax.ShapeDtypeStruct((n_grp, PEER_HEADS * n_i, rows), F32)
    by_head_shape = jax.ShapeDtypeStruct((PEER_HEADS, N_KEYS, rows), F32)
    return pl.pallas_call(
        functools.partial(_peer_score_kernel, n_i=n_i),
        grid=(rows // tm,),
        in_specs=[pl.BlockSpec((tm, d), lambda i: (i, 0)),
                  pl.BlockSpec(wq.shape, lambda i: (0, 0)),
                  pl.BlockSpec(k1.shape, lambda i: (0, 0, 0)),
                  pl.BlockSpec(k2.shape, lambda i: (0, 0, 0))],
        out_specs=[grouped, grouped, by_head],
        out_shape=[grouped_shape, grouped_shape, by_head_shape],
        scratch_shapes=[pltpu.VMEM((TOP_ROWS, tm), F32), pltpu.VMEM((TOP_ROWS, tm), F32),
                        pltpu.VMEM((N_CAND, tm), F32)],
        compiler_params=_cparams("parallel"),
        name="peer_scores",
    )(h2, wq, k1, k2)


def _gelu(x):
    return (0.5 * x) * (1.0 + lax.erf(x * (0.5 ** 0.5)))


def _peer_mix_kernel(ht_ref, u_ref, vt_ref, th_ref, e1_ref, e2_ref, x_ref, mod_ref, *rest,
                     tt, te, n_lat, seq, n_batch, k_gate, final_norm):
    if final_norm:
        fg_ref, o_ref, acc_ref, act_scr, w_scr = rest
    else:
        o_ref, acc_ref, act_scr, w_scr = rest
    j = pl.program_id(1)
    last = pl.num_programs(1) - 1
    d = x_ref.shape[1]
    n_i = te // N_KEYS

    @pl.when(jnp.logical_and(pl.program_id(0) == 0, j == 0))
    def _():
        acc_ref[...] = jnp.zeros_like(acc_ref)

    act_scr[j % 2] = _gelu(jnp.dot(u_ref[...], ht_ref[...], preferred_element_type=F32))

    for ii in range(n_i):
        rs = slice(ii * N_KEYS, (ii + 1) * N_KEYS)
        for tc in range(tt // LANES):
            cs = slice(tc * LANES, (tc + 1) * LANES)
            gate = None
            for h in range(PEER_HEADS):
                row = slice(h * n_i + ii, h * n_i + ii + 1)
                e2 = e2_ref[h, :, cs]
                wgt = jnp.where(e2 >= th_ref[0, row, cs], e2, 0.0) * e1_ref[0, row, cs]
                gate = wgt if gate is None else gate + wgt
            wblk = jnp.where(j >= 1, gate * act_scr[(j + 1) % 2, rs, cs], 0.0)
            w_scr[rs, cs] = wblk.astype(w_scr.dtype)

    acc_ref[...] += jnp.dot(vt_ref[...], w_scr[...], preferred_element_type=F32)

    @pl.when(j == last)
    def _():
        b = _mod_row(pl.program_id(0) * tt, n_lat, seq, n_batch)
        xn = x_ref[...] + _mod_vec(mod_ref, b, k_gate, d) * acc_ref[...].T
        if final_norm:
            xn = xn * lax.rsqrt(jnp.mean(xn * xn, axis=-1, keepdims=True) + EPS) * fg_ref[...]
        o_ref[...] = xn
        acc_ref[...] = jnp.zeros_like(acc_ref)


PEER_TE = 1024


def _peer_mix(h2t, u, vt, th, e1, e2, x, mod, final_g, *, rows, n_lat, seq, n_batch, k_gate,
              tt=512, te=PEER_TE):
    d = x.shape[1]
    n_tiles = u.shape[0] // te
    final_norm = final_g is not None
    n_i = te // N_KEYS

    def cur(j):
        return jnp.minimum(j, n_tiles - 1)

    def prv(j):
        return jnp.maximum(j - 1, 0)

    grouped = pl.BlockSpec((1, PEER_HEADS * n_i, tt), lambda t, j: (prv(j), 0, t))
    once = pl.Buffered(1)
    by_head = pl.BlockSpec((PEER_HEADS, N_KEYS, tt), lambda t, j: (0, 0, t), pipeline_mode=once)
    in_specs = [pl.BlockSpec((d, tt), lambda t, j: (0, t), pipeline_mode=once),
                pl.BlockSpec((te, d), lambda t, j: (cur(j), 0)),
                pl.BlockSpec((d, te), lambda t, j: (0, prv(j))),
                grouped, grouped, by_head,
                pl.BlockSpec((tt, d), lambda t, j: (t, 0), pipeline_mode=once),
                pl.BlockSpec(mod.shape, lambda t, j: (0, 0), pipeline_mode=once)]
    args = [h2t, u, vt, th, e1, e2, x, mod]
    if final_norm:
        in_specs.append(pl.BlockSpec((1, d), lambda t, j: (0, 0)))
        args.append(final_g.reshape(1, d))
    return pl.pallas_call(
        functools.partial(_peer_mix_kernel, tt=tt, te=te, n_lat=n_lat, seq=seq, n_batch=n_batch,
                          k_gate=k_gate, final_norm=final_norm),
        grid=(rows // tt, n_tiles + 1),
        in_specs=in_specs,
        out_specs=pl.BlockSpec((tt, d), lambda t, j: (t, 0)),
        out_shape=jax.ShapeDtypeStruct((rows, d), F32),
        scratch_shapes=[pltpu.VMEM((d, tt), F32), pltpu.VMEM((2, te, tt), F32), pltpu.VMEM((te, tt), BF16)],
        compiler_params=_cparams("arbitrary", "arbitrary"),
        name="peer_mix",
    )(*args)


def _rope_tables(seq, pad_rows):
    half = HEAD_DIM // 2
    t = jnp.arange(seq)
    row = (t // GRID_W).astype(F32)
    col = (t % GRID_W).astype(F32)
    inv = ROPE_THETA ** (-jnp.arange(0, half, 2, dtype=F32) / half)
    ang_r = row[:, None] * inv[None, :]
    ang_c = col[:, None] * inv[None, :]
    cos = jnp.concatenate([jnp.cos(ang_r), jnp.cos(ang_r), jnp.cos(ang_c), jnp.cos(ang_c)], axis=-1)
    sin = jnp.concatenate([-jnp.sin(ang_r), jnp.sin(ang_r), -jnp.sin(ang_c), jnp.sin(ang_c)], axis=-1)
    cos = jnp.concatenate([cos, jnp.ones((pad_rows, HEAD_DIM), F32)], axis=0)
    sin = jnp.concatenate([sin, jnp.zeros((pad_rows, HEAD_DIM), F32)], axis=0)
    return cos, sin


def kernel(x, c, ctx, c_ctx, ada_w, ada_b, norm1_g, norm2_g, w_in, b_gate, q_norm_g, k_norm_g,
           conv_w, w_attn_out, w_conv_out, w_o, peer_wq, peer_k1, peer_k2, peer_u, peer_v, final_g):
    n_batch, seq, d = x.shape
    ctx_len = ctx.shape[1]
    depth = ada_w.shape[0]
    n_lat = n_batch * seq
    n_all = n_lat + n_batch * ctx_len
    assert seq & (seq - 1) == 0 and ctx_len & (ctx_len - 1) == 0
    tm = 512

    xa = x.reshape(n_lat, d)
    xc = ctx.reshape(n_batch * ctx_len, d)
    pad = (-(n_batch + 1)) % 8
    cvec = jnp.concatenate([c, c_ctx[None, :], jnp.zeros((pad, d), F32)], axis=0)
    mod_all = _adaln(cvec, ada_w, ada_b)
    cos_t, sin_t = _rope_tables(seq, tm)
    geom = dict(n_lat=n_lat, seq=seq, n_batch=n_batch)

    for l in range(depth):
        last = l == depth - 1
        rows = n_lat if last else n_all
        mod = mod_all[l]
        w_in_l = w_in[l].astype(BF16)

        h = _norm_mod(xa, norm1_g[l], mod, rows=n_all, k_shift=0, k_scale=1, x_ctx=xc, **geom)
        q, k, v, z, gb, gates = _in_proj(h, w_in_l, q_norm_g[l], k_norm_g[l], b_gate[l], cos_t, sin_t,
                                         rows_q=rows, rows_kv=n_all, n_lat=n_lat, seq=seq, d=d, tm=tm)
        attn = _attention(q, k, v, rows_q=rows, n_lat=n_lat, seq=seq, ctx_len=ctx_len)
        merged = _merge(attn, z, gb, gates, conv_w[l], w_attn_out[l].astype(BF16),
                        w_conv_out[l].astype(BF16), rows=rows, n_lat=n_lat, seq=seq, ctx_len=ctx_len)
        xa = _resid_proj(merged, w_o[l].astype(BF16), xa, mod, rows=rows, k_gate=2, x_ctx=xc, **geom)
        xc = None

        h2, h2t = _norm_mod(xa, norm2_g[l], mod, rows=rows, k_shift=3, k_scale=4, with_transpose=True, **geom)
        th, e1, e2 = _peer_scores(h2, peer_wq[l].astype(BF16), peer_k1[l], peer_k2[l], rows=rows,
                                  n_i=PEER_TE // N_KEYS)
        xa = _peer_mix(h2t, peer_u[l].astype(BF16), peer_v[l].T.astype(BF16), th, e1, e2, xa, mod,
                       final_g if last else None, rows=rows, k_gate=5, **geom)
    return xa.reshape(n_batch, seq, d)
```

```python
import functools

import jax
import jax.numpy as jnp
from jax import lax
from jax.experimental import pallas as pl
from jax.experimental.pallas import tpu as pltpu

F32 = jnp.float32
BF16 = jnp.bfloat16

HEAD_DIM = 128
KV_GROUPS = 4
GRID_W = 64
ROPE_THETA = 10000.0
PEER_HEADS = 8
N_KEYS = 128
PEER_TOPK = 16
N_MOD = 6
EPS = 1e-6
LOG2_E = 1.4426950408889634
LANES = 128
BF16_SUBLANES = 16
VMEM_LIMIT_BYTES = 56 * 1024 * 1024

_NT = (((1,), (1,)), ((), ()))


def _cparams(*sem, flags=None):
    return pltpu.CompilerParams(dimension_semantics=sem, vmem_limit_bytes=VMEM_LIMIT_BYTES, flags=flags)


def _mod_row(row0, n_lat, seq, n_batch):
    return jnp.where(row0 < n_lat, row0 // seq, n_batch)


def _mod_vec(mod_ref, b, k, d):
    return mod_ref[pl.ds(b, 1), k * d:(k + 1) * d]


def _adaln_kernel(c_ref, w_ref, b_ref, o_ref):
    cv = c_ref[...]
    s = cv * jax.nn.sigmoid(cv)
    o_ref[0] = jnp.dot(s, w_ref[0], preferred_element_type=F32) + b_ref[0]


def _adaln(cvec, ada_w, ada_b, tn=1024):
    depth, d, n = ada_w.shape
    rows = cvec.shape[0]
    return pl.pallas_call(
        _adaln_kernel,
        grid=(depth, n // tn),
        in_specs=[pl.BlockSpec((rows, d), lambda l, j: (0, 0)),
                  pl.BlockSpec((1, d, tn), lambda l, j: (l, 0, j)),
                  pl.BlockSpec((1, 1, tn), lambda l, j: (l, 0, j))],
        out_specs=pl.BlockSpec((1, rows, tn), lambda l, j: (l, 0, j)),
        out_shape=jax.ShapeDtypeStruct((depth, rows, n), F32),
        compiler_params=_cparams("arbitrary", "arbitrary"),
        name="adaln",
    )(cvec, ada_w, ada_b.reshape(depth, 1, n))


def _tile_of(lat_ref, ctx_ref, row0, n_lat):
    if ctx_ref is None:
        return lat_ref[...]
    return jnp.where(row0 < n_lat, lat_ref[...], ctx_ref[...])


def _row_specs(x_ctx, block, n_lat, col_of=None):
    n_lat_tiles = n_lat // block[0]
    col = (lambda *j: 0) if col_of is None else col_of
    if x_ctx is None:
        return [pl.BlockSpec(block, lambda i, *j: (i, col(*j)))]
    return [pl.BlockSpec(block, lambda i, *j: (jnp.minimum(i, n_lat_tiles - 1), col(*j))),
            pl.BlockSpec(block, lambda i, *j: (jnp.maximum(i - n_lat_tiles, 0), col(*j)))]


def _norm_mod_kernel(*refs, split, tm, n_lat, seq, n_batch, k_shift, k_scale):
    x_ref, xc_ref = (refs[0], refs[1]) if split else (refs[0], None)
    g_ref, mod_ref, o_ref, *maybe_ot_ref = refs[2 if split else 1:]
    d = x_ref.shape[1]
    row0 = pl.program_id(0) * tm
    b = _mod_row(row0, n_lat, seq, n_batch)
    xf = _tile_of(x_ref, xc_ref, row0, n_lat)
    y = xf * lax.rsqrt(jnp.mean(xf * xf, axis=-1, keepdims=True) + EPS)
    h = y * g_ref[...]
    shift = _mod_vec(mod_ref, b, k_shift, d)
    scale = _mod_vec(mod_ref, b, k_scale, d)
    h = h * (1.0 + scale) + shift
    o_ref[...] = h.astype(o_ref.dtype)
    for ot_ref in maybe_ot_ref:
        ot_ref[...] = h.T.astype(ot_ref.dtype)


def _norm_mod(x, g, mod, *, rows, n_lat, seq, n_batch, k_shift, k_scale, x_ctx=None,
              with_transpose=False, tm=512):
    d = x.shape[1]
    kern = functools.partial(_norm_mod_kernel, split=x_ctx is not None, tm=tm, n_lat=n_lat, seq=seq,
                             n_batch=n_batch, k_shift=k_shift, k_scale=k_scale)
    xs = [x] if x_ctx is None else [x, x_ctx]
    out_specs = [pl.BlockSpec((tm, d), lambda i: (i, 0))]
    out_shape = [jax.ShapeDtypeStruct((rows, d), BF16)]
    if with_transpose:
        out_specs.append(pl.BlockSpec((d, tm), lambda i: (0, i)))
        out_shape.append(jax.ShapeDtypeStruct((d, rows), BF16))
    out = pl.pallas_call(
        kern,
        grid=(rows // tm,),
        in_specs=_row_specs(x_ctx, (tm, d), n_lat) + [pl.BlockSpec((1, d), lambda i: (0, 0)),
                                                         pl.BlockSpec(mod.shape, lambda i: (0, 0))],
        out_specs=out_specs,
        out_shape=out_shape,
        compiler_params=_cparams("parallel"),
        name="norm_mod",
    )(*xs, g.reshape(1, d), mod)
    return out if with_transpose else out[0]


def _heads_norm_rope(acc, ones_bd, perm_bd, g, cos, sin, out_scale):
    n_heads = acc.shape[1] // HEAD_DIM
    ss = jnp.dot((acc * acc).astype(BF16), ones_bd, preferred_element_type=F32)
    y = acc * lax.rsqrt(ss * (1.0 / HEAD_DIM) + EPS) * g
    partner = jnp.dot(y.astype(BF16), perm_bd, preferred_element_type=F32)
    r = y * jnp.tile(cos, (1, n_heads)) + partner * jnp.tile(sin, (1, n_heads))
    if out_scale != 1.0:
        r = r * out_scale
    return r


def _q_kernel(h_ref, w_ref, ones_ref, perm_ref, g_ref, cos_ref, sin_ref, o_ref, *, out_scale):
    acc = jnp.dot(h_ref[...], w_ref[...], preferred_element_type=F32)
    r = _heads_norm_rope(acc, ones_ref[...], perm_ref[...], g_ref[...], cos_ref[...], sin_ref[...], out_scale)
    o_ref[...] = r.astype(o_ref.dtype)


def _kv_kernel(h_ref, wk_ref, wv_ref, ones_ref, perm_ref, g_ref, cos_ref, sin_ref, k_ref, v_ref):
    h = h_ref[...]
    acc = jnp.dot(h, wk_ref[...], preferred_element_type=F32)
    r = _heads_norm_rope(acc, ones_ref[...], perm_ref[...], g_ref[...], cos_ref[...], sin_ref[...], 1.0)
    k_ref[...] = r.astype(k_ref.dtype)
    vacc = jnp.dot(h, wv_ref[...], preferred_element_type=F32).astype(v_ref.dtype)
    ones = jnp.ones((vacc.shape[0], HEAD_DIM), v_ref.dtype)
    for g in range(vacc.shape[1] // HEAD_DIM):
        v_ref[:, 2 * g * HEAD_DIM:(2 * g + 1) * HEAD_DIM] = vacc[:, g * HEAD_DIM:(g + 1) * HEAD_DIM]
        v_ref[:, (2 * g + 1) * HEAD_DIM:(2 * g + 2) * HEAD_DIM] = ones


def _conv_in_kernel(h_ref, wu_ref, wb_ref, wc_ref, z_ref, gb_ref):
    h = h_ref[...]
    u = jnp.dot(h, wu_ref[...], preferred_element_type=F32)
    gc = jnp.dot(h, wc_ref[...], preferred_element_type=F32)
    z_ref[...] = (gc * u).astype(z_ref.dtype)
    gb_ref[...] = jnp.dot(h, wb_ref[...], preferred_element_type=F32).astype(gb_ref.dtype)


def _gate_kernel(h_ref, w_ref, b_ref, o_ref):
    acc = jnp.dot(h_ref[...], w_ref[...], preferred_element_type=F32)
    o_ref[...] = jax.nn.sigmoid(acc + b_ref[...]).astype(o_ref.dtype)


def _rope_block(i, tm, n_lat, seq):
    return jnp.where(i * tm < n_lat, (i * tm % seq) // tm, seq // tm)


def _in_proj(h, w_in, q_g, k_g, b_gate, cos_t, sin_t, *, rows_q, rows_kv, n_lat, seq, d,
             tm=512, tn=512):
    kd = h.shape[1]
    kvw = d // KV_GROUPS
    q_end, k_end, v_end = d, d + kvw, d + 2 * kvw
    u_end, b_end, c_end = v_end + d, v_end + 2 * d, v_end + 3 * d
    hspec = pl.BlockSpec((tm, kd), lambda i, j: (i, 0))
    hd_spec = pl.BlockSpec((1, tn), lambda i, j: (0, 0))
    bd_spec = pl.BlockSpec((tn, tn), lambda i, j: (0, 0))
    heads = tn // HEAD_DIM
    lane = jnp.arange(HEAD_DIM)
    partner = jnp.where(lane % (HEAD_DIM // 2) < HEAD_DIM // 4, lane + HEAD_DIM // 4, lane - HEAD_DIM // 4)
    perm = (lane[:, None] == partner[None, :]).astype(BF16)
    ones_bd = jnp.kron(jnp.eye(heads, dtype=BF16), jnp.ones((HEAD_DIM, HEAD_DIM), BF16))
    perm_bd = jnp.kron(jnp.eye(heads, dtype=BF16), perm)
    tab_spec = pl.BlockSpec((tm, HEAD_DIM), lambda i, j: (_rope_block(i, tm, n_lat, seq), 0))

    def wspec(col0):
        return pl.BlockSpec((kd, tn), lambda i, j: (0, col0 // tn + j))

    def ospec():
        return pl.BlockSpec((tm, tn), lambda i, j: (i, j))

    q = pl.pallas_call(
        functools.partial(_q_kernel, out_scale=HEAD_DIM ** -0.5 * LOG2_E),
        grid=(rows_q // tm, d // tn),
        in_specs=[hspec, wspec(0), bd_spec, bd_spec, hd_spec, tab_spec, tab_spec],
        out_specs=ospec(),
        out_shape=jax.ShapeDtypeStruct((rows_q, d), BF16),
        compiler_params=_cparams("parallel", "arbitrary"),
        name="proj_q",
    )(h, w_in, ones_bd, perm_bd, jnp.tile(q_g, heads).reshape(1, tn), cos_t, sin_t)

    assert kvw == tn
    k, v = pl.pallas_call(
        _kv_kernel,
        grid=(rows_kv // tm, 1),
        in_specs=[hspec, wspec(q_end), wspec(k_end), bd_spec, bd_spec, hd_spec, tab_spec, tab_spec],
        out_specs=[ospec(), pl.BlockSpec((tm, 2 * tn), lambda i, j: (i, j))],
        out_shape=[jax.ShapeDtypeStruct((rows_kv, kvw), BF16), jax.ShapeDtypeStruct((rows_kv, 2 * kvw), BF16)],
        compiler_params=_cparams("parallel", "arbitrary"),
        name="proj_kv",
    )(h, w_in, w_in, ones_bd, perm_bd, jnp.tile(k_g, heads).reshape(1, tn), cos_t, sin_t)

    z, gb = pl.pallas_call(
        _conv_in_kernel,
        grid=(rows_q // tm, d // tn),
        in_specs=[hspec, wspec(v_end), wspec(u_end), wspec(b_end)],
        out_specs=[ospec(), ospec()],
        out_shape=[jax.ShapeDtypeStruct((rows_q, d), BF16)] * 2,
        compiler_params=_cparams("parallel", "arbitrary"),
        name="proj_conv",
    )(h, w_in, w_in, w_in)

    gates = pl.pallas_call(
        _gate_kernel,
        grid=(rows_q // tm, 2 * d // tn),
        in_specs=[hspec, wspec(c_end), pl.BlockSpec((1, tn), lambda i, j: (0, j))],
        out_specs=ospec(),
        out_shape=jax.ShapeDtypeStruct((rows_q, 2 * d), BF16),
        compiler_params=_cparams("parallel", "arbitrary"),
        name="proj_gates",
    )(h, w_in, b_gate.reshape(1, 2 * d))
    return q, k, v, z, gb, gates


ATTN_KEY_CHUNK = 256


def _attn_kernel(q_ref, kl_ref, vl_ref, kc_ref, vc_ref, o_ref, *, tq, n_lat):
    is_lat = pl.program_id(1) * tq < n_lat
    q = jnp.concatenate([q_ref[:, hh * HEAD_DIM:(hh + 1) * HEAD_DIM] for hh in range(KV_GROUPS)], axis=0)

    def step(carry, k, v):
        s = lax.dot_general(q, k, _NT, preferred_element_type=F32)
        m_new = jnp.max(s, axis=-1, keepdims=True)
        pv = lambda m: jnp.dot(jnp.exp2(s - m).astype(v.dtype), v, preferred_element_type=F32)
        if carry is None:
            return m_new, pv(m_new)
        m_old, acc = carry
        m_new = jnp.maximum(m_old, m_new)
        return m_new, acc * jnp.exp2(m_old - m_new) + pv(m_new)

    def run(with_lat):
        carry = step(None, kc_ref[...], vc_ref[...])
        if with_lat:
            for c in range(kl_ref.shape[0] // ATTN_KEY_CHUNK):
                rows = slice(c * ATTN_KEY_CHUNK, (c + 1) * ATTN_KEY_CHUNK)
                carry = step(carry, kl_ref[rows, :], vl_ref[rows, :])
        acc = carry[1]
        out = (acc[:, :HEAD_DIM] / acc[:, HEAD_DIM:]).astype(o_ref.dtype)
        for hh in range(KV_GROUPS):
            o_ref[:, hh * HEAD_DIM:(hh + 1) * HEAD_DIM] = out[hh * tq:(hh + 1) * tq, :]

    @pl.when(is_lat)
    def _():
        run(True)

    @pl.when(jnp.logical_not(is_lat))
    def _():
        run(False)


def _attention(q, k, v, *, rows_q, n_lat, seq, ctx_len, tq=256):
    assert ctx_len % tq == 0 or tq % ctx_len == 0
    assert tq <= ctx_len
    gw = KV_GROUPS * HEAD_DIM
    n_kv = k.shape[1] // HEAD_DIM

    def batch(i):
        return jnp.where(i * tq < n_lat, i * tq // seq, (i * tq - n_lat) // ctx_len)

    qspec = pl.BlockSpec((tq, gw), lambda g, i: (i, g))

    def lat_spec(width):
        return pl.BlockSpec((seq, width), lambda g, i: (batch(i), g))

    def ctx_spec(width):
        return pl.BlockSpec((ctx_len, width), lambda g, i: (n_lat // ctx_len + batch(i), g))

    return pl.pallas_call(
        functools.partial(_attn_kernel, tq=tq, n_lat=n_lat),
        grid=(n_kv, rows_q // tq),
        in_specs=[qspec, lat_spec(HEAD_DIM), lat_spec(2 * HEAD_DIM), ctx_spec(HEAD_DIM), ctx_spec(2 * HEAD_DIM)],
        out_specs=qspec,
        out_shape=jax.ShapeDtypeStruct((rows_q, q.shape[1]), BF16),
        compiler_params=_cparams("parallel", "arbitrary"),
        name="attention",
    )(q, k, v, k, v)


def _merge_kernel(attn_ref, z_ref, zp_ref, zn_ref, gb_ref, ga_ref, gc_ref, cw_ref,
                  wao_ref, wco_ref, o_ref, y_scr, *, tm, n_lat, seq, ctx_len):
    @pl.when(pl.program_id(1) == 0)
    def _():
        z = z_ref[...].astype(F32)
        loc = lax.broadcasted_iota(jnp.int32, (tm, 1), 0)
        row = pl.program_id(0) * tm + loc
        seqlen = jnp.where(row < n_lat, seq, ctx_len)
        pos = row & (seqlen - 1)
        halo_prev = zp_ref[...].astype(F32)[BF16_SUBLANES - 1:BF16_SUBLANES, :]
        halo_next = zn_ref[...].astype(F32)[0:1, :]
        z_prev = jnp.where(loc == 0, halo_prev, pltpu.roll(z, 1, 0))
        z_prev = jnp.where(pos == 0, 0.0, z_prev)
        z_next = jnp.where(loc == tm - 1, halo_next, pltpu.roll(z, tm - 1, 0))
        z_next = jnp.where(pos == seqlen - 1, 0.0, z_next)
        cw = cw_ref[...]
        conv = cw[0:1, :] * z_prev + cw[1:2, :] * z + cw[2:3, :] * z_next
        y_scr[...] = (gb_ref[...].astype(F32) * conv).astype(y_scr.dtype)

    acc_a = jnp.dot(attn_ref[...], wao_ref[...], preferred_element_type=F32)
    acc_c = jnp.dot(y_scr[...], wco_ref[...], preferred_element_type=F32)
    o_ref[...] = (ga_ref[...].astype(F32) * acc_a + gc_ref[...].astype(F32) * acc_c).astype(o_ref.dtype)


def _merge(attn, z, gb, gates, conv_w, w_ao, w_co, *, rows, n_lat, seq, ctx_len, tm=512, tn=512):
    d = attn.shape[1]
    hb = BF16_SUBLANES
    n_hblk = z.shape[0] // hb
    row_spec = pl.BlockSpec((tm, d), lambda i, j: (i, 0))
    prev_spec = pl.BlockSpec((hb, d), lambda i, j: (jnp.maximum(i * (tm // hb) - 1, 0), 0))
    next_spec = pl.BlockSpec((hb, d), lambda i, j: (jnp.minimum((i + 1) * (tm // hb), n_hblk - 1), 0))
    wspec = pl.BlockSpec((d, tn), lambda i, j: (0, j))
    return pl.pallas_call(
        functools.partial(_merge_kernel, tm=tm, n_lat=n_lat, seq=seq, ctx_len=ctx_len),
        grid=(rows // tm, d // tn),
        in_specs=[row_spec, row_spec, prev_spec, next_spec, row_spec,
                  pl.BlockSpec((tm, tn), lambda i, j: (i, j)),
                  pl.BlockSpec((tm, tn), lambda i, j: (i, d // tn + j)),
                  pl.BlockSpec(conv_w.shape, lambda i, j: (0, 0)),
                  wspec, wspec],
        out_specs=pl.BlockSpec((tm, tn), lambda i, j: (i, j)),
        out_shape=jax.ShapeDtypeStruct((rows, d), BF16),
        scratch_shapes=[pltpu.VMEM((tm, d), BF16)],
        compiler_params=_cparams("parallel", "arbitrary"),
        name="merge",
    )(attn, z, z, z, gb, gates, gates, conv_w, w_ao, w_co)


def _resid_proj_kernel(a_ref, w_ref, *refs, split, tm, tn, n_lat, seq, n_batch, k_gate, d):
    x_ref, xc_ref = (refs[0], refs[1]) if split else (refs[0], None)
    mod_ref, o_ref = refs[2 if split else 1:]
    row0 = pl.program_id(0) * tm
    b = _mod_row(row0, n_lat, seq, n_batch)
    col0 = pl.multiple_of(k_gate * d + pl.program_id(1) * tn, LANES)
    gate = mod_ref[pl.ds(b, 1), pl.ds(col0, tn)]
    acc = jnp.dot(a_ref[...], w_ref[...], preferred_element_type=F32)
    o_ref[...] = _tile_of(x_ref, xc_ref, row0, n_lat) + gate * acc


def _resid_proj(a, w, x, mod, *, rows, n_lat, seq, n_batch, k_gate, x_ctx=None, tm=512, tn=512):
    d = a.shape[1]
    xs = [x] if x_ctx is None else [x, x_ctx]
    return pl.pallas_call(
        functools.partial(_resid_proj_kernel, split=x_ctx is not None, tm=tm, tn=tn, n_lat=n_lat, seq=seq,
                          n_batch=n_batch, k_gate=k_gate, d=d),
        grid=(rows // tm, d // tn),
        in_specs=[pl.BlockSpec((tm, d), lambda i, j: (i, 0)),
                  pl.BlockSpec((d, tn), lambda i, j: (0, j))]
        + _row_specs(x_ctx, (tm, tn), n_lat, col_of=lambda j: j)
        + [pl.BlockSpec(mod.shape, lambda i, j: (0, 0))],
        out_specs=pl.BlockSpec((tm, tn), lambda i, j: (i, j)),
        out_shape=jax.ShapeDtypeStruct((rows, d), F32),
        compiler_params=_cparams("parallel", "arbitrary"),
        name="resid_proj",
    )(a, w, *xs, mod)


N_TOP = PEER_TOPK + 1
TOP_ROWS = 24
N_CAND = 96


F32_SUBLANES = 8


def _merge_exchange_pairs(n):
    pairs = []
    p = 1
    while p < n:
        k = p
        while k >= 1:
            for j in range(k % p, n - k, 2 * k):
                for i in range(min(k, n - j - k)):
                    if (i + j) // (2 * p) == (i + j + k) // (2 * p):
                        pairs.append((i + j, i + j + k))
            k //= 2
        p *= 2
    return pairs


def _top_values(s, k, out_ref):
    n = s.shape[0] // F32_SUBLANES
    r = [s[i * F32_SUBLANES:(i + 1) * F32_SUBLANES, :] for i in range(n)]
    for a, b in _merge_exchange_pairs(n):
        r[a], r[b] = jnp.maximum(r[a], r[b]), jnp.minimum(r[a], r[b])
    prev = mx = None
    for t in range(k):
        prev = mx
        mx = jnp.max(r[0], axis=0, keepdims=True)
        if out_ref is not None:
            out_ref[t:t + 1, :] = mx
        pop = r[0] == mx
        for i in range(min(n, k - 1 - t)):
            r[i] = jnp.where(pop, r[i + 1] if i + 1 < n else -jnp.inf, r[i])
    return prev, mx


def _candidate_sums(v1_ref, v2_ref, cand_ref):
    cand_ref[0:TOP_ROWS, :] = v1_ref[0:1, :] + v2_ref[...]
    for a in range(1, 8):
        cand_ref[16 + 8 * a:24 + 8 * a, :] = v1_ref[a:a + 1, :] + v2_ref[0:8, :]
    cand_ref[80:96, :] = v1_ref[8:TOP_ROWS, :] + v2_ref[0:1, :]


def _peer_score_kernel(h_ref, wq_ref, k1_ref, k2_ref, th_ref, e1_ref, e2_ref,
                       v1_scr, v2_scr, cand_scr, *, n_i):
    qp = jnp.dot(h_ref[...], wq_ref[...], preferred_element_type=F32)
    pad = jnp.full((TOP_ROWS - N_TOP, v1_scr.shape[1]), -jnp.inf, F32)
    v1_scr[N_TOP:TOP_ROWS, :] = pad
    v2_scr[N_TOP:TOP_ROWS, :] = pad
    for h in range(PEER_HEADS):
        q1 = qp[:, (2 * h) * N_KEYS:(2 * h + 1) * N_KEYS]
        q2 = qp[:, (2 * h + 1) * N_KEYS:(2 * h + 2) * N_KEYS]
        s1 = lax.dot_general(k1_ref[h], q1, _NT, preferred_element_type=F32) * LOG2_E
        s2 = lax.dot_general(k2_ref[h], q2, _NT, preferred_element_type=F32) * LOG2_E
        _top_values(s1, N_TOP, v1_scr)
        _top_values(s2, N_TOP, v2_scr)
        _candidate_sums(v1_scr, v2_scr, cand_scr)
        cand = cand_scr[...]
        c16, c17 = _top_values(cand, N_TOP, None)
        tau = 0.5 * (c16 + c17)
        top1 = v1_scr[0:1, :]
        top2 = v2_scr[0:1, :]
        z = jnp.sum(jnp.where(cand >= c16, jnp.exp2(cand - (top1 + top2)), 0.0), axis=0, keepdims=True)
        theta = jnp.exp2(tau - s1 - top2)
        e1 = jnp.exp2(s1 - (top1 + jnp.log2(z)))
        for g in range(N_KEYS // n_i):
            th_ref[g, h * n_i:(h + 1) * n_i, :] = theta[g * n_i:(g + 1) * n_i, :]
            e1_ref[g, h * n_i:(h + 1) * n_i, :] = e1[g * n_i:(g + 1) * n_i, :]
        e2_ref[h] = jnp.exp2(s2 - top2)


def _peer_scores(h2, wq, k1, k2, *, rows, n_i, tm=256):
    d = h2.shape[1]
    n_grp = N_KEYS // n_i
    grouped = pl.BlockSpec((n_grp, PEER_HEADS * n_i, tm), lambda i: (0, 0, i))
    by_head = pl.BlockSpec((PEER_HEADS, N_KEYS, tm), lambda i: (0, 0, i))
    grouped_shape = jax.ShapeDtypeStruct((n_grp, PEER_HEADS * n_i, rows), F32)
    by_head_shape = jax.ShapeDtypeStruct((PEER_HEADS, N_KEYS, rows), F32)
    return pl.pallas_call(
        functools.partial(_peer_score_kernel, n_i=n_i),
        grid=(rows // tm,),
        in_specs=[pl.BlockSpec((tm, d), lambda i: (i, 0)),
                  pl.BlockSpec(wq.shape, lambda i: (0, 0)),
                  pl.BlockSpec(k1.shape, lambda i: (0, 0, 0)),
                  pl.BlockSpec(k2.shape, lambda i: (0, 0, 0))],
        out_specs=[grouped, grouped, by_head],
        out_shape=[grouped_shape, grouped_shape, by_head_shape],
        scratch_shapes=[pltpu.VMEM((TOP_ROWS, tm), F32), pltpu.VMEM((TOP_ROWS, tm), F32),
                        pltpu.VMEM((N_CAND, tm), F32)],
        compiler_params=_cparams("parallel"),
        name="peer_scores",
    )(h2, wq, k1, k2)


def _gelu(x):
    return (0.5 * x) * (1.0 + lax.erf(x * (0.5 ** 0.5)))


def _peer_mix_kernel(ht_ref, u_ref, v_ref, th_ref, e1_ref, e2_ref, x_ref, mod_ref, *rest,
                     tt, te, n_lat, seq, n_batch, k_gate, final_norm):
    if final_norm:
        fg_ref, o_ref, acc_ref, act_scr, w_scr = rest
    else:
        o_ref, acc_ref, act_scr, w_scr = rest
    j = pl.program_id(1)
    last = pl.num_programs(1) - 1
    d = x_ref.shape[1]
    n_i = te // N_KEYS

    @pl.when(jnp.logical_and(pl.program_id(0) == 0, j == 0))
    def _():
        acc_ref[...] = jnp.zeros_like(acc_ref)

    act_scr[j % 2] = _gelu(jnp.dot(u_ref[...], ht_ref[...], preferred_element_type=F32))

    for ii in range(n_i):
        rs = slice(ii * N_KEYS, (ii + 1) * N_KEYS)
        for tc in range(tt // LANES):
            cs = slice(tc * LANES, (tc + 1) * LANES)
            gate = None
            for h in range(PEER_HEADS):
                row = slice(h * n_i + ii, h * n_i + ii + 1)
                e2 = e2_ref[h, :, cs]
                wgt = jnp.where(e2 >= th_ref[0, row, cs], e2, 0.0) * e1_ref[0, row, cs]
                gate = wgt if gate is None else gate + wgt
            wblk = jnp.where(j >= 1, gate * act_scr[(j + 1) % 2, rs, cs], 0.0)
            w_scr[rs, cs] = wblk.astype(w_scr.dtype)

    acc_ref[...] += lax.dot_general(v_ref[...], w_scr[...], (((0,), (0,)), ((), ())),
                                    preferred_element_type=F32)

    @pl.when(j == last)
    def _():
        b = _mod_row(pl.program_id(0) * tt, n_lat, seq, n_batch)
        xn = x_ref[...] + _mod_vec(mod_ref, b, k_gate, d) * acc_ref[...].T
        if final_norm:
            xn = xn * lax.rsqrt(jnp.mean(xn * xn, axis=-1, keepdims=True) + EPS) * fg_ref[...]
        o_ref[...] = xn
        acc_ref[...] = jnp.zeros_like(acc_ref)


PEER_TE = 512


def _peer_mix(h2t, u, v, th, e1, e2, x, mod, final_g, *, rows, n_lat, seq, n_batch, k_gate,
              tt=512, te=PEER_TE):
    d = x.shape[1]
    n_tiles = u.shape[0] // te
    final_norm = final_g is not None
    n_i = te // N_KEYS

    def cur(j):
        return jnp.minimum(j, n_tiles - 1)

    def prv(j):
        return jnp.maximum(j - 1, 0)

    grouped = pl.BlockSpec((1, PEER_HEADS * n_i, tt), lambda t, j: (prv(j), 0, t))
    by_head = pl.BlockSpec((PEER_HEADS, N_KEYS, tt), lambda t, j: (0, 0, t))
    in_specs = [pl.BlockSpec((d, tt), lambda t, j: (0, t)),
                pl.BlockSpec((te, d), lambda t, j: (cur(j), 0)),
                pl.BlockSpec((te, d), lambda t, j: (prv(j), 0)),
                grouped, grouped, by_head,
                pl.BlockSpec((tt, d), lambda t, j: (t, 0)),
                pl.BlockSpec(mod.shape, lambda t, j: (0, 0))]
    args = [h2t, u, v, th, e1, e2, x, mod]
    if final_norm:
        in_specs.append(pl.BlockSpec((1, d), lambda t, j: (0, 0)))
        args.append(final_g.reshape(1, d))
    return pl.pallas_call(
        functools.partial(_peer_mix_kernel, tt=tt, te=te, n_lat=n_lat, seq=seq, n_batch=n_batch,
                          k_gate=k_gate, final_norm=final_norm),
        grid=(rows // tt, n_tiles + 1),
        in_specs=in_specs,
        out_specs=pl.BlockSpec((tt, d), lambda t, j: (t, 0)),
        out_shape=jax.ShapeDtypeStruct((rows, d), F32),
        scratch_shapes=[pltpu.VMEM((d, tt), F32), pltpu.VMEM((2, te, tt), F32), pltpu.VMEM((te, tt), BF16)],
        compiler_params=_cparams("arbitrary", "arbitrary"),
        name="peer_mix",
    )(*args)


def _rope_tables(seq, pad_rows):
    half = HEAD_DIM // 2
    t = jnp.arange(seq)
    row = (t // GRID_W).astype(F32)
    col = (t % GRID_W).astype(F32)
    inv = ROPE_THETA ** (-jnp.arange(0, half, 2, dtype=F32) / half)
    ang_r = row[:, None] * inv[None, :]
    ang_c = col[:, None] * inv[None, :]
    cos = jnp.concatenate([jnp.cos(ang_r), jnp.cos(ang_r), jnp.cos(ang_c), jnp.cos(ang_c)], axis=-1)
    sin = jnp.concatenate([-jnp.sin(ang_r), jnp.sin(ang_r), -jnp.sin(ang_c), jnp.sin(ang_c)], axis=-1)
    cos = jnp.concatenate([cos, jnp.ones((pad_rows, HEAD_DIM), F32)], axis=0)
    sin = jnp.concatenate([sin, jnp.zeros((pad_rows, HEAD_DIM), F32)], axis=0)
    return cos, sin


def kernel(x, c, ctx, c_ctx, ada_w, ada_b, norm1_g, norm2_g, w_in, b_gate, q_norm_g, k_norm_g,
           conv_w, w_attn_out, w_conv_out, w_o, peer_wq, peer_k1, peer_k2, peer_u, peer_v, final_g):
    n_batch, seq, d = x.shape
    ctx_len = ctx.shape[1]
    depth = ada_w.shape[0]
    n_lat = n_batch * seq
    n_all = n_lat + n_batch * ctx_len
    assert seq & (seq - 1) == 0 and ctx_len & (ctx_len - 1) == 0
    tm = 512

    xa = x.reshape(n_lat, d)
    xc = ctx.reshape(n_batch * ctx_len, d)
    pad = (-(n_batch + 1)) % 8
    cvec = jnp.concatenate([c, c_ctx[None, :], jnp.zeros((pad, d), F32)], axis=0)
    mod_all = _adaln(cvec, ada_w, ada_b)
    cos_t, sin_t = _rope_tables(seq, tm)
    geom = dict(n_lat=n_lat, seq=seq, n_batch=n_batch)

    for l in range(depth):
        last = l == depth - 1
        rows = n_lat if last else n_all
        mod = mod_all[l]
        w_in_l = w_in[l].astype(BF16)

        h = _norm_mod(xa, norm1_g[l], mod, rows=n_all, k_shift=0, k_scale=1, x_ctx=xc, **geom)
        q, k, v, z, gb, gates = _in_proj(h, w_in_l, q_norm_g[l], k_norm_g[l], b_gate[l], cos_t, sin_t,
                                         rows_q=rows, rows_kv=n_all, n_lat=n_lat, seq=seq, d=d, tm=tm)
        attn = _attention(q, k, v, rows_q=rows, n_lat=n_lat, seq=seq, ctx_len=ctx_len)
        merged = _merge(attn, z, gb, gates, conv_w[l], w_attn_out[l].astype(BF16),
                        w_conv_out[l].astype(BF16), rows=rows, n_lat=n_lat, seq=seq, ctx_len=ctx_len)
        xa = _resid_proj(merged, w_o[l].astype(BF16), xa, mod, rows=rows, k_gate=2, x_ctx=xc, **geom)
        xc = None

        h2, h2t = _norm_mod(xa, norm2_g[l], mod, rows=rows, k_shift=3, k_scale=4, with_transpose=True, **geom)
        th, e1, e2 = _peer_scores(h2, peer_wq[l].astype(BF16), peer_k1[l], peer_k2[l], rows=rows,
                                  n_i=PEER_TE // N_KEYS)
        xa = _peer_mix(h2t, peer_u[l].astype(BF16), peer_v[l].astype(BF16), th, e1, e2, xa, mod,
                       final_g if last else None, rows=rows, k_gate=5, **geom)
    return xa.reshape(n_batch, seq, d)
```

```python
import functools

import jax
import jax.numpy as jnp
from jax import lax
from jax.experimental import pallas as pl
from jax.experimental.pallas import tpu as pltpu

F32 = jnp.float32
BF16 = jnp.bfloat16

HEAD_DIM = 128
KV_GROUPS = 4
GRID_W = 64
ROPE_THETA = 10000.0
PEER_HEADS = 8
N_KEYS = 128
PEER_TOPK = 16
N_MOD = 6
EPS = 1e-6
LOG2_E = 1.4426950408889634
LANES = 128
BF16_SUBLANES = 16
VMEM_LIMIT_BYTES = 56 * 1024 * 1024

_NT = (((1,), (1,)), ((), ()))


def _cparams(*sem, flags=None):
    return pltpu.CompilerParams(dimension_semantics=sem, vmem_limit_bytes=VMEM_LIMIT_BYTES, flags=flags)


def _mod_row(row0, n_lat, seq, n_batch):
    return jnp.where(row0 < n_lat, row0 // seq, n_batch)


def _mod_vec(mod_ref, b, k, d):
    return mod_ref[pl.ds(b, 1), k * d:(k + 1) * d]


def _adaln_kernel(c_ref, w_ref, b_ref, o_ref):
    cv = c_ref[...]
    s = cv * jax.nn.sigmoid(cv)
    o_ref[0] = jnp.dot(s, w_ref[0], preferred_element_type=F32) + b_ref[0]


def _adaln(cvec, ada_w, ada_b, tn=1024):
    depth, d, n = ada_w.shape
    rows = cvec.shape[0]
    return pl.pallas_call(
        _adaln_kernel,
        grid=(depth, n // tn),
        in_specs=[pl.BlockSpec((rows, d), lambda l, j: (0, 0)),
                  pl.BlockSpec((1, d, tn), lambda l, j: (l, 0, j)),
                  pl.BlockSpec((1, 1, tn), lambda l, j: (l, 0, j))],
        out_specs=pl.BlockSpec((1, rows, tn), lambda l, j: (l, 0, j)),
        out_shape=jax.ShapeDtypeStruct((depth, rows, n), F32),
        compiler_params=_cparams("arbitrary", "arbitrary"),
        name="adaln",
    )(cvec, ada_w, ada_b.reshape(depth, 1, n))


def _tile_of(lat_ref, ctx_ref, row0, n_lat):
    if ctx_ref is None:
        return lat_ref[...]
    return jnp.where(row0 < n_lat, lat_ref[...], ctx_ref[...])


def _row_specs(x_ctx, block, n_lat, col_of=None):
    n_lat_tiles = n_lat // block[0]
    col = (lambda *j: 0) if col_of is None else col_of
    if x_ctx is None:
        return [pl.BlockSpec(block, lambda i, *j: (i, col(*j)))]
    return [pl.BlockSpec(block, lambda i, *j: (jnp.minimum(i, n_lat_tiles - 1), col(*j))),
            pl.BlockSpec(block, lambda i, *j: (jnp.maximum(i - n_lat_tiles, 0), col(*j)))]


def _norm_mod_kernel(*refs, split, tm, n_lat, seq, n_batch, k_shift, k_scale):
    x_ref, xc_ref = (refs[0], refs[1]) if split else (refs[0], None)
    g_ref, mod_ref, o_ref, *maybe_ot_ref = refs[2 if split else 1:]
    d = x_ref.shape[1]
    row0 = pl.program_id(0) * tm
    b = _mod_row(row0, n_lat, seq, n_batch)
    xf = _tile_of(x_ref, xc_ref, row0, n_lat)
    y = xf * lax.rsqrt(jnp.mean(xf * xf, axis=-1, keepdims=True) + EPS)
    h = y * g_ref[...]
    shift = _mod_vec(mod_ref, b, k_shift, d)
    scale = _mod_vec(mod_ref, b, k_scale, d)
    h = h * (1.0 + scale) + shift
    o_ref[...] = h.astype(o_ref.dtype)
    for ot_ref in maybe_ot_ref:
        ot_ref[...] = h.T.astype(ot_ref.dtype)


def _norm_mod(x, g, mod, *, rows, n_lat, seq, n_batch, k_shift, k_scale, x_ctx=None,
              with_transpose=False, tm=512):
    d = x.shape[1]
    kern = functools.partial(_norm_mod_kernel, split=x_ctx is not None, tm=tm, n_lat=n_lat, seq=seq,
                             n_batch=n_batch, k_shift=k_shift, k_scale=k_scale)
    xs = [x] if x_ctx is None else [x, x_ctx]
    out_specs = [pl.BlockSpec((tm, d), lambda i: (i, 0))]
    out_shape = [jax.ShapeDtypeStruct((rows, d), BF16)]
    if with_transpose:
        out_specs.append(pl.BlockSpec((d, tm), lambda i: (0, i)))
        out_shape.append(jax.ShapeDtypeStruct((d, rows), BF16))
    out = pl.pallas_call(
        kern,
        grid=(rows // tm,),
        in_specs=_row_specs(x_ctx, (tm, d), n_lat) + [pl.BlockSpec((1, d), lambda i: (0, 0)),
                                                         pl.BlockSpec(mod.shape, lambda i: (0, 0))],
        out_specs=out_specs,
        out_shape=out_shape,
        compiler_params=_cparams("parallel"),
        name="norm_mod",
    )(*xs, g.reshape(1, d), mod)
    return out if with_transpose else out[0]


def _heads_norm_rope(acc, ones_bd, perm_bd, g, cos, sin, out_scale):
    n_heads = acc.shape[1] // HEAD_DIM
    ss = jnp.dot((acc * acc).astype(BF16), ones_bd, preferred_element_type=F32)
    y = acc * lax.rsqrt(ss * (1.0 / HEAD_DIM) + EPS) * g
    partner = jnp.dot(y.astype(BF16), perm_bd, preferred_element_type=F32)
    r = y * jnp.tile(cos, (1, n_heads)) + partner * jnp.tile(sin, (1, n_heads))
    if out_scale != 1.0:
        r = r * out_scale
    return r


def _q_kernel(h_ref, w_ref, ones_ref, perm_ref, g_ref, cos_ref, sin_ref, o_ref, *, out_scale):
    acc = jnp.dot(h_ref[...], w_ref[...], preferred_element_type=F32)
    r = _heads_norm_rope(acc, ones_ref[...], perm_ref[...], g_ref[...], cos_ref[...], sin_ref[...], out_scale)
    o_ref[...] = r.astype(o_ref.dtype)


def _kv_kernel(h_ref, wk_ref, wv_ref, ones_ref, perm_ref, g_ref, cos_ref, sin_ref, k_ref, v_ref):
    h = h_ref[...]
    acc = jnp.dot(h, wk_ref[...], preferred_element_type=F32)
    r = _heads_norm_rope(acc, ones_ref[...], perm_ref[...], g_ref[...], cos_ref[...], sin_ref[...], 1.0)
    k_ref[...] = r.astype(k_ref.dtype)
    vacc = jnp.dot(h, wv_ref[...], preferred_element_type=F32).astype(v_ref.dtype)
    ones = jnp.ones((vacc.shape[0], HEAD_DIM), v_ref.dtype)
    for g in range(vacc.shape[1] // HEAD_DIM):
        v_ref[:, 2 * g * HEAD_DIM:(2 * g + 1) * HEAD_DIM] = vacc[:, g * HEAD_DIM:(g + 1) * HEAD_DIM]
        v_ref[:, (2 * g + 1) * HEAD_DIM:(2 * g + 2) * HEAD_DIM] = ones


def _conv_in_kernel(h_ref, wu_ref, wb_ref, wc_ref, z_ref, gb_ref):
    h = h_ref[...]
    u = jnp.dot(h, wu_ref[...], preferred_element_type=F32)
    gc = jnp.dot(h, wc_ref[...], preferred_element_type=F32)
    z_ref[...] = (gc * u).astype(z_ref.dtype)
    gb_ref[...] = jnp.dot(h, wb_ref[...], preferred_element_type=F32).astype(gb_ref.dtype)


def _gate_kernel(h_ref, w_ref, b_ref, o_ref):
    acc = jnp.dot(h_ref[...], w_ref[...], preferred_element_type=F32)
    o_ref[...] = (0.5 + 0.5 * jnp.tanh(0.5 * (acc + b_ref[...]))).astype(o_ref.dtype)


def _rope_block(i, tm, n_lat, seq):
    return jnp.where(i * tm < n_lat, (i * tm % seq) // tm, seq // tm)


def _in_proj(h, w_in, q_g, k_g, b_gate, cos_t, sin_t, *, rows_q, rows_kv, n_lat, seq, d,
             tm=512, tn=512):
    kd = h.shape[1]
    kvw = d // KV_GROUPS
    q_end, k_end, v_end = d, d + kvw, d + 2 * kvw
    u_end, b_end, c_end = v_end + d, v_end + 2 * d, v_end + 3 * d
    hspec = pl.BlockSpec((tm, kd), lambda i, j: (i, 0))
    hd_spec = pl.BlockSpec((1, tn), lambda i, j: (0, 0))
    bd_spec = pl.BlockSpec((tn, tn), lambda i, j: (0, 0))
    heads = tn // HEAD_DIM
    lane = jnp.arange(HEAD_DIM)
    partner = jnp.where(lane % (HEAD_DIM // 2) < HEAD_DIM // 4, lane + HEAD_DIM // 4, lane - HEAD_DIM // 4)
    perm = (lane[:, None] == partner[None, :]).astype(BF16)
    ones_bd = jnp.kron(jnp.eye(heads, dtype=BF16), jnp.ones((HEAD_DIM, HEAD_DIM), BF16))
    perm_bd = jnp.kron(jnp.eye(heads, dtype=BF16), perm)
    tab_spec = pl.BlockSpec((tm, HEAD_DIM), lambda i, j: (_rope_block(i, tm, n_lat, seq), 0))

    def wspec(col0):
        return pl.BlockSpec((kd, tn), lambda i, j: (0, col0 // tn + j))

    def ospec():
        return pl.BlockSpec((tm, tn), lambda i, j: (i, j))

    q = pl.pallas_call(
        functools.partial(_q_kernel, out_scale=HEAD_DIM ** -0.5 * LOG2_E),
        grid=(rows_q // tm, d // tn),
        in_specs=[hspec, wspec(0), bd_spec, bd_spec, hd_spec, tab_spec, tab_spec],
        out_specs=ospec(),
        out_shape=jax.ShapeDtypeStruct((rows_q, d), BF16),
        compiler_params=_cparams("parallel", "arbitrary"),
        name="proj_q",
    )(h, w_in, ones_bd, perm_bd, jnp.tile(q_g, heads).reshape(1, tn), cos_t, sin_t)

    assert kvw == tn
    k, v = pl.pallas_call(
        _kv_kernel,
        grid=(rows_kv // tm, 1),
        in_specs=[hspec, wspec(q_end), wspec(k_end), bd_spec, bd_spec, hd_spec, tab_spec, tab_spec],
        out_specs=[ospec(), pl.BlockSpec((tm, 2 * tn), lambda i, j: (i, j))],
        out_shape=[jax.ShapeDtypeStruct((rows_kv, kvw), BF16), jax.ShapeDtypeStruct((rows_kv, 2 * kvw), BF16)],
        compiler_params=_cparams("parallel", "arbitrary"),
        name="proj_kv",
    )(h, w_in, w_in, ones_bd, perm_bd, jnp.tile(k_g, heads).reshape(1, tn), cos_t, sin_t)

    z, gb = pl.pallas_call(
        _conv_in_kernel,
        grid=(rows_q // tm, d // tn),
        in_specs=[hspec, wspec(v_end), wspec(u_end), wspec(b_end)],
        out_specs=[ospec(), ospec()],
        out_shape=[jax.ShapeDtypeStruct((rows_q, d), BF16)] * 2,
        compiler_params=_cparams("parallel", "arbitrary"),
        name="proj_conv",
    )(h, w_in, w_in, w_in)

    gates = pl.pallas_call(
        _gate_kernel,
        grid=(rows_q // tm, 2 * d // tn),
        in_specs=[hspec, wspec(c_end), pl.BlockSpec((1, tn), lambda i, j: (0, j))],
        out_specs=ospec(),
        out_shape=jax.ShapeDtypeStruct((rows_q, 2 * d), BF16),
        compiler_params=_cparams("parallel", "arbitrary"),
        name="proj_gates",
    )(h, w_in, b_gate.reshape(1, 2 * d))
    return q, k, v, z, gb, gates


ATTN_KEY_CHUNK = 256


def _attn_kernel(q_ref, kl_ref, vl_ref, kc_ref, vc_ref, o_ref, *, tq, n_lat):
    is_lat = pl.program_id(1) * tq < n_lat
    q = jnp.concatenate([q_ref[:, hh * HEAD_DIM:(hh + 1) * HEAD_DIM] for hh in range(KV_GROUPS)], axis=0)

    def step(carry, k, v):
        s = lax.dot_general(q, k, _NT, preferred_element_type=F32)
        m_new = jnp.max(s, axis=-1, keepdims=True)
        pv = lambda m: jnp.dot(jnp.exp2(s - m).astype(v.dtype), v, preferred_element_type=F32)
        if carry is None:
            return m_new, pv(m_new)
        m_old, acc = carry
        m_new = jnp.maximum(m_old, m_new)
        return m_new, acc * jnp.exp2(m_old - m_new) + pv(m_new)

    def run(with_lat):
        carry = step(None, kc_ref[...], vc_ref[...])
        if with_lat:
            for c in range(kl_ref.shape[0] // ATTN_KEY_CHUNK):
                rows = slice(c * ATTN_KEY_CHUNK, (c + 1) * ATTN_KEY_CHUNK)
                carry = step(carry, kl_ref[rows, :], vl_ref[rows, :])
        acc = carry[1]
        out = (acc[:, :HEAD_DIM] / acc[:, HEAD_DIM:]).astype(o_ref.dtype)
        for hh in range(KV_GROUPS):
            o_ref[:, hh * HEAD_DIM:(hh + 1) * HEAD_DIM] = out[hh * tq:(hh + 1) * tq, :]

    @pl.when(is_lat)
    def _():
        run(True)

    @pl.when(jnp.logical_not(is_lat))
    def _():
        run(False)


def _attention(q, k, v, *, rows_q, n_lat, seq, ctx_len, tq=256):
    assert ctx_len % tq == 0 or tq % ctx_len == 0
    assert tq <= ctx_len
    gw = KV_GROUPS * HEAD_DIM
    n_kv = k.shape[1] // HEAD_DIM

    def batch(i):
        return jnp.where(i * tq < n_lat, i * tq // seq, (i * tq - n_lat) // ctx_len)

    qspec = pl.BlockSpec((tq, gw), lambda g, i: (i, g))

    def lat_spec(width):
        return pl.BlockSpec((seq, width), lambda g, i: (batch(i), g))

    def ctx_spec(width):
        return pl.BlockSpec((ctx_len, width), lambda g, i: (n_lat // ctx_len + batch(i), g))

    return pl.pallas_call(
        functools.partial(_attn_kernel, tq=tq, n_lat=n_lat),
        grid=(n_kv, rows_q // tq),
        in_specs=[qspec, lat_spec(HEAD_DIM), lat_spec(2 * HEAD_DIM), ctx_spec(HEAD_DIM), ctx_spec(2 * HEAD_DIM)],
        out_specs=qspec,
        out_shape=jax.ShapeDtypeStruct((rows_q, q.shape[1]), BF16),
        compiler_params=_cparams("parallel", "arbitrary"),
        name="attention",
    )(q, k, v, k, v)


def _merge_kernel(attn_ref, z_ref, zp_ref, zn_ref, gb_ref, ga_ref, gc_ref, cw_ref,
                  wao_ref, wco_ref, o_ref, y_scr, *, tm, n_lat, seq, ctx_len):
    @pl.when(pl.program_id(1) == 0)
    def _():
        z = z_ref[...].astype(F32)
        loc = lax.broadcasted_iota(jnp.int32, (tm, 1), 0)
        row = pl.program_id(0) * tm + loc
        seqlen = jnp.where(row < n_lat, seq, ctx_len)
        pos = row & (seqlen - 1)
        halo_prev = zp_ref[...].astype(F32)[BF16_SUBLANES - 1:BF16_SUBLANES, :]
        halo_next = zn_ref[...].astype(F32)[0:1, :]
        z_prev = jnp.where(loc == 0, halo_prev, pltpu.roll(z, 1, 0))
        z_prev = jnp.where(pos == 0, 0.0, z_prev)
        z_next = jnp.where(loc == tm - 1, halo_next, pltpu.roll(z, tm - 1, 0))
        z_next = jnp.where(pos == seqlen - 1, 0.0, z_next)
        cw = cw_ref[...]
        conv = cw[0:1, :] * z_prev + cw[1:2, :] * z + cw[2:3, :] * z_next
        y_scr[...] = (gb_ref[...].astype(F32) * conv).astype(y_scr.dtype)

    acc_a = jnp.dot(attn_ref[...], wao_ref[...], preferred_element_type=F32)
    acc_c = jnp.dot(y_scr[...], wco_ref[...], preferred_element_type=F32)
    o_ref[...] = (ga_ref[...].astype(F32) * acc_a + gc_ref[...].astype(F32) * acc_c).astype(o_ref.dtype)


def _merge(attn, z, gb, gates, conv_w, w_ao, w_co, *, rows, n_lat, seq, ctx_len, tm=512, tn=512):
    d = attn.shape[1]
    hb = BF16_SUBLANES
    n_hblk = z.shape[0] // hb
    row_spec = pl.BlockSpec((tm, d), lambda i, j: (i, 0))
    prev_spec = pl.BlockSpec((hb, d), lambda i, j: (jnp.maximum(i * (tm // hb) - 1, 0), 0))
    next_spec = pl.BlockSpec((hb, d), lambda i, j: (jnp.minimum((i + 1) * (tm // hb), n_hblk - 1), 0))
    wspec = pl.BlockSpec((d, tn), lambda i, j: (0, j))
    return pl.pallas_call(
        functools.partial(_merge_kernel, tm=tm, n_lat=n_lat, seq=seq, ctx_len=ctx_len),
        grid=(rows // tm, d // tn),
        in_specs=[row_spec, row_spec, prev_spec, next_spec, row_spec,
                  pl.BlockSpec((tm, tn), lambda i, j: (i, j)),
                  pl.BlockSpec((tm, tn), lambda i, j: (i, d // tn + j)),
                  pl.BlockSpec(conv_w.shape, lambda i, j: (0, 0)),
                  wspec, wspec],
        out_specs=pl.BlockSpec((tm, tn), lambda i, j: (i, j)),
        out_shape=jax.ShapeDtypeStruct((rows, d), BF16),
        scratch_shapes=[pltpu.VMEM((tm, d), BF16)],
        compiler_params=_cparams("parallel", "arbitrary"),
        name="merge",
    )(attn, z, z, z, gb, gates, gates, conv_w, w_ao, w_co)


def _resid_proj_kernel(a_ref, w_ref, *refs, split, tm, tn, n_lat, seq, n_batch, k_gate, d):
    x_ref, xc_ref = (refs[0], refs[1]) if split else (refs[0], None)
    mod_ref, o_ref = refs[2 if split else 1:]
    row0 = pl.program_id(0) * tm
    b = _mod_row(row0, n_lat, seq, n_batch)
    col0 = pl.multiple_of(k_gate * d + pl.program_id(1) * tn, LANES)
    gate = mod_ref[pl.ds(b, 1), pl.ds(col0, tn)]
    acc = jnp.dot(a_ref[...], w_ref[...], preferred_element_type=F32)
    o_ref[...] = _tile_of(x_ref, xc_ref, row0, n_lat) + gate * acc


def _resid_proj(a, w, x, mod, *, rows, n_lat, seq, n_batch, k_gate, x_ctx=None, tm=1024, tn=512):
    d = a.shape[1]
    xs = [x] if x_ctx is None else [x, x_ctx]
    return pl.pallas_call(
        functools.partial(_resid_proj_kernel, split=x_ctx is not None, tm=tm, tn=tn, n_lat=n_lat, seq=seq,
                          n_batch=n_batch, k_gate=k_gate, d=d),
        grid=(rows // tm, d // tn),
        in_specs=[pl.BlockSpec((tm, d), lambda i, j: (i, 0)),
                  pl.BlockSpec((d, tn), lambda i, j: (0, j))]
        + _row_specs(x_ctx, (tm, tn), n_lat, col_of=lambda j: j)
        + [pl.BlockSpec(mod.shape, lambda i, j: (0, 0))],
        out_specs=pl.BlockSpec((tm, tn), lambda i, j: (i, j)),
        out_shape=jax.ShapeDtypeStruct((rows, d), F32),
        compiler_params=_cparams("parallel", "arbitrary"),
        name="resid_proj",
    )(a, w, *xs, mod)


N_TOP = PEER_TOPK + 1
TOP_ROWS = 24
N_CAND = 96


F32_SUBLANES = 8


def _merge_exchange_pairs(n):
    pairs = []
    p = 1
    while p < n:
        k = p
        while k >= 1:
            for j in range(k % p, n - k, 2 * k):
                for i in range(min(k, n - j - k)):
                    if (i + j) // (2 * p) == (i + j + k) // (2 * p):
                        pairs.append((i + j, i + j + k))
            k //= 2
        p *= 2
    return pairs


def _top_values(s, k, out_ref):
    n = s.shape[0] // F32_SUBLANES
    r = [s[i * F32_SUBLANES:(i + 1) * F32_SUBLANES, :] for i in range(n)]
    for a, b in _merge_exchange_pairs(n):
        r[a], r[b] = jnp.maximum(r[a], r[b]), jnp.minimum(r[a], r[b])
    prev = mx = None
    for t in range(k):
        prev = mx
        mx = jnp.max(r[0], axis=0, keepdims=True)
        if out_ref is not None:
            out_ref[t:t + 1, :] = mx
        pop = r[0] == mx
        for i in range(min(n, k - 1 - t)):
            r[i] = jnp.where(pop, r[i + 1] if i + 1 < n else -jnp.inf, r[i])
    return prev, mx


def _candidate_sums(v1_ref, v2_ref, cand_ref):
    cand_ref[0:TOP_ROWS, :] = v1_ref[0:1, :] + v2_ref[...]
    for a in range(1, 8):
        cand_ref[16 + 8 * a:24 + 8 * a, :] = v1_ref[a:a + 1, :] + v2_ref[0:8, :]
    cand_ref[80:96, :] = v1_ref[8:TOP_ROWS, :] + v2_ref[0:1, :]


def _peer_score_kernel(h_ref, wq_ref, k1_ref, k2_ref, th_ref, e1_ref, e2_ref,
                       v1_scr, v2_scr, cand_scr, *, n_i):
    qp = jnp.dot(h_ref[...], wq_ref[...], preferred_element_type=F32)
    pad = jnp.full((TOP_ROWS - N_TOP, v1_scr.shape[1]), -jnp.inf, F32)
    v1_scr[N_TOP:TOP_ROWS, :] = pad
    v2_scr[N_TOP:TOP_ROWS, :] = pad
    for h in range(PEER_HEADS):
        q1 = qp[:, (2 * h) * N_KEYS:(2 * h + 1) * N_KEYS]
        q2 = qp[:, (2 * h + 1) * N_KEYS:(2 * h + 2) * N_KEYS]
        s1 = lax.dot_general(k1_ref[h], q1, _NT, preferred_element_type=F32) * LOG2_E
        s2 = lax.dot_general(k2_ref[h], q2, _NT, preferred_element_type=F32) * LOG2_E
        _top_values(s1, N_TOP, v1_scr)
        _top_values(s2, N_TOP, v2_scr)
        _candidate_sums(v1_scr, v2_scr, cand_scr)
        cand = cand_scr[...]
        c16, c17 = _top_values(cand, N_TOP, None)
        tau = 0.5 * (c16 + c17)
        top1 = v1_scr[0:1, :]
        top2 = v2_scr[0:1, :]
        z = jnp.sum(jnp.where(cand >= c16, jnp.exp2(cand - (top1 + top2)), 0.0), axis=0, keepdims=True)
        theta = jnp.exp2(tau - s1 - top2)
        e1 = jnp.exp2(s1 - (top1 + jnp.log2(z)))
        for g in range(N_KEYS // n_i):
            th_ref[g, h * n_i:(h + 1) * n_i, :] = theta[g * n_i:(g + 1) * n_i, :]
            e1_ref[g, h * n_i:(h + 1) * n_i, :] = e1[g * n_i:(g + 1) * n_i, :]
        e2_ref[h] = jnp.exp2(s2 - top2)


def _peer_scores(h2, wq, k1, k2, *, rows, n_i, tm=256):
    d = h2.shape[1]
    n_grp = N_KEYS // n_i
    grouped = pl.BlockSpec((n_grp, PEER_HEADS * n_i, tm), lambda i: (0, 0, i))
    by_head = pl.BlockSpec((PEER_HEADS, N_KEYS, tm), lambda i: (0, 0, i))
    grouped_shape = jax.ShapeDtypeStruct((n_grp, PEER_HEADS * n_i, rows), F32)
    by_head_shape = jax.ShapeDtypeStruct((PEER_HEADS, N_KEYS, rows), F32)
    return pl.pallas_call(
        functools.partial(_peer_score_kernel, n_i=n_i),
        grid=(rows // tm,),
        in_specs=[pl.BlockSpec((tm, d), lambda i: (i, 0)),
                  pl.BlockSpec(wq.shape, lambda i: (0, 0)),
                  pl.BlockSpec(k1.shape, lambda i: (0, 0, 0)),
                  pl.BlockSpec(k2.shape, lambda i: (0, 0, 0))],
        out_specs=[grouped, grouped, by_head],
        out_shape=[grouped_shape, grouped_shape, by_head_shape],
        scratch_shapes=[pltpu.VMEM((TOP_ROWS, tm), F32), pltpu.VMEM((TOP_ROWS, tm), F32),
                        pltpu.VMEM((N_CAND, tm), F32)],
        compiler_params=_cparams("parallel"),
        name="peer_scores",
    )(h2, wq, k1, k2)


def _gelu(x):
    return (0.5 * x) * (1.0 + lax.erf(x * (0.5 ** 0.5)))


def _peer_mix_kernel(ht_ref, u_ref, vt_ref, th_ref, e1_ref, e2_ref, x_ref, mod_ref, *rest,
                     tt, te, n_lat, seq, n_batch, k_gate, final_norm):
    if final_norm:
        fg_ref, o_ref, acc_ref, act_scr, w_scr = rest
    else:
        o_ref, acc_ref, act_scr, w_scr = rest
    j = pl.program_id(1)
    last = pl.num_programs(1) - 1
    d = x_ref.shape[1]
    n_i = te // N_KEYS

    @pl.when(jnp.logical_and(pl.program_id(0) == 0, j == 0))
    def _():
        acc_ref[...] = jnp.zeros_like(acc_ref)

    act_scr[j % 2] = _gelu(jnp.dot(u_ref[...], ht_ref[...], preferred_element_type=F32))

    for ii in range(n_i):
        rs = slice(ii * N_KEYS, (ii + 1) * N_KEYS)
        for tc in range(tt // LANES):
            cs = slice(tc * LANES, (tc + 1) * LANES)
            gate = None
            for h in range(PEER_HEADS):
                row = slice(h * n_i + ii, h * n_i + ii + 1)
                e2 = e2_ref[h, :, cs]
                wgt = jnp.where(e2 >= th_ref[0, row, cs], e2, 0.0) * e1_ref[0, row, cs]
                gate = wgt if gate is None else gate + wgt
            wblk = jnp.where(j >= 1, gate * act_scr[(j + 1) % 2, rs, cs], 0.0)
            w_scr[rs, cs] = wblk.astype(w_scr.dtype)

    acc_ref[...] += jnp.dot(vt_ref[...], w_scr[...], preferred_element_type=F32)

    @pl.when(j == last)
    def _():
        b = _mod_row(pl.program_id(0) * tt, n_lat, seq, n_batch)
        xn = x_ref[...] + _mod_vec(mod_ref, b, k_gate, d) * acc_ref[...].T
        if final_norm:
            xn = xn * lax.rsqrt(jnp.mean(xn * xn, axis=-1, keepdims=True) + EPS) * fg_ref[...]
        o_ref[...] = xn
        acc_ref[...] = jnp.zeros_like(acc_ref)


PEER_TE = 512


def _peer_mix(h2t, u, vt, th, e1, e2, x, mod, final_g, *, rows, n_lat, seq, n_batch, k_gate,
              tt=512, te=PEER_TE):
    d = x.shape[1]
    n_tiles = u.shape[0] // te
    final_norm = final_g is not None
    n_i = te // N_KEYS

    def cur(j):
        return jnp.minimum(j, n_tiles - 1)

    def prv(j):
        return jnp.maximum(j - 1, 0)

    grouped = pl.BlockSpec((1, PEER_HEADS * n_i, tt), lambda t, j: (prv(j), 0, t))
    by_head = pl.BlockSpec((PEER_HEADS, N_KEYS, tt), lambda t, j: (0, 0, t))
    in_specs = [pl.BlockSpec((d, tt), lambda t, j: (0, t)),
                pl.BlockSpec((te, d), lambda t, j: (cur(j), 0)),
                pl.BlockSpec((d, te), lambda t, j: (0, prv(j))),
                grouped, grouped, by_head,
                pl.BlockSpec((tt, d), lambda t, j: (t, 0)),
                pl.BlockSpec(mod.shape, lambda t, j: (0, 0))]
    args = [h2t, u, vt, th, e1, e2, x, mod]
    if final_norm:
        in_specs.append(pl.BlockSpec((1, d), lambda t, j: (0, 0)))
        args.append(final_g.reshape(1, d))
    return pl.pallas_call(
        functools.partial(_peer_mix_kernel, tt=tt, te=te, n_lat=n_lat, seq=seq, n_batch=n_batch,
                          k_gate=k_gate, final_norm=final_norm),
        grid=(rows // tt, n_tiles + 1),
        in_specs=in_specs,
        out_specs=pl.BlockSpec((tt, d), lambda t, j: (t, 0)),
        out_shape=jax.ShapeDtypeStruct((rows, d), F32),
        scratch_shapes=[pltpu.VMEM((d, tt), F32), pltpu.VMEM((2, te, tt), F32), pltpu.VMEM((te, tt), BF16)],
        compiler_params=_cparams("arbitrary", "arbitrary"),
        name="peer_mix",
    )(*args)


def _rope_tables(seq, pad_rows):
    half = HEAD_DIM // 2
    t = jnp.arange(seq)
    row = (t // GRID_W).astype(F32)
    col = (t % GRID_W).astype(F32)
    inv = ROPE_THETA ** (-jnp.arange(0, half, 2, dtype=F32) / half)
    ang_r = row[:, None] * inv[None, :]
    ang_c = col[:, None] * inv[None, :]
    cos = jnp.concatenate([jnp.cos(ang_r), jnp.cos(ang_r), jnp.cos(ang_c), jnp.cos(ang_c)], axis=-1)
    sin = jnp.concatenate([-jnp.sin(ang_r), jnp.sin(ang_r), -jnp.sin(ang_c), jnp.sin(ang_c)], axis=-1)
    cos = jnp.concatenate([cos, jnp.ones((pad_rows, HEAD_DIM), F32)], axis=0)
    sin = jnp.concatenate([sin, jnp.zeros((pad_rows, HEAD_DIM), F32)], axis=0)
    return cos, sin


def kernel(x, c, ctx, c_ctx, ada_w, ada_b, norm1_g, norm2_g, w_in, b_gate, q_norm_g, k_norm_g,
           conv_w, w_attn_out, w_conv_out, w_o, peer_wq, peer_k1, peer_k2, peer_u, peer_v, final_g):
    n_batch, seq, d = x.shape
    ctx_len = ctx.shape[1]
    depth = ada_w.shape[0]
    n_lat = n_batch * seq
    n_all = n_lat + n_batch * ctx_len
    assert seq & (seq - 1) == 0 and ctx_len & (ctx_len - 1) == 0
    tm = 1024

    xa = x.reshape(n_lat, d)
    xc = ctx.reshape(n_batch * ctx_len, d)
    pad = (-(n_batch + 1)) % 8
    cvec = jnp.concatenate([c, c_ctx[None, :], jnp.zeros((pad, d), F32)], axis=0)
    mod_all = _adaln(cvec, ada_w, ada_b)
    cos_t, sin_t = _rope_tables(seq, tm)
    geom = dict(n_lat=n_lat, seq=seq, n_batch=n_batch)

    for l in range(depth):
        last = l == depth - 1
        rows = n_lat if last else n_all
        mod = mod_all[l]
        w_in_l = w_in[l].astype(BF16)

        h = _norm_mod(xa, norm1_g[l], mod, rows=n_all, k_shift=0, k_scale=1, x_ctx=xc, **geom)
        q, k, v, z, gb, gates = _in_proj(h, w_in_l, q_norm_g[l], k_norm_g[l], b_gate[l], cos_t, sin_t,
                                         rows_q=rows, rows_kv=n_all, n_lat=n_lat, seq=seq, d=d, tm=tm)
        attn = _attention(q, k, v, rows_q=rows, n_lat=n_lat, seq=seq, ctx_len=ctx_len)
        merged = _merge(attn, z, gb, gates, conv_w[l], w_attn_out[l].astype(BF16),
                        w_conv_out[l].astype(BF16), rows=rows, n_lat=n_lat, seq=seq, ctx_len=ctx_len)
        xa = _resid_proj(merged, w_o[l].astype(BF16), xa, mod, rows=rows, k_gate=2, x_ctx=xc, **geom)
        xc = None

        h2, h2t = _norm_mod(xa, norm2_g[l], mod, rows=rows, k_shift=3, k_scale=4, with_transpose=True, **geom)
        th, e1, e2 = _peer_scores(h2, peer_wq[l].astype(BF16), peer_k1[l], peer_k2[l], rows=rows,
                                  n_i=PEER_TE // N_KEYS)
        xa = _peer_mix(h2t, peer_u[l].astype(BF16), peer_v[l].T.astype(BF16), th, e1, e2, xa, mod,
                       final_g if last else None, rows=rows, k_gate=5, **geom)
    return xa.reshape(n_batch, seq, d)
```

```python
import functools

import jax
import jax.numpy as jnp
from jax import lax
from jax.experimental import pallas as pl
from jax.experimental.pallas import tpu as pltpu

F32 = jnp.float32
BF16 = jnp.bfloat16

HEAD_DIM = 128
KV_GROUPS = 4
GRID_W = 64
ROPE_THETA = 10000.0
PEER_HEADS = 8
N_KEYS = 128
PEER_TOPK = 16
N_MOD = 6
EPS = 1e-6
LOG2_E = 1.4426950408889634
LANES = 128
BF16_SUBLANES = 16
VMEM_LIMIT_BYTES = 56 * 1024 * 1024

_NT = (((1,), (1,)), ((), ()))


def _cparams(*sem, flags=None):
    return pltpu.CompilerParams(dimension_semantics=sem, vmem_limit_bytes=VMEM_LIMIT_BYTES, flags=flags)


def _mod_row(row0, n_lat, seq, n_batch):
    return jnp.where(row0 < n_lat, row0 // seq, n_batch)


def _mod_vec(mod_ref, b, k, d):
    return mod_ref[pl.ds(b, 1), k * d:(k + 1) * d]


def _adaln_kernel(c_ref, w_ref, b_ref, o_ref):
    cv = c_ref[...]
    s = cv * jax.nn.sigmoid(cv)
    o_ref[0] = jnp.dot(s, w_ref[0], preferred_element_type=F32) + b_ref[0]


def _adaln(cvec, ada_w, ada_b, tn=1024):
    depth, d, n = ada_w.shape
    rows = cvec.shape[0]
    return pl.pallas_call(
        _adaln_kernel,
        grid=(depth, n // tn),
        in_specs=[pl.BlockSpec((rows, d), lambda l, j: (0, 0)),
                  pl.BlockSpec((1, d, tn), lambda l, j: (l, 0, j)),
                  pl.BlockSpec((1, 1, tn), lambda l, j: (l, 0, j))],
        out_specs=pl.BlockSpec((1, rows, tn), lambda l, j: (l, 0, j)),
        out_shape=jax.ShapeDtypeStruct((depth, rows, n), F32),
        compiler_params=_cparams("arbitrary", "arbitrary"),
        name="adaln",
    )(cvec, ada_w, ada_b.reshape(depth, 1, n))


def _tile_of(lat_ref, ctx_ref, row0, n_lat):
    if ctx_ref is None:
        return lat_ref[...]
    return jnp.where(row0 < n_lat, lat_ref[...], ctx_ref[...])


def _row_specs(x_ctx, block, n_lat, col_of=None):
    n_lat_tiles = n_lat // block[0]
    col = (lambda *j: 0) if col_of is None else col_of
    if x_ctx is None:
        return [pl.BlockSpec(block, lambda i, *j: (i, col(*j)))]
    return [pl.BlockSpec(block, lambda i, *j: (jnp.minimum(i, n_lat_tiles - 1), col(*j))),
            pl.BlockSpec(block, lambda i, *j: (jnp.maximum(i - n_lat_tiles, 0), col(*j)))]


def _norm_mod_kernel(*refs, split, tm, n_lat, seq, n_batch, k_shift, k_scale):
    x_ref, xc_ref = (refs[0], refs[1]) if split else (refs[0], None)
    g_ref, mod_ref, o_ref, *maybe_ot_ref = refs[2 if split else 1:]
    d = x_ref.shape[1]
    row0 = pl.program_id(0) * tm
    b = _mod_row(row0, n_lat, seq, n_batch)
    xf = _tile_of(x_ref, xc_ref, row0, n_lat)
    y = xf * lax.rsqrt(jnp.mean(xf * xf, axis=-1, keepdims=True) + EPS)
    h = y * g_ref[...]
    shift = _mod_vec(mod_ref, b, k_shift, d)
    scale = _mod_vec(mod_ref, b, k_scale, d)
    h = h * (1.0 + scale) + shift
    o_ref[...] = h.astype(o_ref.dtype)
    for ot_ref in maybe_ot_ref:
        ot_ref[...] = h.T.astype(ot_ref.dtype)


def _norm_mod(x, g, mod, *, rows, n_lat, seq, n_batch, k_shift, k_scale, x_ctx=None,
              with_transpose=False, tm=512):
    d = x.shape[1]
    kern = functools.partial(_norm_mod_kernel, split=x_ctx is not None, tm=tm, n_lat=n_lat, seq=seq,
                             n_batch=n_batch, k_shift=k_shift, k_scale=k_scale)
    xs = [x] if x_ctx is None else [x, x_ctx]
    out_specs = [pl.BlockSpec((tm, d), lambda i: (i, 0))]
    out_shape = [jax.ShapeDtypeStruct((rows, d), BF16)]
    if with_transpose:
        out_specs.append(pl.BlockSpec((d, tm), lambda i: (0, i)))
        out_shape.append(jax.ShapeDtypeStruct((d, rows), BF16))
    out = pl.pallas_call(
        kern,
        grid=(rows // tm,),
        in_specs=_row_specs(x_ctx, (tm, d), n_lat) + [pl.BlockSpec((1, d), lambda i: (0, 0)),
                                                         pl.BlockSpec(mod.shape, lambda i: (0, 0))],
        out_specs=out_specs,
        out_shape=out_shape,
        compiler_params=_cparams("parallel"),
        name="norm_mod",
    )(*xs, g.reshape(1, d), mod)
    return out if with_transpose else out[0]


def _heads_norm_rope(acc, ones_bd, perm_bd, g, cos, sin, out_scale):
    n_heads = acc.shape[1] // HEAD_DIM
    ss = jnp.dot((acc * acc).astype(BF16), ones_bd, preferred_element_type=F32)
    y = acc * lax.rsqrt(ss * (1.0 / HEAD_DIM) + EPS) * g
    partner = jnp.dot(y.astype(BF16), perm_bd, preferred_element_type=F32)
    r = y * jnp.tile(cos, (1, n_heads)) + partner * jnp.tile(sin, (1, n_heads))
    if out_scale != 1.0:
        r = r * out_scale
    return r


def _q_kernel(h_ref, w_ref, ones_ref, perm_ref, g_ref, cos_ref, sin_ref, o_ref, *, out_scale):
    acc = jnp.dot(h_ref[...], w_ref[...], preferred_element_type=F32)
    r = _heads_norm_rope(acc, ones_ref[...], perm_ref[...], g_ref[...], cos_ref[...], sin_ref[...], out_scale)
    o_ref[...] = r.astype(o_ref.dtype)


def _kv_kernel(h_ref, wk_ref, wv_ref, ones_ref, perm_ref, g_ref, cos_ref, sin_ref, k_ref, v_ref):
    h = h_ref[...]
    acc = jnp.dot(h, wk_ref[...], preferred_element_type=F32)
    r = _heads_norm_rope(acc, ones_ref[...], perm_ref[...], g_ref[...], cos_ref[...], sin_ref[...], 1.0)
    k_ref[...] = r.astype(k_ref.dtype)
    vacc = jnp.dot(h, wv_ref[...], preferred_element_type=F32).astype(v_ref.dtype)
    ones = jnp.ones((vacc.shape[0], HEAD_DIM), v_ref.dtype)
    for g in range(vacc.shape[1] // HEAD_DIM):
        v_ref[:, 2 * g * HEAD_DIM:(2 * g + 1) * HEAD_DIM] = vacc[:, g * HEAD_DIM:(g + 1) * HEAD_DIM]
        v_ref[:, (2 * g + 1) * HEAD_DIM:(2 * g + 2) * HEAD_DIM] = ones


def _conv_in_kernel(h_ref, wu_ref, wb_ref, wc_ref, z_ref, gb_ref):
    h = h_ref[...]
    u = jnp.dot(h, wu_ref[...], preferred_element_type=F32)
    gc = jnp.dot(h, wc_ref[...], preferred_element_type=F32)
    z_ref[...] = (gc * u).astype(z_ref.dtype)
    gb_ref[...] = jnp.dot(h, wb_ref[...], preferred_element_type=F32).astype(gb_ref.dtype)


def _gate_kernel(h_ref, w_ref, b_ref, o_ref):
    acc = jnp.dot(h_ref[...], w_ref[...], preferred_element_type=F32)
    o_ref[...] = (0.5 + 0.5 * jnp.tanh(0.5 * (acc + b_ref[...]))).astype(o_ref.dtype)


def _rope_block(i, tm, n_lat, seq):
    return jnp.where(i * tm < n_lat, (i * tm % seq) // tm, seq // tm)


def _in_proj(h, w_in, q_g, k_g, b_gate, cos_t, sin_t, *, rows_q, rows_kv, n_lat, seq, d,
             tm=512, tn=512):
    kd = h.shape[1]
    kvw = d // KV_GROUPS
    q_end, k_end, v_end = d, d + kvw, d + 2 * kvw
    u_end, b_end, c_end = v_end + d, v_end + 2 * d, v_end + 3 * d
    hspec = pl.BlockSpec((tm, kd), lambda i, j: (i, 0))
    hd_spec = pl.BlockSpec((1, tn), lambda i, j: (0, 0))
    bd_spec = pl.BlockSpec((tn, tn), lambda i, j: (0, 0))
    heads = tn // HEAD_DIM
    lane = jnp.arange(HEAD_DIM)
    partner = jnp.where(lane % (HEAD_DIM // 2) < HEAD_DIM // 4, lane + HEAD_DIM // 4, lane - HEAD_DIM // 4)
    perm = (lane[:, None] == partner[None, :]).astype(BF16)
    ones_bd = jnp.kron(jnp.eye(heads, dtype=BF16), jnp.ones((HEAD_DIM, HEAD_DIM), BF16))
    perm_bd = jnp.kron(jnp.eye(heads, dtype=BF16), perm)
    tab_spec = pl.BlockSpec((tm, HEAD_DIM), lambda i, j: (_rope_block(i, tm, n_lat, seq), 0))

    def wspec(col0):
        return pl.BlockSpec((kd, tn), lambda i, j: (0, col0 // tn + j))

    def ospec():
        return pl.BlockSpec((tm, tn), lambda i, j: (i, j))

    q = pl.pallas_call(
        functools.partial(_q_kernel, out_scale=HEAD_DIM ** -0.5 * LOG2_E),
        grid=(rows_q // tm, d // tn),
        in_specs=[hspec, wspec(0), bd_spec, bd_spec, hd_spec, tab_spec, tab_spec],
        out_specs=ospec(),
        out_shape=jax.ShapeDtypeStruct((rows_q, d), BF16),
        compiler_params=_cparams("parallel", "arbitrary"),
        name="proj_q",
    )(h, w_in, ones_bd, perm_bd, jnp.tile(q_g, heads).reshape(1, tn), cos_t, sin_t)

    assert kvw == tn
    k, v = pl.pallas_call(
        _kv_kernel,
        grid=(rows_kv // tm, 1),
        in_specs=[hspec, wspec(q_end), wspec(k_end), bd_spec, bd_spec, hd_spec, tab_spec, tab_spec],
        out_specs=[ospec(), pl.BlockSpec((tm, 2 * tn), lambda i, j: (i, j))],
        out_shape=[jax.ShapeDtypeStruct((rows_kv, kvw), BF16), jax.ShapeDtypeStruct((rows_kv, 2 * kvw), BF16)],
        compiler_params=_cparams("parallel", "arbitrary"),
        name="proj_kv",
    )(h, w_in, w_in, ones_bd, perm_bd, jnp.tile(k_g, heads).reshape(1, tn), cos_t, sin_t)

    z, gb = pl.pallas_call(
        _conv_in_kernel,
        grid=(rows_q // tm, d // tn),
        in_specs=[hspec, wspec(v_end), wspec(u_end), wspec(b_end)],
        out_specs=[ospec(), ospec()],
        out_shape=[jax.ShapeDtypeStruct((rows_q, d), BF16)] * 2,
        compiler_params=_cparams("parallel", "arbitrary"),
        name="proj_conv",
    )(h, w_in, w_in, w_in)

    gates = pl.pallas_call(
        _gate_kernel,
        grid=(rows_q // tm, 2 * d // tn),
        in_specs=[hspec, wspec(c_end), pl.BlockSpec((1, tn), lambda i, j: (0, j))],
        out_specs=ospec(),
        out_shape=jax.ShapeDtypeStruct((rows_q, 2 * d), BF16),
        compiler_params=_cparams("parallel", "arbitrary"),
        name="proj_gates",
    )(h, w_in, b_gate.reshape(1, 2 * d))
    return q, k, v, z, gb, gates


ATTN_KEY_CHUNK = 256


def _attn_kernel(q_ref, kl_ref, vl_ref, kc_ref, vc_ref, o_ref, *, tq, n_lat):
    is_lat = pl.program_id(1) * tq < n_lat
    q = jnp.concatenate([q_ref[:, hh * HEAD_DIM:(hh + 1) * HEAD_DIM] for hh in range(KV_GROUPS)], axis=0)

    def step(carry, k, v):
        s = lax.dot_general(q, k, _NT, preferred_element_type=F32)
        m_new = jnp.max(s, axis=-1, keepdims=True)
        pv = lambda m: jnp.dot(jnp.exp2(s - m).astype(v.dtype), v, preferred_element_type=F32)
        if carry is None:
            return m_new, pv(m_new)
        m_old, acc = carry
        m_new = jnp.maximum(m_old, m_new)
        return m_new, acc * jnp.exp2(m_old - m_new) + pv(m_new)

    def run(with_lat):
        carry = step(None, kc_ref[...], vc_ref[...])
        if with_lat:
            for c in range(kl_ref.shape[0] // ATTN_KEY_CHUNK):
                rows = slice(c * ATTN_KEY_CHUNK, (c + 1) * ATTN_KEY_CHUNK)
                carry = step(carry, kl_ref[rows, :], vl_ref[rows, :])
        acc = carry[1]
        out = (acc[:, :HEAD_DIM] / acc[:, HEAD_DIM:]).astype(o_ref.dtype)
        for hh in range(KV_GROUPS):
            o_ref[:, hh * HEAD_DIM:(hh + 1) * HEAD_DIM] = out[hh * tq:(hh + 1) * tq, :]

    @pl.when(is_lat)
    def _():
        run(True)

    @pl.when(jnp.logical_not(is_lat))
    def _():
        run(False)


def _attention(q, k, v, *, rows_q, n_lat, seq, ctx_len, tq=256):
    assert ctx_len % tq == 0 or tq % ctx_len == 0
    assert tq <= ctx_len
    gw = KV_GROUPS * HEAD_DIM
    n_kv = k.shape[1] // HEAD_DIM

    def batch(i):
        return jnp.where(i * tq < n_lat, i * tq // seq, (i * tq - n_lat) // ctx_len)

    qspec = pl.BlockSpec((tq, gw), lambda g, i: (i, g))

    def lat_spec(width):
        return pl.BlockSpec((seq, width), lambda g, i: (batch(i), g))

    def ctx_spec(width):
        return pl.BlockSpec((ctx_len, width), lambda g, i: (n_lat // ctx_len + batch(i), g))

    return pl.pallas_call(
        functools.partial(_attn_kernel, tq=tq, n_lat=n_lat),
        grid=(n_kv, rows_q // tq),
        in_specs=[qspec, lat_spec(HEAD_DIM), lat_spec(2 * HEAD_DIM), ctx_spec(HEAD_DIM), ctx_spec(2 * HEAD_DIM)],
        out_specs=qspec,
        out_shape=jax.ShapeDtypeStruct((rows_q, q.shape[1]), BF16),
        compiler_params=_cparams("parallel", "arbitrary"),
        name="attention",
    )(q, k, v, k, v)


def _merge_kernel(attn_ref, z_ref, zp_ref, zn_ref, gb_ref, ga_ref, gc_ref, cw_ref,
                  wao_ref, wco_ref, o_ref, y_scr, *, tm, n_lat, seq, ctx_len):
    @pl.when(pl.program_id(1) == 0)
    def _():
        z = z_ref[...].astype(F32)
        loc = lax.broadcasted_iota(jnp.int32, (tm, 1), 0)
        row = pl.program_id(0) * tm + loc
        seqlen = jnp.where(row < n_lat, seq, ctx_len)
        pos = row & (seqlen - 1)
        halo_prev = zp_ref[...].astype(F32)[BF16_SUBLANES - 1:BF16_SUBLANES, :]
        halo_next = zn_ref[...].astype(F32)[0:1, :]
        z_prev = jnp.where(loc == 0, halo_prev, pltpu.roll(z, 1, 0))
        z_prev = jnp.where(pos == 0, 0.0, z_prev)
        z_next = jnp.where(loc == tm - 1, halo_next, pltpu.roll(z, tm - 1, 0))
        z_next = jnp.where(pos == seqlen - 1, 0.0, z_next)
        cw = cw_ref[...]
        conv = cw[0:1, :] * z_prev + cw[1:2, :] * z + cw[2:3, :] * z_next
        y_scr[...] = (gb_ref[...].astype(F32) * conv).astype(y_scr.dtype)

    acc_a = jnp.dot(attn_ref[...], wao_ref[...], preferred_element_type=F32)
    acc_c = jnp.dot(y_scr[...], wco_ref[...], preferred_element_type=F32)
    o_ref[...] = (ga_ref[...].astype(F32) * acc_a + gc_ref[...].astype(F32) * acc_c).astype(o_ref.dtype)


def _merge(attn, z, gb, gates, conv_w, w_ao, w_co, *, rows, n_lat, seq, ctx_len, tm=512, tn=1024):
    d = attn.shape[1]
    hb = BF16_SUBLANES
    n_hblk = z.shape[0] // hb
    row_spec = pl.BlockSpec((tm, d), lambda i, j: (i, 0))
    prev_spec = pl.BlockSpec((hb, d), lambda i, j: (jnp.maximum(i * (tm // hb) - 1, 0), 0))
    next_spec = pl.BlockSpec((hb, d), lambda i, j: (jnp.minimum((i + 1) * (tm // hb), n_hblk - 1), 0))
    wspec = pl.BlockSpec((d, tn), lambda i, j: (0, j))
    return pl.pallas_call(
        functools.partial(_merge_kernel, tm=tm, n_lat=n_lat, seq=seq, ctx_len=ctx_len),
        grid=(rows // tm, d // tn),
        in_specs=[row_spec, row_spec, prev_spec, next_spec, row_spec,
                  pl.BlockSpec((tm, tn), lambda i, j: (i, j)),
                  pl.BlockSpec((tm, tn), lambda i, j: (i, d // tn + j)),
                  pl.BlockSpec(conv_w.shape, lambda i, j: (0, 0)),
                  wspec, wspec],
        out_specs=pl.BlockSpec((tm, tn), lambda i, j: (i, j)),
        out_shape=jax.ShapeDtypeStruct((rows, d), BF16),
        scratch_shapes=[pltpu.VMEM((tm, d), BF16)],
        compiler_params=_cparams("parallel", "arbitrary"),
        name="merge",
    )(attn, z, z, z, gb, gates, gates, conv_w, w_ao, w_co)


def _resid_proj_kernel(a_ref, w_ref, *refs, split, tm, tn, n_lat, seq, n_batch, k_gate, d):
    x_ref, xc_ref = (refs[0], refs[1]) if split else (refs[0], None)
    mod_ref, o_ref = refs[2 if split else 1:]
    row0 = pl.program_id(0) * tm
    b = _mod_row(row0, n_lat, seq, n_batch)
    col0 = pl.multiple_of(k_gate * d + pl.program_id(1) * tn, LANES)
    gate = mod_ref[pl.ds(b, 1), pl.ds(col0, tn)]
    acc = jnp.dot(a_ref[...], w_ref[...], preferred_element_type=F32)
    o_ref[...] = _tile_of(x_ref, xc_ref, row0, n_lat) + gate * acc


def _resid_proj(a, w, x, mod, *, rows, n_lat, seq, n_batch, k_gate, x_ctx=None, tm=1024, tn=1024):
    d = a.shape[1]
    xs = [x] if x_ctx is None else [x, x_ctx]
    return pl.pallas_call(
        functools.partial(_resid_proj_kernel, split=x_ctx is not None, tm=tm, tn=tn, n_lat=n_lat, seq=seq,
                          n_batch=n_batch, k_gate=k_gate, d=d),
        grid=(rows // tm, d // tn),
        in_specs=[pl.BlockSpec((tm, d), lambda i, j: (i, 0)),
                  pl.BlockSpec((d, tn), lambda i, j: (0, j))]
        + _row_specs(x_ctx, (tm, tn), n_lat, col_of=lambda j: j)
        + [pl.BlockSpec(mod.shape, lambda i, j: (0, 0))],
        out_specs=pl.BlockSpec((tm, tn), lambda i, j: (i, j)),
        out_shape=jax.ShapeDtypeStruct((rows, d), F32),
        compiler_params=_cparams("parallel", "arbitrary"),
        name="resid_proj",
    )(a, w, *xs, mod)


N_TOP = PEER_TOPK + 1
TOP_ROWS = 24
N_CAND = 96


F32_SUBLANES = 8


def _merge_exchange_pairs(n):
    pairs = []
    p = 1
    while p < n:
        k = p
        while k >= 1:
            for j in range(k % p, n - k, 2 * k):
                for i in range(min(k, n - j - k)):
                    if (i + j) // (2 * p) == (i + j + k) // (2 * p):
                        pairs.append((i + j, i + j + k))
            k //= 2
        p *= 2
    return pairs


def _top_values(s, k, out_ref):
    n = s.shape[0] // F32_SUBLANES
    r = [s[i * F32_SUBLANES:(i + 1) * F32_SUBLANES, :] for i in range(n)]
    for a, b in _merge_exchange_pairs(n):
        r[a], r[b] = jnp.maximum(r[a], r[b]), jnp.minimum(r[a], r[b])
    prev = mx = None
    for t in range(k):
        prev = mx
        mx = jnp.max(r[0], axis=0, keepdims=True)
        if out_ref is not None:
            out_ref[t:t + 1, :] = mx
        pop = r[0] == mx
        for i in range(min(n, k - 1 - t)):
            r[i] = jnp.where(pop, r[i + 1] if i + 1 < n else -jnp.inf, r[i])
    return prev, mx


def _candidate_sums(v1_ref, v2_ref, cand_ref):
    cand_ref[0:TOP_ROWS, :] = v1_ref[0:1, :] + v2_ref[...]
    for a in range(1, 8):
        cand_ref[16 + 8 * a:24 + 8 * a, :] = v1_ref[a:a + 1, :] + v2_ref[0:8, :]
    cand_ref[80:96, :] = v1_ref[8:TOP_ROWS, :] + v2_ref[0:1, :]


def _peer_score_kernel(h_ref, wq_ref, k1_ref, k2_ref, th_ref, e1_ref, e2_ref,
                       v1_scr, v2_scr, cand_scr, *, n_i):
    qp = jnp.dot(h_ref[...], wq_ref[...], preferred_element_type=F32)
    pad = jnp.full((TOP_ROWS - N_TOP, v1_scr.shape[1]), -jnp.inf, F32)
    v1_scr[N_TOP:TOP_ROWS, :] = pad
    v2_scr[N_TOP:TOP_ROWS, :] = pad
    for h in range(PEER_HEADS):
        q1 = qp[:, (2 * h) * N_KEYS:(2 * h + 1) * N_KEYS]
        q2 = qp[:, (2 * h + 1) * N_KEYS:(2 * h + 2) * N_KEYS]
        s1 = lax.dot_general(k1_ref[h], q1, _NT, preferred_element_type=F32) * LOG2_E
        s2 = lax.dot_general(k2_ref[h], q2, _NT, preferred_element_type=F32) * LOG2_E
        _top_values(s1, N_TOP, v1_scr)
        _top_values(s2, N_TOP, v2_scr)
        _candidate_sums(v1_scr, v2_scr, cand_scr)
        cand = cand_scr[...]
        c16, c17 = _top_values(cand, N_TOP, None)
        tau = 0.5 * (c16 + c17)
        top1 = v1_scr[0:1, :]
        top2 = v2_scr[0:1, :]
        z = jnp.sum(jnp.where(cand >= c16, jnp.exp2(cand - (top1 + top2)), 0.0), axis=0, keepdims=True)
        theta = jnp.exp2(tau - s1 - top2)
        e1 = jnp.exp2(s1 - (top1 + jnp.log2(z)))
        for g in range(N_KEYS // n_i):
            th_ref[g, h * n_i:(h + 1) * n_i, :] = theta[g * n_i:(g + 1) * n_i, :]
            e1_ref[g, h * n_i:(h + 1) * n_i, :] = e1[g * n_i:(g + 1) * n_i, :]
        e2_ref[h] = jnp.exp2(s2 - top2)


def _peer_scores(h2, wq, k1, k2, *, rows, n_i, tm=256):
    d = h2.shape[1]
    n_grp = N_KEYS // n_i
    grouped = pl.BlockSpec((n_grp, PEER_HEADS * n_i, tm), lambda i: (0, 0, i))
    by_head = pl.BlockSpec((PEER_HEADS, N_KEYS, tm), lambda i: (0, 0, i))
    grouped_shape = jax.ShapeDtypeStruct((n_grp, PEER_HEADS * n_i, rows), F32)
    by_head_shape = jax.ShapeDtypeStruct((PEER_HEADS, N_KEYS, rows), F32)
    return pl.pallas_call(
        functools.partial(_peer_score_kernel, n_i=n_i),
        grid=(rows // tm,),
        in_specs=[pl.BlockSpec((tm, d), lambda i: (i, 0)),
                  pl.BlockSpec(wq.shape, lambda i: (0, 0)),
                  pl.BlockSpec(k1.shape, lambda i: (0, 0, 0)),
                  pl.BlockSpec(k2.shape, lambda i: (0, 0, 0))],
        out_specs=[grouped, grouped, by_head],
        out_shape=[grouped_shape, grouped_shape, by_head_shape],
        scratch_shapes=[pltpu.VMEM((TOP_ROWS, tm), F32), pltpu.VMEM((TOP_ROWS, tm), F32),
                        pltpu.VMEM((N_CAND, tm), F32)],
        compiler_params=_cparams("parallel"),
        name="peer_scores",
    )(h2, wq, k1, k2)


def _gelu(x):
    return (0.5 * x) * (1.0 + lax.erf(x * (0.5 ** 0.5)))


def _peer_mix_kernel(ht_ref, u_ref, vt_ref, th_ref, e1_ref, e2_ref, x_ref, mod_ref, *rest,
                     tt, te, n_lat, seq, n_batch, k_gate, final_norm):
    if final_norm:
        fg_ref, o_ref, acc_ref, act_scr, g_scr, w_scr = rest
    else:
        o_ref, acc_ref, act_scr, g_scr, w_scr = rest
    j = pl.program_id(1)
    last = pl.num_programs(1) - 1
    d = x_ref.shape[1]
    n_i = te // N_KEYS

    @pl.when(jnp.logical_and(pl.program_id(0) == 0, j == 0))
    def _():
        acc_ref[...] = jnp.zeros_like(acc_ref)

    act_scr[j % 2] = _gelu(jnp.dot(u_ref[...], ht_ref[...], preferred_element_type=F32))

    for ii in range(n_i):
        rs = slice(ii * N_KEYS, (ii + 1) * N_KEYS)
        for tc in range(tt // LANES):
            cs = slice(tc * LANES, (tc + 1) * LANES)
            gate = None
            for h in range(PEER_HEADS):
                row = slice(h * n_i + ii, h * n_i + ii + 1)
                e2 = e2_ref[h, :, cs]
                wgt = jnp.where(e2 >= th_ref[0, row, cs], e2, 0.0) * e1_ref[0, row, cs]
                gate = wgt if gate is None else gate + wgt
            g_scr[rs, cs] = gate

    prev = act_scr[(j + 1) % 2]
    w_scr[...] = jnp.where(j >= 1, g_scr[...] * prev, 0.0).astype(w_scr.dtype)
    acc_ref[...] += jnp.dot(vt_ref[...], w_scr[...], preferred_element_type=F32)

    @pl.when(j == last)
    def _():
        b = _mod_row(pl.program_id(0) * tt, n_lat, seq, n_batch)
        xn = x_ref[...] + _mod_vec(mod_ref, b, k_gate, d) * acc_ref[...].T
        if final_norm:
            xn = xn * lax.rsqrt(jnp.mean(xn * xn, axis=-1, keepdims=True) + EPS) * fg_ref[...]
        o_ref[...] = xn
        acc_ref[...] = jnp.zeros_like(acc_ref)


PEER_TE = 512


def _peer_mix(h2t, u, vt, th, e1, e2, x, mod, final_g, *, rows, n_lat, seq, n_batch, k_gate,
              tt=512, te=PEER_TE):
    d = x.shape[1]
    n_tiles = u.shape[0] // te
    final_norm = final_g is not None
    n_i = te // N_KEYS

    def cur(j):
        return jnp.minimum(j, n_tiles - 1)

    def prv(j):
        return jnp.maximum(j - 1, 0)

    grouped = pl.BlockSpec((1, PEER_HEADS * n_i, tt), lambda t, j: (prv(j), 0, t))
    by_head = pl.BlockSpec((PEER_HEADS, N_KEYS, tt), lambda t, j: (0, 0, t))
    in_specs = [pl.BlockSpec((d, tt), lambda t, j: (0, t)),
                pl.BlockSpec((te, d), lambda t, j: (cur(j), 0)),
                pl.BlockSpec((d, te), lambda t, j: (0, prv(j))),
                grouped, grouped, by_head,
                pl.BlockSpec((tt, d), lambda t, j: (t, 0)),
                pl.BlockSpec(mod.shape, lambda t, j: (0, 0))]
    args = [h2t, u, vt, th, e1, e2, x, mod]
    if final_norm:
        in_specs.append(pl.BlockSpec((1, d), lambda t, j: (0, 0)))
        args.append(final_g.reshape(1, d))
    return pl.pallas_call(
        functools.partial(_peer_mix_kernel, tt=tt, te=te, n_lat=n_lat, seq=seq, n_batch=n_batch,
                          k_gate=k_gate, final_norm=final_norm),
        grid=(rows // tt, n_tiles + 1),
        in_specs=in_specs,
        out_specs=pl.BlockSpec((tt, d), lambda t, j: (t, 0)),
        out_shape=jax.ShapeDtypeStruct((rows, d), F32),
        scratch_shapes=[pltpu.VMEM((d, tt), F32), pltpu.VMEM((2, te, tt), F32), pltpu.VMEM((te, tt), F32),
                        pltpu.VMEM((te, tt), BF16)],
        compiler_params=_cparams("arbitrary", "arbitrary"),
        name="peer_mix",
    )(*args)


def _rope_tables(seq, pad_rows):
    half = HEAD_DIM // 2
    t = jnp.arange(seq)
    row = (t // GRID_W).astype(F32)
    col = (t % GRID_W).astype(F32)
    inv = ROPE_THETA ** (-jnp.arange(0, half, 2, dtype=F32) / half)
    ang_r = row[:, None] * inv[None, :]
    ang_c = col[:, None] * inv[None, :]
    cos = jnp.concatenate([jnp.cos(ang_r), jnp.cos(ang_r), jnp.cos(ang_c), jnp.cos(ang_c)], axis=-1)
    sin = jnp.concatenate([-jnp.sin(ang_r), jnp.sin(ang_r), -jnp.sin(ang_c), jnp.sin(ang_c)], axis=-1)
    cos = jnp.concatenate([cos, jnp.ones((pad_rows, HEAD_DIM), F32)], axis=0)
    sin = jnp.concatenate([sin, jnp.zeros((pad_rows, HEAD_DIM), F32)], axis=0)
    return cos, sin


def kernel(x, c, ctx, c_ctx, ada_w, ada_b, norm1_g, norm2_g, w_in, b_gate, q_norm_g, k_norm_g,
           conv_w, w_attn_out, w_conv_out, w_o, peer_wq, peer_k1, peer_k2, peer_u, peer_v, final_g):
    n_batch, seq, d = x.shape
    ctx_len = ctx.shape[1]
    depth = ada_w.shape[0]
    n_lat = n_batch * seq
    n_all = n_lat + n_batch * ctx_len
    assert seq & (seq - 1) == 0 and ctx_len & (ctx_len - 1) == 0
    tm = 1024
    assert seq % tm == 0 and (n_batch * ctx_len) % tm == 0 and seq % GRID_W == 0

    xa = x.reshape(n_lat, d)
    xc = ctx.reshape(n_batch * ctx_len, d)
    pad = (-(n_batch + 1)) % 8
    cvec = jnp.concatenate([c, c_ctx[None, :], jnp.zeros((pad, d), F32)], axis=0)
    mod_all = _adaln(cvec, ada_w, ada_b)
    cos_t, sin_t = _rope_tables(seq, tm)
    geom = dict(n_lat=n_lat, seq=seq, n_batch=n_batch)

    for l in range(depth):
        last = l == depth - 1
        rows = n_lat if last else n_all
        mod = mod_all[l]
        w_in_l = w_in[l].astype(BF16)

        h = _norm_mod(xa, norm1_g[l], mod, rows=n_all, k_shift=0, k_scale=1, x_ctx=xc, **geom)
        q, k, v, z, gb, gates = _in_proj(h, w_in_l, q_norm_g[l], k_norm_g[l], b_gate[l], cos_t, sin_t,
                                         rows_q=rows, rows_kv=n_all, n_lat=n_lat, seq=seq, d=d, tm=tm)
        attn = _attention(q, k, v, rows_q=rows, n_lat=n_lat, seq=seq, ctx_len=ctx_len)
        merged = _merge(attn, z, gb, gates, conv_w[l], w_attn_out[l].astype(BF16),
                        w_conv_out[l].astype(BF16), rows=rows, n_lat=n_lat, seq=seq, ctx_len=ctx_len)
        xa = _resid_proj(merged, w_o[l].astype(BF16), xa, mod, rows=rows, k_gate=2, x_ctx=xc, **geom)
        xc = None

        h2, h2t = _norm_mod(xa, norm2_g[l], mod, rows=rows, k_shift=3, k_scale=4, with_transpose=True, **geom)
        th, e1, e2 = _peer_scores(h2, peer_wq[l].astype(BF16), peer_k1[l], peer_k2[l], rows=rows,
                                  n_i=PEER_TE // N_KEYS)
        xa = _peer_mix(h2t, peer_u[l].astype(BF16), peer_v[l].T.astype(BF16), th, e1, e2, xa, mod,
                       final_g if last else None, rows=rows, k_gate=5, **geom)
    return xa.reshape(n_batch, seq, d)
```

```python
import functools

import jax
import jax.numpy as jnp
from jax import lax
from jax.experimental import pallas as pl
from jax.experimental.pallas import tpu as pltpu

F32 = jnp.float32
BF16 = jnp.bfloat16

HEAD_DIM = 128
KV_GROUPS = 4
GRID_W = 64
ROPE_THETA = 10000.0
PEER_HEADS = 8
N_KEYS = 128
PEER_TOPK = 16
N_MOD = 6
EPS = 1e-6
LOG2_E = 1.4426950408889634
LANES = 128
BF16_SUBLANES = 16
VMEM_LIMIT_BYTES = 56 * 1024 * 1024

_NT = (((1,), (1,)), ((), ()))


def _cparams(*sem, flags=None):
    return pltpu.CompilerParams(dimension_semantics=sem, vmem_limit_bytes=VMEM_LIMIT_BYTES, flags=flags)


def _mod_row(row0, n_lat, seq, n_batch):
    return jnp.where(row0 < n_lat, row0 // seq, n_batch)


def _mod_vec(mod_ref, b, k, d):
    return mod_ref[pl.ds(b, 1), k * d:(k + 1) * d]


def _adaln_kernel(c_ref, w_ref, b_ref, o_ref):
    cv = c_ref[...]
    s = cv * jax.nn.sigmoid(cv)
    o_ref[0] = jnp.dot(s, w_ref[0], preferred_element_type=F32) + b_ref[0]


def _adaln(cvec, ada_w, ada_b, tn=1024):
    depth, d, n = ada_w.shape
    rows = cvec.shape[0]
    return pl.pallas_call(
        _adaln_kernel,
        grid=(depth, n // tn),
        in_specs=[pl.BlockSpec((rows, d), lambda l, j: (0, 0)),
                  pl.BlockSpec((1, d, tn), lambda l, j: (l, 0, j)),
                  pl.BlockSpec((1, 1, tn), lambda l, j: (l, 0, j))],
        out_specs=pl.BlockSpec((1, rows, tn), lambda l, j: (l, 0, j)),
        out_shape=jax.ShapeDtypeStruct((depth, rows, n), F32),
        compiler_params=_cparams("arbitrary", "arbitrary"),
        name="adaln",
    )(cvec, ada_w, ada_b.reshape(depth, 1, n))


def _tile_of(lat_ref, ctx_ref, row0, n_lat):
    if ctx_ref is None:
        return lat_ref[...]
    return jnp.where(row0 < n_lat, lat_ref[...], ctx_ref[...])


def _row_specs(x_ctx, block, n_lat, col_of=None):
    n_lat_tiles = n_lat // block[0]
    col = (lambda *j: 0) if col_of is None else col_of
    if x_ctx is None:
        return [pl.BlockSpec(block, lambda i, *j: (i, col(*j)))]
    return [pl.BlockSpec(block, lambda i, *j: (jnp.minimum(i, n_lat_tiles - 1), col(*j))),
            pl.BlockSpec(block, lambda i, *j: (jnp.maximum(i - n_lat_tiles, 0), col(*j)))]


def _norm_mod_kernel(*refs, split, tm, n_lat, seq, n_batch, k_shift, k_scale):
    x_ref, xc_ref = (refs[0], refs[1]) if split else (refs[0], None)
    g_ref, mod_ref, o_ref, *maybe_ot_ref = refs[2 if split else 1:]
    d = x_ref.shape[1]
    row0 = pl.program_id(0) * tm
    b = _mod_row(row0, n_lat, seq, n_batch)
    xf = _tile_of(x_ref, xc_ref, row0, n_lat)
    y = xf * lax.rsqrt(jnp.mean(xf * xf, axis=-1, keepdims=True) + EPS)
    h = y * g_ref[...]
    shift = _mod_vec(mod_ref, b, k_shift, d)
    scale = _mod_vec(mod_ref, b, k_scale, d)
    h = h * (1.0 + scale) + shift
    o_ref[...] = h.astype(o_ref.dtype)
    for ot_ref in maybe_ot_ref:
        ot_ref[...] = h.T.astype(ot_ref.dtype)


def _norm_mod(x, g, mod, *, rows, n_lat, seq, n_batch, k_shift, k_scale, x_ctx=None,
              with_transpose=False, tm=512):
    d = x.shape[1]
    kern = functools.partial(_norm_mod_kernel, split=x_ctx is not None, tm=tm, n_lat=n_lat, seq=seq,
                             n_batch=n_batch, k_shift=k_shift, k_scale=k_scale)
    xs = [x] if x_ctx is None else [x, x_ctx]
    out_specs = [pl.BlockSpec((tm, d), lambda i: (i, 0))]
    out_shape = [jax.ShapeDtypeStruct((rows, d), BF16)]
    if with_transpose:
        out_specs.append(pl.BlockSpec((d, tm), lambda i: (0, i)))
        out_shape.append(jax.ShapeDtypeStruct((d, rows), BF16))
    out = pl.pallas_call(
        kern,
        grid=(rows // tm,),
        in_specs=_row_specs(x_ctx, (tm, d), n_lat) + [pl.BlockSpec((1, d), lambda i: (0, 0)),
                                                         pl.BlockSpec(mod.shape, lambda i: (0, 0))],
        out_specs=out_specs,
        out_shape=out_shape,
        compiler_params=_cparams("parallel"),
        name="norm_mod",
    )(*xs, g.reshape(1, d), mod)
    return out if with_transpose else out[0]


def _heads_norm_rope(acc, ones_bd, perm_bd, g, cos, sin, out_scale):
    n_heads = acc.shape[1] // HEAD_DIM
    ss = jnp.dot((acc * acc).astype(BF16), ones_bd, preferred_element_type=F32)
    y = acc * lax.rsqrt(ss * (1.0 / HEAD_DIM) + EPS) * g
    partner = jnp.dot(y.astype(BF16), perm_bd, preferred_element_type=F32)
    r = y * jnp.tile(cos, (1, n_heads)) + partner * jnp.tile(sin, (1, n_heads))
    if out_scale != 1.0:
        r = r * out_scale
    return r


def _q_kernel(h_ref, w_ref, ones_ref, perm_ref, g_ref, cos_ref, sin_ref, o_ref, *, out_scale):
    acc = jnp.dot(h_ref[...], w_ref[...], preferred_element_type=F32)
    r = _heads_norm_rope(acc, ones_ref[...], perm_ref[...], g_ref[...], cos_ref[...], sin_ref[...], out_scale)
    o_ref[...] = r.astype(o_ref.dtype)


def _kv_kernel(h_ref, wk_ref, wv_ref, ones_ref, perm_ref, g_ref, cos_ref, sin_ref, k_ref, v_ref):
    h = h_ref[...]
    acc = jnp.dot(h, wk_ref[...], preferred_element_type=F32)
    r = _heads_norm_rope(acc, ones_ref[...], perm_ref[...], g_ref[...], cos_ref[...], sin_ref[...], 1.0)
    k_ref[...] = r.astype(k_ref.dtype)
    vacc = jnp.dot(h, wv_ref[...], preferred_element_type=F32).astype(v_ref.dtype)
    ones = jnp.ones((vacc.shape[0], HEAD_DIM), v_ref.dtype)
    for g in range(vacc.shape[1] // HEAD_DIM):
        v_ref[:, 2 * g * HEAD_DIM:(2 * g + 1) * HEAD_DIM] = vacc[:, g * HEAD_DIM:(g + 1) * HEAD_DIM]
        v_ref[:, (2 * g + 1) * HEAD_DIM:(2 * g + 2) * HEAD_DIM] = ones


def _conv_in_kernel(h_ref, wu_ref, wb_ref, wc_ref, z_ref, gb_ref):
    h = h_ref[...]
    u = jnp.dot(h, wu_ref[...], preferred_element_type=F32)
    gc = jnp.dot(h, wc_ref[...], preferred_element_type=F32)
    z_ref[...] = (gc * u).astype(z_ref.dtype)
    gb_ref[...] = jnp.dot(h, wb_ref[...], preferred_element_type=F32).astype(gb_ref.dtype)


def _gate_kernel(h_ref, w_ref, b_ref, o_ref):
    acc = jnp.dot(h_ref[...], w_ref[...], preferred_element_type=F32)
    o_ref[...] = (0.5 + 0.5 * jnp.tanh(0.5 * (acc + b_ref[...]))).astype(o_ref.dtype)


def _rope_block(i, tm, n_lat, seq):
    return jnp.where(i * tm < n_lat, (i * tm % seq) // tm, seq // tm)


def _in_proj(h, w_in, q_g, k_g, b_gate, cos_t, sin_t, *, rows_q, rows_kv, n_lat, seq, d,
             tm=512, tn=512):
    kd = h.shape[1]
    kvw = d // KV_GROUPS
    q_end, k_end, v_end = d, d + kvw, d + 2 * kvw
    u_end, b_end, c_end = v_end + d, v_end + 2 * d, v_end + 3 * d
    hspec = pl.BlockSpec((tm, kd), lambda i, j: (i, 0))
    hd_spec = pl.BlockSpec((1, tn), lambda i, j: (0, 0))
    bd_spec = pl.BlockSpec((tn, tn), lambda i, j: (0, 0))
    heads = tn // HEAD_DIM
    lane = jnp.arange(HEAD_DIM)
    partner = jnp.where(lane % (HEAD_DIM // 2) < HEAD_DIM // 4, lane + HEAD_DIM // 4, lane - HEAD_DIM // 4)
    perm = (lane[:, None] == partner[None, :]).astype(BF16)
    ones_bd = jnp.kron(jnp.eye(heads, dtype=BF16), jnp.ones((HEAD_DIM, HEAD_DIM), BF16))
    perm_bd = jnp.kron(jnp.eye(heads, dtype=BF16), perm)
    tab_spec = pl.BlockSpec((tm, HEAD_DIM), lambda i, j: (_rope_block(i, tm, n_lat, seq), 0))

    def wspec(col0):
        return pl.BlockSpec((kd, tn), lambda i, j: (0, col0 // tn + j))

    def ospec():
        return pl.BlockSpec((tm, tn), lambda i, j: (i, j))

    q = pl.pallas_call(
        functools.partial(_q_kernel, out_scale=HEAD_DIM ** -0.5 * LOG2_E),
        grid=(rows_q // tm, d // tn),
        in_specs=[hspec, wspec(0), bd_spec, bd_spec, hd_spec, tab_spec, tab_spec],
        out_specs=ospec(),
        out_shape=jax.ShapeDtypeStruct((rows_q, d), BF16),
        compiler_params=_cparams("parallel", "arbitrary"),
        name="proj_q",
    )(h, w_in, ones_bd, perm_bd, jnp.tile(q_g, heads).reshape(1, tn), cos_t, sin_t)

    assert kvw == tn
    k, v = pl.pallas_call(
        _kv_kernel,
        grid=(rows_kv // tm, 1),
        in_specs=[hspec, wspec(q_end), wspec(k_end), bd_spec, bd_spec, hd_spec, tab_spec, tab_spec],
        out_specs=[ospec(), pl.BlockSpec((tm, 2 * tn), lambda i, j: (i, j))],
        out_shape=[jax.ShapeDtypeStruct((rows_kv, kvw), BF16), jax.ShapeDtypeStruct((rows_kv, 2 * kvw), BF16)],
        compiler_params=_cparams("parallel", "arbitrary"),
        name="proj_kv",
    )(h, w_in, w_in, ones_bd, perm_bd, jnp.tile(k_g, heads).reshape(1, tn), cos_t, sin_t)

    z, gb = pl.pallas_call(
        _conv_in_kernel,
        grid=(rows_q // tm, d // tn),
        in_specs=[hspec, wspec(v_end), wspec(u_end), wspec(b_end)],
        out_specs=[ospec(), ospec()],
        out_shape=[jax.ShapeDtypeStruct((rows_q, d), BF16)] * 2,
        compiler_params=_cparams("parallel", "arbitrary"),
        name="proj_conv",
    )(h, w_in, w_in, w_in)

    gates = pl.pallas_call(
        _gate_kernel,
        grid=(rows_q // tm, 2 * d // tn),
        in_specs=[hspec, wspec(c_end), pl.BlockSpec((1, tn), lambda i, j: (0, j))],
        out_specs=ospec(),
        out_shape=jax.ShapeDtypeStruct((rows_q, 2 * d), BF16),
        compiler_params=_cparams("parallel", "arbitrary"),
        name="proj_gates",
    )(h, w_in, b_gate.reshape(1, 2 * d))
    return q, k, v, z, gb, gates


ATTN_KEY_CHUNK = 256


def _attn_kernel(q_ref, kl_ref, vl_ref, kc_ref, vc_ref, o_ref, *, tq, n_lat):
    is_lat = pl.program_id(1) * tq < n_lat
    q = jnp.concatenate([q_ref[:, hh * HEAD_DIM:(hh + 1) * HEAD_DIM] for hh in range(KV_GROUPS)], axis=0)

    def step(carry, k, v):
        s = lax.dot_general(q, k, _NT, preferred_element_type=F32)
        m_new = jnp.max(s, axis=-1, keepdims=True)
        pv = lambda m: jnp.dot(jnp.exp2(s - m).astype(v.dtype), v, preferred_element_type=F32)
        if carry is None:
            return m_new, pv(m_new)
        m_old, acc = carry
        m_new = jnp.maximum(m_old, m_new)
        return m_new, acc * jnp.exp2(m_old - m_new) + pv(m_new)

    def run(with_lat):
        carry = step(None, kc_ref[...], vc_ref[...])
        if with_lat:
            for c in range(kl_ref.shape[0] // ATTN_KEY_CHUNK):
                rows = slice(c * ATTN_KEY_CHUNK, (c + 1) * ATTN_KEY_CHUNK)
                carry = step(carry, kl_ref[rows, :], vl_ref[rows, :])
        acc = carry[1]
        out = (acc[:, :HEAD_DIM] / acc[:, HEAD_DIM:]).astype(o_ref.dtype)
        for hh in range(KV_GROUPS):
            o_ref[:, hh * HEAD_DIM:(hh + 1) * HEAD_DIM] = out[hh * tq:(hh + 1) * tq, :]

    @pl.when(is_lat)
    def _():
        run(True)

    @pl.when(jnp.logical_not(is_lat))
    def _():
        run(False)


def _attention(q, k, v, *, rows_q, n_lat, seq, ctx_len, tq=256):
    assert ctx_len % tq == 0 or tq % ctx_len == 0
    assert tq <= ctx_len
    gw = KV_GROUPS * HEAD_DIM
    n_kv = k.shape[1] // HEAD_DIM

    def batch(i):
        return jnp.where(i * tq < n_lat, i * tq // seq, (i * tq - n_lat) // ctx_len)

    qspec = pl.BlockSpec((tq, gw), lambda g, i: (i, g))

    def lat_spec(width):
        return pl.BlockSpec((seq, width), lambda g, i: (batch(i), g))

    def ctx_spec(width):
        return pl.BlockSpec((ctx_len, width), lambda g, i: (n_lat // ctx_len + batch(i), g))

    return pl.pallas_call(
        functools.partial(_attn_kernel, tq=tq, n_lat=n_lat),
        grid=(n_kv, rows_q // tq),
        in_specs=[qspec, lat_spec(HEAD_DIM), lat_spec(2 * HEAD_DIM), ctx_spec(HEAD_DIM), ctx_spec(2 * HEAD_DIM)],
        out_specs=qspec,
        out_shape=jax.ShapeDtypeStruct((rows_q, q.shape[1]), BF16),
        compiler_params=_cparams("parallel", "arbitrary"),
        name="attention",
    )(q, k, v, k, v)


def _merge_kernel(attn_ref, z_ref, zp_ref, zn_ref, gb_ref, ga_ref, gc_ref, cw_ref,
                  wao_ref, wco_ref, o_ref, y_scr, *, tm, n_lat, seq, ctx_len):
    @pl.when(pl.program_id(1) == 0)
    def _():
        z = z_ref[...].astype(F32)
        loc = lax.broadcasted_iota(jnp.int32, (tm, 1), 0)
        row = pl.program_id(0) * tm + loc
        seqlen = jnp.where(row < n_lat, seq, ctx_len)
        pos = row & (seqlen - 1)
        halo_prev = zp_ref[...].astype(F32)[BF16_SUBLANES - 1:BF16_SUBLANES, :]
        halo_next = zn_ref[...].astype(F32)[0:1, :]
        z_prev = jnp.where(loc == 0, halo_prev, pltpu.roll(z, 1, 0))
        z_prev = jnp.where(pos == 0, 0.0, z_prev)
        z_next = jnp.where(loc == tm - 1, halo_next, pltpu.roll(z, tm - 1, 0))
        z_next = jnp.where(pos == seqlen - 1, 0.0, z_next)
        cw = cw_ref[...]
        conv = cw[0:1, :] * z_prev + cw[1:2, :] * z + cw[2:3, :] * z_next
        y_scr[...] = (gb_ref[...].astype(F32) * conv).astype(y_scr.dtype)

    acc_a = jnp.dot(attn_ref[...], wao_ref[...], preferred_element_type=F32)
    acc_c = jnp.dot(y_scr[...], wco_ref[...], preferred_element_type=F32)
    o_ref[...] = (ga_ref[...].astype(F32) * acc_a + gc_ref[...].astype(F32) * acc_c).astype(o_ref.dtype)


def _merge(attn, z, gb, gates, conv_w, w_ao, w_co, *, rows, n_lat, seq, ctx_len, tm=512, tn=1024):
    d = attn.shape[1]
    hb = BF16_SUBLANES
    n_hblk = z.shape[0] // hb
    row_spec = pl.BlockSpec((tm, d), lambda i, j: (i, 0))
    prev_spec = pl.BlockSpec((hb, d), lambda i, j: (jnp.maximum(i * (tm // hb) - 1, 0), 0))
    next_spec = pl.BlockSpec((hb, d), lambda i, j: (jnp.minimum((i + 1) * (tm // hb), n_hblk - 1), 0))
    wspec = pl.BlockSpec((d, tn), lambda i, j: (0, j))
    return pl.pallas_call(
        functools.partial(_merge_kernel, tm=tm, n_lat=n_lat, seq=seq, ctx_len=ctx_len),
        grid=(rows // tm, d // tn),
        in_specs=[row_spec, row_spec, prev_spec, next_spec, row_spec,
                  pl.BlockSpec((tm, tn), lambda i, j: (i, j)),
                  pl.BlockSpec((tm, tn), lambda i, j: (i, d // tn + j)),
                  pl.BlockSpec(conv_w.shape, lambda i, j: (0, 0)),
                  wspec, wspec],
        out_specs=pl.BlockSpec((tm, tn), lambda i, j: (i, j)),
        out_shape=jax.ShapeDtypeStruct((rows, d), BF16),
        scratch_shapes=[pltpu.VMEM((tm, d), BF16)],
        compiler_params=_cparams("parallel", "arbitrary"),
        name="merge",
    )(attn, z, z, z, gb, gates, gates, conv_w, w_ao, w_co)


def _resid_proj_kernel(a_ref, w_ref, *refs, split, tm, tn, n_lat, seq, n_batch, k_gate, d):
    x_ref, xc_ref = (refs[0], refs[1]) if split else (refs[0], None)
    mod_ref, o_ref = refs[2 if split else 1:]
    row0 = pl.program_id(0) * tm
    b = _mod_row(row0, n_lat, seq, n_batch)
    col0 = pl.multiple_of(k_gate * d + pl.program_id(1) * tn, LANES)
    gate = mod_ref[pl.ds(b, 1), pl.ds(col0, tn)]
    acc = jnp.dot(a_ref[...], w_ref[...], preferred_element_type=F32)
    o_ref[...] = _tile_of(x_ref, xc_ref, row0, n_lat) + gate * acc


def _resid_proj(a, w, x, mod, *, rows, n_lat, seq, n_batch, k_gate, x_ctx=None, tm=1024, tn=1024):
    d = a.shape[1]
    xs = [x] if x_ctx is None else [x, x_ctx]
    return pl.pallas_call(
        functools.partial(_resid_proj_kernel, split=x_ctx is not None, tm=tm, tn=tn, n_lat=n_lat, seq=seq,
                          n_batch=n_batch, k_gate=k_gate, d=d),
        grid=(rows // tm, d // tn),
        in_specs=[pl.BlockSpec((tm, d), lambda i, j: (i, 0)),
                  pl.BlockSpec((d, tn), lambda i, j: (0, j))]
        + _row_specs(x_ctx, (tm, tn), n_lat, col_of=lambda j: j)
        + [pl.BlockSpec(mod.shape, lambda i, j: (0, 0))],
        out_specs=pl.BlockSpec((tm, tn), lambda i, j: (i, j)),
        out_shape=jax.ShapeDtypeStruct((rows, d), F32),
        compiler_params=_cparams("parallel", "arbitrary"),
        name="resid_proj",
    )(a, w, *xs, mod)


N_TOP = PEER_TOPK + 1
TOP_ROWS = 24
N_CAND = 96


F32_SUBLANES = 8


def _merge_exchange_pairs(n):
    pairs = []
    p = 1
    while p < n:
        k = p
        while k >= 1:
            for j in range(k % p, n - k, 2 * k):
                for i in range(min(k, n - j - k)):
                    if (i + j) // (2 * p) == (i + j + k) // (2 * p):
                        pairs.append((i + j, i + j + k))
            k //= 2
        p *= 2
    return pairs


def _top_values(s, k, out_ref):
    n = s.shape[0] // F32_SUBLANES
    r = [s[i * F32_SUBLANES:(i + 1) * F32_SUBLANES, :] for i in range(n)]
    for a, b in _merge_exchange_pairs(n):
        r[a], r[b] = jnp.maximum(r[a], r[b]), jnp.minimum(r[a], r[b])
    prev = mx = None
    for t in range(k):
        prev = mx
        mx = jnp.max(r[0], axis=0, keepdims=True)
        if out_ref is not None:
            out_ref[t:t + 1, :] = mx
        pop = r[0] == mx
        for i in range(min(n, k - 1 - t)):
            r[i] = jnp.where(pop, r[i + 1] if i + 1 < n else -jnp.inf, r[i])
    return prev, mx


def _candidate_sums(v1_ref, v2_ref, cand_ref):
    cand_ref[0:TOP_ROWS, :] = v1_ref[0:1, :] + v2_ref[...]
    for a in range(1, 8):
        cand_ref[16 + 8 * a:24 + 8 * a, :] = v1_ref[a:a + 1, :] + v2_ref[0:8, :]
    cand_ref[80:96, :] = v1_ref[8:TOP_ROWS, :] + v2_ref[0:1, :]


def _peer_score_kernel(h_ref, wq_ref, k1_ref, k2_ref, th_ref, e1_ref, e2_ref,
                       v1_scr, v2_scr, cand_scr, *, n_i):
    qp = jnp.dot(h_ref[...], wq_ref[...], preferred_element_type=F32)
    pad = jnp.full((TOP_ROWS - N_TOP, v1_scr.shape[1]), -jnp.inf, F32)
    v1_scr[N_TOP:TOP_ROWS, :] = pad
    v2_scr[N_TOP:TOP_ROWS, :] = pad
    for h in range(PEER_HEADS):
        q1 = qp[:, (2 * h) * N_KEYS:(2 * h + 1) * N_KEYS]
        q2 = qp[:, (2 * h + 1) * N_KEYS:(2 * h + 2) * N_KEYS]
        s1 = lax.dot_general(k1_ref[h], q1, _NT, preferred_element_type=F32) * LOG2_E
        s2 = lax.dot_general(k2_ref[h], q2, _NT, preferred_element_type=F32) * LOG2_E
        _top_values(s1, N_TOP, v1_scr)
        _top_values(s2, N_TOP, v2_scr)
        _candidate_sums(v1_scr, v2_scr, cand_scr)
        cand = cand_scr[...]
        c16, c17 = _top_values(cand, N_TOP, None)
        tau = 0.5 * (c16 + c17)
        top1 = v1_scr[0:1, :]
        top2 = v2_scr[0:1, :]
        z = jnp.sum(jnp.where(cand >= c16, jnp.exp2(cand - (top1 + top2)), 0.0), axis=0, keepdims=True)
        theta = jnp.exp2(tau - s1 - top2)
        e1 = jnp.exp2(s1 - (top1 + jnp.log2(z)))
        for g in range(N_KEYS // n_i):
            th_ref[g, h * n_i:(h + 1) * n_i, :] = theta[g * n_i:(g + 1) * n_i, :]
            e1_ref[g, h * n_i:(h + 1) * n_i, :] = e1[g * n_i:(g + 1) * n_i, :]
        e2_ref[h] = jnp.exp2(s2 - top2)


def _peer_scores(h2, wq, k1, k2, *, rows, n_i, tm=256):
    d = h2.shape[1]
    n_grp = N_KEYS // n_i
    grouped = pl.BlockSpec((n_grp, PEER_HEADS * n_i, tm), lambda i: (0, 0, i))
    by_head = pl.BlockSpec((PEER_HEADS, N_KEYS, tm), lambda i: (0, 0, i))
    grouped_shape = jax.ShapeDtypeStruct((n_grp, PEER_HEADS * n_i, rows), F32)
    by_head_shape = jax.ShapeDtypeStruct((PEER_HEADS, N_KEYS, rows), F32)
    return pl.pallas_call(
        functools.partial(_peer_score_kernel, n_i=n_i),
        grid=(rows // tm,),
        in_specs=[pl.BlockSpec((tm, d), lambda i: (i, 0)),
                  pl.BlockSpec(wq.shape, lambda i: (0, 0)),
                  pl.BlockSpec(k1.shape, lambda i: (0, 0, 0)),
                  pl.BlockSpec(k2.shape, lambda i: (0, 0, 0))],
        out_specs=[grouped, grouped, by_head],
        out_shape=[grouped_shape, grouped_shape, by_head_shape],
        scratch_shapes=[pltpu.VMEM((TOP_ROWS, tm), F32), pltpu.VMEM((TOP_ROWS, tm), F32),
                        pltpu.VMEM((N_CAND, tm), F32)],
        compiler_params=_cparams("parallel"),
        name="peer_scores",
    )(h2, wq, k1, k2)


def _gelu(x):
    return (0.5 * x) * (1.0 + lax.erf(x * (0.5 ** 0.5)))


def _peer_mix_kernel(ht_ref, u_ref, vt_ref, th_ref, e1_ref, e2_ref, x_ref, mod_ref, *rest,
                     tt, te, n_lat, seq, n_batch, k_gate, final_norm):
    if final_norm:
        fg_ref, o_ref, acc_ref, act_scr, g_scr, w_scr = rest
    else:
        o_ref, acc_ref, act_scr, g_scr, w_scr = rest
    j = pl.program_id(1)
    last = pl.num_programs(1) - 1
    d = x_ref.shape[1]
    n_i = te // N_KEYS

    @pl.when(jnp.logical_and(pl.program_id(0) == 0, j == 0))
    def _():
        acc_ref[...] = jnp.zeros_like(acc_ref)

    act_scr[j % 2] = _gelu(jnp.dot(u_ref[...], ht_ref[...], preferred_element_type=F32))

    tails = []
    for ii in range(n_i):
        for tc in range(tt // LANES):
            cs = slice(tc * LANES, (tc + 1) * LANES)
            for r in range(N_KEYS // F32_SUBLANES):
                ks = slice(r * F32_SUBLANES, (r + 1) * F32_SUBLANES)
                gate = tails[-GATE_CHAINS] * 0.0 if len(tails) >= GATE_CHAINS else None
                for h in range(PEER_HEADS):
                    row = slice(h * n_i + ii, h * n_i + ii + 1)
                    e2 = e2_ref[h, ks, cs]
                    wgt = jnp.where(e2 >= th_ref[0, row, cs], e2, 0.0) * e1_ref[0, row, cs]
                    gate = wgt if gate is None else gate + wgt
                tails.append(gate)
                g_scr[ii * N_KEYS + r * F32_SUBLANES:ii * N_KEYS + (r + 1) * F32_SUBLANES, cs] = gate

    prev = act_scr[(j + 1) % 2]
    w_scr[...] = jnp.where(j >= 1, g_scr[...] * prev, 0.0).astype(w_scr.dtype)
    acc_ref[...] += jnp.dot(vt_ref[...], w_scr[...], preferred_element_type=F32)

    @pl.when(j == last)
    def _():
        b = _mod_row(pl.program_id(0) * tt, n_lat, seq, n_batch)
        xn = x_ref[...] + _mod_vec(mod_ref, b, k_gate, d) * acc_ref[...].T
        if final_norm:
            xn = xn * lax.rsqrt(jnp.mean(xn * xn, axis=-1, keepdims=True) + EPS) * fg_ref[...]
        o_ref[...] = xn
        acc_ref[...] = jnp.zeros_like(acc_ref)


PEER_TE = 512
GATE_CHAINS = 2


def _peer_mix(h2t, u, vt, th, e1, e2, x, mod, final_g, *, rows, n_lat, seq, n_batch, k_gate,
              tt=512, te=PEER_TE):
    d = x.shape[1]
    n_tiles = u.shape[0] // te
    final_norm = final_g is not None
    n_i = te // N_KEYS

    def cur(j):
        return jnp.minimum(j, n_tiles - 1)

    def prv(j):
        return jnp.maximum(j - 1, 0)

    grouped = pl.BlockSpec((1, PEER_HEADS * n_i, tt), lambda t, j: (prv(j), 0, t))
    by_head = pl.BlockSpec((PEER_HEADS, N_KEYS, tt), lambda t, j: (0, 0, t))
    in_specs = [pl.BlockSpec((d, tt), lambda t, j: (0, t)),
                pl.BlockSpec((te, d), lambda t, j: (cur(j), 0)),
                pl.BlockSpec((d, te), lambda t, j: (0, prv(j))),
                grouped, grouped, by_head,
                pl.BlockSpec((tt, d), lambda t, j: (t, 0)),
                pl.BlockSpec(mod.shape, lambda t, j: (0, 0))]
    args = [h2t, u, vt, th, e1, e2, x, mod]
    if final_norm:
        in_specs.append(pl.BlockSpec((1, d), lambda t, j: (0, 0)))
        args.append(final_g.reshape(1, d))
    return pl.pallas_call(
        functools.partial(_peer_mix_kernel, tt=tt, te=te, n_lat=n_lat, seq=seq, n_batch=n_batch,
                          k_gate=k_gate, final_norm=final_norm),
        grid=(rows // tt, n_tiles + 1),
        in_specs=in_specs,
        out_specs=pl.BlockSpec((tt, d), lambda t, j: (t, 0)),
        out_shape=jax.ShapeDtypeStruct((rows, d), F32),
        scratch_shapes=[pltpu.VMEM((d, tt), F32), pltpu.VMEM((2, te, tt), F32), pltpu.VMEM((te, tt), F32),
                        pltpu.VMEM((te, tt), BF16)],
        compiler_params=_cparams("arbitrary", "arbitrary"),
        name="peer_mix",
    )(*args)


def _rope_tables(seq, pad_rows):
    half = HEAD_DIM // 2
    t = jnp.arange(seq)
    row = (t // GRID_W).astype(F32)
    col = (t % GRID_W).astype(F32)
    inv = ROPE_THETA ** (-jnp.arange(0, half, 2, dtype=F32) / half)
    ang_r = row[:, None] * inv[None, :]
    ang_c = col[:, None] * inv[None, :]
    cos = jnp.concatenate([jnp.cos(ang_r), jnp.cos(ang_r), jnp.cos(ang_c), jnp.cos(ang_c)], axis=-1)
    sin = jnp.concatenate([-jnp.sin(ang_r), jnp.sin(ang_r), -jnp.sin(ang_c), jnp.sin(ang_c)], axis=-1)
    cos = jnp.concatenate([cos, jnp.ones((pad_rows, HEAD_DIM), F32)], axis=0)
    sin = jnp.concatenate([sin, jnp.zeros((pad_rows, HEAD_DIM), F32)], axis=0)
    return cos, sin


def kernel(x, c, ctx, c_ctx, ada_w, ada_b, norm1_g, norm2_g, w_in, b_gate, q_norm_g, k_norm_g,
           conv_w, w_attn_out, w_conv_out, w_o, peer_wq, peer_k1, peer_k2, peer_u, peer_v, final_g):
    n_batch, seq, d = x.shape
    ctx_len = ctx.shape[1]
    depth = ada_w.shape[0]
    n_lat = n_batch * seq
    n_all = n_lat + n_batch * ctx_len
    assert seq & (seq - 1) == 0 and ctx_len & (ctx_len - 1) == 0
    tm = 1024
    assert seq % tm == 0 and (n_batch * ctx_len) % tm == 0 and seq % GRID_W == 0

    xa = x.reshape(n_lat, d)
    xc = ctx.reshape(n_batch * ctx_len, d)
    pad = (-(n_batch + 1)) % 8
    cvec = jnp.concatenate([c, c_ctx[None, :], jnp.zeros((pad, d), F32)], axis=0)
    mod_all = _adaln(cvec, ada_w, ada_b)
    cos_t, sin_t = _rope_tables(seq, tm)
    geom = dict(n_lat=n_lat, seq=seq, n_batch=n_batch)

    for l in range(depth):
        last = l == depth - 1
        rows = n_lat if last else n_all
        mod = mod_all[l]
        w_in_l = w_in[l].astype(BF16)

        h = _norm_mod(xa, norm1_g[l], mod, rows=n_all, k_shift=0, k_scale=1, x_ctx=xc, **geom)
        q, k, v, z, gb, gates = _in_proj(h, w_in_l, q_norm_g[l], k_norm_g[l], b_gate[l], cos_t, sin_t,
                                         rows_q=rows, rows_kv=n_all, n_lat=n_lat, seq=seq, d=d, tm=tm)
        attn = _attention(q, k, v, rows_q=rows, n_lat=n_lat, seq=seq, ctx_len=ctx_len)
        merged = _merge(attn, z, gb, gates, conv_w[l], w_attn_out[l].astype(BF16),
                        w_conv_out[l].astype(BF16), rows=rows, n_lat=n_lat, seq=seq, ctx_len=ctx_len)
        xa = _resid_proj(merged, w_o[l].astype(BF16), xa, mod, rows=rows, k_gate=2, x_ctx=xc, **geom)
        xc = None

        h2, h2t = _norm_mod(xa, norm2_g[l], mod, rows=rows, k_shift=3, k_scale=4, with_transpose=True, **geom)
        th, e1, e2 = _peer_scores(h2, peer_wq[l].astype(BF16), peer_k1[l], peer_k2[l], rows=rows,
                                  n_i=PEER_TE // N_KEYS)
        xa = _peer_mix(h2t, peer_u[l].astype(BF16), peer_v[l].T.astype(BF16), th, e1, e2, xa, mod,
                       final_g if last else None, rows=rows, k_gate=5, **geom)
    return xa.reshape(n_batch, seq, d)
```

```python
import functools

import jax
import jax.numpy as jnp
from jax import lax
from jax.experimental import pallas as pl
from jax.experimental.pallas import tpu as pltpu

F32 = jnp.float32
BF16 = jnp.bfloat16

HEAD_DIM = 128
KV_GROUPS = 4
GRID_W = 64
ROPE_THETA = 10000.0
PEER_HEADS = 8
N_KEYS = 128
PEER_TOPK = 16
N_MOD = 6
EPS = 1e-6
LOG2_E = 1.4426950408889634
LANES = 128
BF16_SUBLANES = 16
VMEM_LIMIT_BYTES = 56 * 1024 * 1024

_NT = (((1,), (1,)), ((), ()))


def _cparams(*sem, flags=None):
    return pltpu.CompilerParams(dimension_semantics=sem, vmem_limit_bytes=VMEM_LIMIT_BYTES, flags=flags)


def _mod_row(row0, n_lat, seq, n_batch):
    return jnp.where(row0 < n_lat, row0 // seq, n_batch)


def _mod_vec(mod_ref, b, k, d):
    return mod_ref[pl.ds(b, 1), k * d:(k + 1) * d]


def _adaln_kernel(c_ref, w_ref, b_ref, o_ref):
    cv = c_ref[...]
    s = cv * jax.nn.sigmoid(cv)
    o_ref[0] = jnp.dot(s, w_ref[0], preferred_element_type=F32) + b_ref[0]


def _adaln(cvec, ada_w, ada_b, tn=1024):
    depth, d, n = ada_w.shape
    rows = cvec.shape[0]
    return pl.pallas_call(
        _adaln_kernel,
        grid=(depth, n // tn),
        in_specs=[pl.BlockSpec((rows, d), lambda l, j: (0, 0)),
                  pl.BlockSpec((1, d, tn), lambda l, j: (l, 0, j)),
                  pl.BlockSpec((1, 1, tn), lambda l, j: (l, 0, j))],
        out_specs=pl.BlockSpec((1, rows, tn), lambda l, j: (l, 0, j)),
        out_shape=jax.ShapeDtypeStruct((depth, rows, n), F32),
        compiler_params=_cparams("arbitrary", "arbitrary"),
        name="adaln",
    )(cvec, ada_w, ada_b.reshape(depth, 1, n))


def _tile_of(lat_ref, ctx_ref, row0, n_lat):
    if ctx_ref is None:
        return lat_ref[...]
    return jnp.where(row0 < n_lat, lat_ref[...], ctx_ref[...])


def _row_specs(x_ctx, block, n_lat, col_of=None):
    n_lat_tiles = n_lat // block[0]
    col = (lambda *j: 0) if col_of is None else col_of
    if x_ctx is None:
        return [pl.BlockSpec(block, lambda i, *j: (i, col(*j)))]
    return [pl.BlockSpec(block, lambda i, *j: (jnp.minimum(i, n_lat_tiles - 1), col(*j))),
            pl.BlockSpec(block, lambda i, *j: (jnp.maximum(i - n_lat_tiles, 0), col(*j)))]


def _norm_mod_kernel(*refs, split, tm, n_lat, seq, n_batch, k_shift, k_scale):
    x_ref, xc_ref = (refs[0], refs[1]) if split else (refs[0], None)
    g_ref, mod_ref, o_ref, *maybe_ot_ref = refs[2 if split else 1:]
    d = x_ref.shape[1]
    row0 = pl.program_id(0) * tm
    b = _mod_row(row0, n_lat, seq, n_batch)
    xf = _tile_of(x_ref, xc_ref, row0, n_lat)
    y = xf * lax.rsqrt(jnp.mean(xf * xf, axis=-1, keepdims=True) + EPS)
    h = y * g_ref[...]
    shift = _mod_vec(mod_ref, b, k_shift, d)
    scale = _mod_vec(mod_ref, b, k_scale, d)
    h = h * (1.0 + scale) + shift
    o_ref[...] = h.astype(o_ref.dtype)
    for ot_ref in maybe_ot_ref:
        ot_ref[...] = h.T.astype(ot_ref.dtype)


def _norm_mod(x, g, mod, *, rows, n_lat, seq, n_batch, k_shift, k_scale, x_ctx=None,
              with_transpose=False, tm=512):
    d = x.shape[1]
    kern = functools.partial(_norm_mod_kernel, split=x_ctx is not None, tm=tm, n_lat=n_lat, seq=seq,
                             n_batch=n_batch, k_shift=k_shift, k_scale=k_scale)
    xs = [x] if x_ctx is None else [x, x_ctx]
    out_specs = [pl.BlockSpec((tm, d), lambda i: (i, 0))]
    out_shape = [jax.ShapeDtypeStruct((rows, d), BF16)]
    if with_transpose:
        out_specs.append(pl.BlockSpec((d, tm), lambda i: (0, i)))
        out_shape.append(jax.ShapeDtypeStruct((d, rows), BF16))
    out = pl.pallas_call(
        kern,
        grid=(rows // tm,),
        in_specs=_row_specs(x_ctx, (tm, d), n_lat) + [pl.BlockSpec((1, d), lambda i: (0, 0)),
                                                         pl.BlockSpec(mod.shape, lambda i: (0, 0))],
        out_specs=out_specs,
        out_shape=out_shape,
        compiler_params=_cparams("parallel"),
        name="norm_mod",
    )(*xs, g.reshape(1, d), mod)
    return out if with_transpose else out[0]


def _heads_norm_rope(acc, ones_bd, perm_bd, g, cos, sin, out_scale):
    n_heads = acc.shape[1] // HEAD_DIM
    ss = jnp.dot((acc * acc).astype(BF16), ones_bd, preferred_element_type=F32)
    y = acc * lax.rsqrt(ss * (1.0 / HEAD_DIM) + EPS) * g
    partner = jnp.dot(y.astype(BF16), perm_bd, preferred_element_type=F32)
    r = y * jnp.tile(cos, (1, n_heads)) + partner * jnp.tile(sin, (1, n_heads))
    if out_scale != 1.0:
        r = r * out_scale
    return r


def _q_kernel(h_ref, w_ref, ones_ref, perm_ref, g_ref, cos_ref, sin_ref, o_ref, *, out_scale):
    acc = jnp.dot(h_ref[...], w_ref[...], preferred_element_type=F32)
    r = _heads_norm_rope(acc, ones_ref[...], perm_ref[...], g_ref[...], cos_ref[...], sin_ref[...], out_scale)
    o_ref[...] = r.astype(o_ref.dtype)


def _kv_kernel(h_ref, wk_ref, wv_ref, ones_ref, perm_ref, g_ref, cos_ref, sin_ref, k_ref, v_ref):
    h = h_ref[...]
    acc = jnp.dot(h, wk_ref[...], preferred_element_type=F32)
    r = _heads_norm_rope(acc, ones_ref[...], perm_ref[...], g_ref[...], cos_ref[...], sin_ref[...], 1.0)
    k_ref[...] = r.astype(k_ref.dtype)
    vacc = jnp.dot(h, wv_ref[...], preferred_element_type=F32).astype(v_ref.dtype)
    ones = jnp.ones((vacc.shape[0], HEAD_DIM), v_ref.dtype)
    for g in range(vacc.shape[1] // HEAD_DIM):
        v_ref[:, 2 * g * HEAD_DIM:(2 * g + 1) * HEAD_DIM] = vacc[:, g * HEAD_DIM:(g + 1) * HEAD_DIM]
        v_ref[:, (2 * g + 1) * HEAD_DIM:(2 * g + 2) * HEAD_DIM] = ones


def _conv_in_kernel(h_ref, wu_ref, wb_ref, wc_ref, z_ref, gb_ref):
    h = h_ref[...]
    u = jnp.dot(h, wu_ref[...], preferred_element_type=F32)
    gc = jnp.dot(h, wc_ref[...], preferred_element_type=F32)
    z_ref[...] = (gc * u).astype(z_ref.dtype)
    gb_ref[...] = jnp.dot(h, wb_ref[...], preferred_element_type=F32).astype(gb_ref.dtype)


def _gate_kernel(h_ref, w_ref, b_ref, o_ref):
    acc = jnp.dot(h_ref[...], w_ref[...], preferred_element_type=F32)
    o_ref[...] = (0.5 + 0.5 * jnp.tanh(0.5 * (acc + b_ref[...]))).astype(o_ref.dtype)


def _rope_block(i, tm, n_lat, seq):
    return jnp.where(i * tm < n_lat, (i * tm % seq) // tm, seq // tm)


def _in_proj(h, w_in, q_g, k_g, b_gate, cos_t, sin_t, *, rows_q, rows_kv, n_lat, seq, d,
             tm=512, tn=512):
    kd = h.shape[1]
    kvw = d // KV_GROUPS
    q_end, k_end, v_end = d, d + kvw, d + 2 * kvw
    u_end, b_end, c_end = v_end + d, v_end + 2 * d, v_end + 3 * d
    hspec = pl.BlockSpec((tm, kd), lambda i, j: (i, 0))
    hd_spec = pl.BlockSpec((1, tn), lambda i, j: (0, 0))
    bd_spec = pl.BlockSpec((tn, tn), lambda i, j: (0, 0))
    heads = tn // HEAD_DIM
    lane = jnp.arange(HEAD_DIM)
    partner = jnp.where(lane % (HEAD_DIM // 2) < HEAD_DIM // 4, lane + HEAD_DIM // 4, lane - HEAD_DIM // 4)
    perm = (lane[:, None] == partner[None, :]).astype(BF16)
    ones_bd = jnp.kron(jnp.eye(heads, dtype=BF16), jnp.ones((HEAD_DIM, HEAD_DIM), BF16))
    perm_bd = jnp.kron(jnp.eye(heads, dtype=BF16), perm)
    tab_spec = pl.BlockSpec((tm, HEAD_DIM), lambda i, j: (_rope_block(i, tm, n_lat, seq), 0))

    def wspec(col0):
        return pl.BlockSpec((kd, tn), lambda i, j: (0, col0 // tn + j))

    def ospec():
        return pl.BlockSpec((tm, tn), lambda i, j: (i, j))

    q = pl.pallas_call(
        functools.partial(_q_kernel, out_scale=HEAD_DIM ** -0.5 * LOG2_E),
        grid=(rows_q // tm, d // tn),
        in_specs=[hspec, wspec(0), bd_spec, bd_spec, hd_spec, tab_spec, tab_spec],
        out_specs=ospec(),
        out_shape=jax.ShapeDtypeStruct((rows_q, d), BF16),
        compiler_params=_cparams("parallel", "arbitrary"),
        name="proj_q",
    )(h, w_in, ones_bd, perm_bd, jnp.tile(q_g, heads).reshape(1, tn), cos_t, sin_t)

    assert kvw == tn
    k, v = pl.pallas_call(
        _kv_kernel,
        grid=(rows_kv // tm, 1),
        in_specs=[hspec, wspec(q_end), wspec(k_end), bd_spec, bd_spec, hd_spec, tab_spec, tab_spec],
        out_specs=[ospec(), pl.BlockSpec((tm, 2 * tn), lambda i, j: (i, j))],
        out_shape=[jax.ShapeDtypeStruct((rows_kv, kvw), BF16), jax.ShapeDtypeStruct((rows_kv, 2 * kvw), BF16)],
        compiler_params=_cparams("parallel", "arbitrary"),
        name="proj_kv",
    )(h, w_in, w_in, ones_bd, perm_bd, jnp.tile(k_g, heads).reshape(1, tn), cos_t, sin_t)

    z, gb = pl.pallas_call(
        _conv_in_kernel,
        grid=(rows_q // tm, d // tn),
        in_specs=[hspec, wspec(v_end), wspec(u_end), wspec(b_end)],
        out_specs=[ospec(), ospec()],
        out_shape=[jax.ShapeDtypeStruct((rows_q, d), BF16)] * 2,
        compiler_params=_cparams("parallel", "arbitrary"),
        name="proj_conv",
    )(h, w_in, w_in, w_in)

    gates = pl.pallas_call(
        _gate_kernel,
        grid=(rows_q // tm, 2 * d // tn),
        in_specs=[hspec, wspec(c_end), pl.BlockSpec((1, tn), lambda i, j: (0, j))],
        out_specs=ospec(),
        out_shape=jax.ShapeDtypeStruct((rows_q, 2 * d), BF16),
        compiler_params=_cparams("parallel", "arbitrary"),
        name="proj_gates",
    )(h, w_in, b_gate.reshape(1, 2 * d))
    return q, k, v, z, gb, gates


ATTN_KEY_CHUNK = 256


def _attn_kernel(q_ref, kl_ref, vl_ref, kc_ref, vc_ref, o_ref, *, tq, n_lat):
    is_lat = pl.program_id(1) * tq < n_lat
    q = jnp.concatenate([q_ref[:, hh * HEAD_DIM:(hh + 1) * HEAD_DIM] for hh in range(KV_GROUPS)], axis=0)

    def step(carry, k, v):
        s = lax.dot_general(q, k, _NT, preferred_element_type=F32)
        m_new = jnp.max(s, axis=-1, keepdims=True)
        pv = lambda m: jnp.dot(jnp.exp2(s - m).astype(v.dtype), v, preferred_element_type=F32)
        if carry is None:
            return m_new, pv(m_new)
        m_old, acc = carry
        m_new = jnp.maximum(m_old, m_new)
        return m_new, acc * jnp.exp2(m_old - m_new) + pv(m_new)

    def run(with_lat):
        carry = step(None, kc_ref[...], vc_ref[...])
        if with_lat:
            for c in range(kl_ref.shape[0] // ATTN_KEY_CHUNK):
                rows = slice(c * ATTN_KEY_CHUNK, (c + 1) * ATTN_KEY_CHUNK)
                carry = step(carry, kl_ref[rows, :], vl_ref[rows, :])
        acc = carry[1]
        out = (acc[:, :HEAD_DIM] / acc[:, HEAD_DIM:]).astype(o_ref.dtype)
        for hh in range(KV_GROUPS):
            o_ref[:, hh * HEAD_DIM:(hh + 1) * HEAD_DIM] = out[hh * tq:(hh + 1) * tq, :]

    @pl.when(is_lat)
    def _():
        run(True)

    @pl.when(jnp.logical_not(is_lat))
    def _():
        run(False)


def _attention(q, k, v, *, rows_q, n_lat, seq, ctx_len, tq=256):
    assert ctx_len % tq == 0 or tq % ctx_len == 0
    assert tq <= ctx_len
    gw = KV_GROUPS * HEAD_DIM
    n_kv = k.shape[1] // HEAD_DIM

    def batch(i):
        return jnp.where(i * tq < n_lat, i * tq // seq, (i * tq - n_lat) // ctx_len)

    qspec = pl.BlockSpec((tq, gw), lambda g, i: (i, g))

    def lat_spec(width):
        return pl.BlockSpec((seq, width), lambda g, i: (batch(i), g))

    def ctx_spec(width):
        return pl.BlockSpec((ctx_len, width), lambda g, i: (n_lat // ctx_len + batch(i), g))

    return pl.pallas_call(
        functools.partial(_attn_kernel, tq=tq, n_lat=n_lat),
        grid=(n_kv, rows_q // tq),
        in_specs=[qspec, lat_spec(HEAD_DIM), lat_spec(2 * HEAD_DIM), ctx_spec(HEAD_DIM), ctx_spec(2 * HEAD_DIM)],
        out_specs=qspec,
        out_shape=jax.ShapeDtypeStruct((rows_q, q.shape[1]), BF16),
        compiler_params=_cparams("parallel", "arbitrary"),
        name="attention",
    )(q, k, v, k, v)


def _merge_kernel(attn_ref, z_ref, zp_ref, zn_ref, gb_ref, ga_ref, gc_ref, cw_ref,
                  wao_ref, wco_ref, o_ref, y_scr, *, tm, n_lat, seq, ctx_len):
    @pl.when(pl.program_id(1) == 0)
    def _():
        z = z_ref[...].astype(F32)
        loc = lax.broadcasted_iota(jnp.int32, (tm, 1), 0)
        row = pl.program_id(0) * tm + loc
        seqlen = jnp.where(row < n_lat, seq, ctx_len)
        pos = row & (seqlen - 1)
        halo_prev = zp_ref[...].astype(F32)[BF16_SUBLANES - 1:BF16_SUBLANES, :]
        halo_next = zn_ref[...].astype(F32)[0:1, :]
        z_prev = jnp.where(loc == 0, halo_prev, pltpu.roll(z, 1, 0))
        z_prev = jnp.where(pos == 0, 0.0, z_prev)
        z_next = jnp.where(loc == tm - 1, halo_next, pltpu.roll(z, tm - 1, 0))
        z_next = jnp.where(pos == seqlen - 1, 0.0, z_next)
        cw = cw_ref[...]
        conv = cw[0:1, :] * z_prev + cw[1:2, :] * z + cw[2:3, :] * z_next
        y_scr[...] = (gb_ref[...].astype(F32) * conv).astype(y_scr.dtype)

    acc_a = jnp.dot(attn_ref[...], wao_ref[...], preferred_element_type=F32)
    acc_c = jnp.dot(y_scr[...], wco_ref[...], preferred_element_type=F32)
    o_ref[...] = (ga_ref[...].astype(F32) * acc_a + gc_ref[...].astype(F32) * acc_c).astype(o_ref.dtype)


def _merge(attn, z, gb, gates, conv_w, w_ao, w_co, *, rows, n_lat, seq, ctx_len, tm=512, tn=1024):
    d = attn.shape[1]
    hb = BF16_SUBLANES
    n_hblk = z.shape[0] // hb
    row_spec = pl.BlockSpec((tm, d), lambda i, j: (i, 0))
    prev_spec = pl.BlockSpec((hb, d), lambda i, j: (jnp.maximum(i * (tm // hb) - 1, 0), 0))
    next_spec = pl.BlockSpec((hb, d), lambda i, j: (jnp.minimum((i + 1) * (tm // hb), n_hblk - 1), 0))
    wspec = pl.BlockSpec((d, tn), lambda i, j: (0, j))
    return pl.pallas_call(
        functools.partial(_merge_kernel, tm=tm, n_lat=n_lat, seq=seq, ctx_len=ctx_len),
        grid=(rows // tm, d // tn),
        in_specs=[row_spec, row_spec, prev_spec, next_spec, row_spec,
                  pl.BlockSpec((tm, tn), lambda i, j: (i, j)),
                  pl.BlockSpec((tm, tn), lambda i, j: (i, d // tn + j)),
                  pl.BlockSpec(conv_w.shape, lambda i, j: (0, 0)),
                  wspec, wspec],
        out_specs=pl.BlockSpec((tm, tn), lambda i, j: (i, j)),
        out_shape=jax.ShapeDtypeStruct((rows, d), BF16),
        scratch_shapes=[pltpu.VMEM((tm, d), BF16)],
        compiler_params=_cparams("parallel", "arbitrary"),
        name="merge",
    )(attn, z, z, z, gb, gates, gates, conv_w, w_ao, w_co)


def _resid_proj_kernel(a_ref, w_ref, *refs, split, tm, tn, n_lat, seq, n_batch, k_gate, d):
    x_ref, xc_ref = (refs[0], refs[1]) if split else (refs[0], None)
    mod_ref, o_ref = refs[2 if split else 1:]
    row0 = pl.program_id(0) * tm
    b = _mod_row(row0, n_lat, seq, n_batch)
    col0 = pl.multiple_of(k_gate * d + pl.program_id(1) * tn, LANES)
    gate = mod_ref[pl.ds(b, 1), pl.ds(col0, tn)]
    acc = jnp.dot(a_ref[...], w_ref[...], preferred_element_type=F32)
    o_ref[...] = _tile_of(x_ref, xc_ref, row0, n_lat) + gate * acc


def _resid_proj(a, w, x, mod, *, rows, n_lat, seq, n_batch, k_gate, x_ctx=None, tm=1024, tn=1024):
    d = a.shape[1]
    xs = [x] if x_ctx is None else [x, x_ctx]
    return pl.pallas_call(
        functools.partial(_resid_proj_kernel, split=x_ctx is not None, tm=tm, tn=tn, n_lat=n_lat, seq=seq,
                          n_batch=n_batch, k_gate=k_gate, d=d),
        grid=(rows // tm, d // tn),
        in_specs=[pl.BlockSpec((tm, d), lambda i, j: (i, 0)),
                  pl.BlockSpec((d, tn), lambda i, j: (0, j))]
        + _row_specs(x_ctx, (tm, tn), n_lat, col_of=lambda j: j)
        + [pl.BlockSpec(mod.shape, lambda i, j: (0, 0))],
        out_specs=pl.BlockSpec((tm, tn), lambda i, j: (i, j)),
        out_shape=jax.ShapeDtypeStruct((rows, d), F32),
        compiler_params=_cparams("parallel", "arbitrary"),
        name="resid_proj",
    )(a, w, *xs, mod)


N_TOP = PEER_TOPK + 1
TOP_ROWS = 24
N_CAND = 96


F32_SUBLANES = 8


def _merge_exchange_pairs(n):
    pairs = []
    p = 1
    while p < n:
        k = p
        while k >= 1:
            for j in range(k % p, n - k, 2 * k):
                for i in range(min(k, n - j - k)):
                    if (i + j) // (2 * p) == (i + j + k) // (2 * p):
                        pairs.append((i + j, i + j + k))
            k //= 2
        p *= 2
    return pairs


def _top_values(s, k, out_ref):
    n = s.shape[0] // F32_SUBLANES
    r = [s[i * F32_SUBLANES:(i + 1) * F32_SUBLANES, :] for i in range(n)]
    for a, b in _merge_exchange_pairs(n):
        r[a], r[b] = jnp.maximum(r[a], r[b]), jnp.minimum(r[a], r[b])
    prev = mx = None
    for t in range(k):
        prev = mx
        mx = jnp.max(r[0], axis=0, keepdims=True)
        if out_ref is not None:
            out_ref[t:t + 1, :] = mx
        pop = r[0] == mx
        for i in range(min(n, k - 1 - t)):
            r[i] = jnp.where(pop, r[i + 1] if i + 1 < n else -jnp.inf, r[i])
    return prev, mx


def _candidate_sums(v1_ref, v2_ref, cand_ref):
    cand_ref[0:TOP_ROWS, :] = v1_ref[0:1, :] + v2_ref[...]
    for a in range(1, 8):
        cand_ref[16 + 8 * a:24 + 8 * a, :] = v1_ref[a:a + 1, :] + v2_ref[0:8, :]
    cand_ref[80:96, :] = v1_ref[8:TOP_ROWS, :] + v2_ref[0:1, :]


def _peer_score_kernel(h_ref, wq_ref, k1_ref, k2_ref, th_ref, e1_ref, e2_ref,
                       v1_scr, v2_scr, cand_scr, *, n_i):
    qp = jnp.dot(h_ref[...], wq_ref[...], preferred_element_type=F32)
    pad = jnp.full((TOP_ROWS - N_TOP, v1_scr.shape[1]), -jnp.inf, F32)
    v1_scr[N_TOP:TOP_ROWS, :] = pad
    v2_scr[N_TOP:TOP_ROWS, :] = pad
    for h in range(PEER_HEADS):
        q1 = qp[:, (2 * h) * N_KEYS:(2 * h + 1) * N_KEYS]
        q2 = qp[:, (2 * h + 1) * N_KEYS:(2 * h + 2) * N_KEYS]
        s1 = lax.dot_general(k1_ref[h], q1, _NT, preferred_element_type=F32) * LOG2_E
        s2 = lax.dot_general(k2_ref[h], q2, _NT, preferred_element_type=F32) * LOG2_E
        _top_values(s1, N_TOP, v1_scr)
        _top_values(s2, N_TOP, v2_scr)
        _candidate_sums(v1_scr, v2_scr, cand_scr)
        cand = cand_scr[...]
        c16, c17 = _top_values(cand, N_TOP, None)
        tau = 0.5 * (c16 + c17)
        top1 = v1_scr[0:1, :]
        top2 = v2_scr[0:1, :]
        z = jnp.sum(jnp.where(cand >= c16, jnp.exp2(cand - (top1 + top2)), 0.0), axis=0, keepdims=True)
        theta = jnp.exp2(tau - s1 - top2)
        e1 = jnp.exp2(s1 - (top1 + jnp.log2(z)))
        for g in range(N_KEYS // n_i):
            th_ref[g, h * n_i:(h + 1) * n_i, :] = theta[g * n_i:(g + 1) * n_i, :]
            e1_ref[g, h * n_i:(h + 1) * n_i, :] = e1[g * n_i:(g + 1) * n_i, :]
        e2_ref[h] = jnp.exp2(s2 - top2)


def _peer_scores(h2, wq, k1, k2, *, rows, n_i, tm=256):
    d = h2.shape[1]
    n_grp = N_KEYS // n_i
    grouped = pl.BlockSpec((n_grp, PEER_HEADS * n_i, tm), lambda i: (0, 0, i))
    by_head = pl.BlockSpec((PEER_HEADS, N_KEYS, tm), lambda i: (0, 0, i))
    grouped_shape = jax.ShapeDtypeStruct((n_grp, PEER_HEADS * n_i, rows), F32)
    by_head_shape = jax.ShapeDtypeStruct((PEER_HEADS, N_KEYS, rows), F32)
    return pl.pallas_call(
        functools.partial(_peer_score_kernel, n_i=n_i),
        grid=(rows // tm,),
        in_specs=[pl.BlockSpec((tm, d), lambda i: (i, 0)),
                  pl.BlockSpec(wq.shape, lambda i: (0, 0)),
                  pl.BlockSpec(k1.shape, lambda i: (0, 0, 0)),
                  pl.BlockSpec(k2.shape, lambda i: (0, 0, 0))],
        out_specs=[grouped, grouped, by_head],
        out_shape=[grouped_shape, grouped_shape, by_head_shape],
        scratch_shapes=[pltpu.VMEM((TOP_ROWS, tm), F32), pltpu.VMEM((TOP_ROWS, tm), F32),
                        pltpu.VMEM((N_CAND, tm), F32)],
        compiler_params=_cparams("parallel"),
        name="peer_scores",
    )(h2, wq, k1, k2)


def _gelu(x):
    return (0.5 * x) * (1.0 + lax.erf(x * (0.5 ** 0.5)))


def _peer_mix_kernel(ht_ref, u_ref, vt_ref, th_ref, e1_ref, e2_ref, x_ref, mod_ref, *rest,
                     tt, te, n_lat, seq, n_batch, k_gate, final_norm):
    if final_norm:
        fg_ref, o_ref, acc_ref, a_scr, g_scr, w_scr = rest
    else:
        o_ref, acc_ref, a_scr, g_scr, w_scr = rest
    j = pl.program_id(1)
    d = x_ref.shape[1]
    n_i = te // N_KEYS

    @pl.when(j == 0)
    def _():
        acc_ref[...] = jnp.zeros_like(acc_ref)

    a_scr[...] = jnp.dot(u_ref[...], ht_ref[...], preferred_element_type=F32)

    tails = []
    for ii in range(n_i):
        for tc in range(tt // LANES):
            cs = slice(tc * LANES, (tc + 1) * LANES)
            for r in range(N_KEYS // F32_SUBLANES):
                ks = slice(r * F32_SUBLANES, (r + 1) * F32_SUBLANES)
                gate = tails[-GATE_CHAINS] * 0.0 if len(tails) >= GATE_CHAINS else None
                for h in range(PEER_HEADS):
                    row = slice(h * n_i + ii, h * n_i + ii + 1)
                    e2 = e2_ref[h, ks, cs]
                    wgt = jnp.where(e2 >= th_ref[0, row, cs], e2, 0.0) * e1_ref[0, row, cs]
                    gate = wgt if gate is None else gate + wgt
                tails.append(gate)
                g_scr[ii * N_KEYS + r * F32_SUBLANES:ii * N_KEYS + (r + 1) * F32_SUBLANES, cs] = gate

    w_scr[...] = (g_scr[...] * _gelu(a_scr[...])).astype(w_scr.dtype)
    acc_ref[...] += jnp.dot(vt_ref[...], w_scr[...], preferred_element_type=F32)

    @pl.when(j == pl.num_programs(1) - 1)
    def _():
        b = _mod_row(pl.program_id(0) * tt, n_lat, seq, n_batch)
        xn = x_ref[...] + _mod_vec(mod_ref, b, k_gate, d) * acc_ref[...].T
        if final_norm:
            xn = xn * lax.rsqrt(jnp.mean(xn * xn, axis=-1, keepdims=True) + EPS) * fg_ref[...]
        o_ref[...] = xn


PEER_TE = 512
GATE_CHAINS = 2


def _peer_mix(h2t, u, vt, th, e1, e2, x, mod, final_g, *, rows, n_lat, seq, n_batch, k_gate,
              tt=512, te=PEER_TE):
    d = x.shape[1]
    n_tiles = u.shape[0] // te
    final_norm = final_g is not None
    n_i = te // N_KEYS
    grouped = pl.BlockSpec((1, PEER_HEADS * n_i, tt), lambda t, j: (j, 0, t))
    by_head = pl.BlockSpec((PEER_HEADS, N_KEYS, tt), lambda t, j: (0, 0, t))
    in_specs = [pl.BlockSpec((d, tt), lambda t, j: (0, t)),
                pl.BlockSpec((te, d), lambda t, j: (j, 0)),
                pl.BlockSpec((d, te), lambda t, j: (0, j)),
                grouped, grouped, by_head,
                pl.BlockSpec((tt, d), lambda t, j: (t, 0)),
                pl.BlockSpec(mod.shape, lambda t, j: (0, 0))]
    args = [h2t, u, vt, th, e1, e2, x, mod]
    if final_norm:
        in_specs.append(pl.BlockSpec((1, d), lambda t, j: (0, 0)))
        args.append(final_g.reshape(1, d))
    return pl.pallas_call(
        functools.partial(_peer_mix_kernel, tt=tt, te=te, n_lat=n_lat, seq=seq, n_batch=n_batch,
                          k_gate=k_gate, final_norm=final_norm),
        grid=(rows // tt, n_tiles),
        in_specs=in_specs,
        out_specs=pl.BlockSpec((tt, d), lambda t, j: (t, 0)),
        out_shape=jax.ShapeDtypeStruct((rows, d), F32),
        scratch_shapes=[pltpu.VMEM((d, tt), F32), pltpu.VMEM((te, tt), F32), pltpu.VMEM((te, tt), F32),
                        pltpu.VMEM((te, tt), BF16)],
        compiler_params=_cparams("parallel", "arbitrary"),
        name="peer_mix",
    )(*args)


def _rope_tables(seq, pad_rows):
    half = HEAD_DIM // 2
    t = jnp.arange(seq)
    row = (t // GRID_W).astype(F32)
    col = (t % GRID_W).astype(F32)
    inv = ROPE_THETA ** (-jnp.arange(0, half, 2, dtype=F32) / half)
    ang_r = row[:, None] * inv[None, :]
    ang_c = col[:, None] * inv[None, :]
    cos = jnp.concatenate([jnp.cos(ang_r), jnp.cos(ang_r), jnp.cos(ang_c), jnp.cos(ang_c)], axis=-1)
    sin = jnp.concatenate([-jnp.sin(ang_r), jnp.sin(ang_r), -jnp.sin(ang_c), jnp.sin(ang_c)], axis=-1)
    cos = jnp.concatenate([cos, jnp.ones((pad_rows, HEAD_DIM), F32)], axis=0)
    sin = jnp.concatenate([sin, jnp.zeros((pad_rows, HEAD_DIM), F32)], axis=0)
    return cos, sin


def kernel(x, c, ctx, c_ctx, ada_w, ada_b, norm1_g, norm2_g, w_in, b_gate, q_norm_g, k_norm_g,
           conv_w, w_attn_out, w_conv_out, w_o, peer_wq, peer_k1, peer_k2, peer_u, peer_v, final_g):
    n_batch, seq, d = x.shape
    ctx_len = ctx.shape[1]
    depth = ada_w.shape[0]
    n_lat = n_batch * seq
    n_all = n_lat + n_batch * ctx_len
    assert seq & (seq - 1) == 0 and ctx_len & (ctx_len - 1) == 0
    tm = 1024
    assert seq % tm == 0 and (n_batch * ctx_len) % tm == 0 and seq % GRID_W == 0

    xa = x.reshape(n_lat, d)
    xc = ctx.reshape(n_batch * ctx_len, d)
    pad = (-(n_batch + 1)) % 8
    cvec = jnp.concatenate([c, c_ctx[None, :], jnp.zeros((pad, d), F32)], axis=0)
    mod_all = _adaln(cvec, ada_w, ada_b)
    cos_t, sin_t = _rope_tables(seq, tm)
    geom = dict(n_lat=n_lat, seq=seq, n_batch=n_batch)

    for l in range(depth):
        last = l == depth - 1
        rows = n_lat if last else n_all
        mod = mod_all[l]
        w_in_l = w_in[l].astype(BF16)

        h = _norm_mod(xa, norm1_g[l], mod, rows=n_all, k_shift=0, k_scale=1, x_ctx=xc, **geom)
        q, k, v, z, gb, gates = _in_proj(h, w_in_l, q_norm_g[l], k_norm_g[l], b_gate[l], cos_t, sin_t,
                                         rows_q=rows, rows_kv=n_all, n_lat=n_lat, seq=seq, d=d, tm=tm)
        attn = _attention(q, k, v, rows_q=rows, n_lat=n_lat, seq=seq, ctx_len=ctx_len)
        merged = _merge(attn, z, gb, gates, conv_w[l], w_attn_out[l].astype(BF16),
                        w_conv_out[l].astype(BF16), rows=rows, n_lat=n_lat, seq=seq, ctx_len=ctx_len)
        xa = _resid_proj(merged, w_o[l].astype(BF16), xa, mod, rows=rows, k_gate=2, x_ctx=xc, **geom)
        xc = None

        h2, h2t = _norm_mod(xa, norm2_g[l], mod, rows=rows, k_shift=3, k_scale=4, with_transpose=True, **geom)
        th, e1, e2 = _peer_scores(h2, peer_wq[l].astype(BF16), peer_k1[l], peer_k2[l], rows=rows,
                                  n_i=PEER_TE // N_KEYS)
        xa = _peer_mix(h2t, peer_u[l].astype(BF16), peer_v[l].T.astype(BF16), th, e1, e2, xa, mod,
                       final_g if last else None, rows=rows, k_gate=5, **geom)
    return xa.reshape(n_batch, seq, d)
```

```python
import functools

import jax
import jax.numpy as jnp
from jax import lax
from jax.experimental import pallas as pl
from jax.experimental.pallas import tpu as pltpu

F32 = jnp.float32
BF16 = jnp.bfloat16

HEAD_DIM = 128
KV_GROUPS = 4
GRID_W = 64
ROPE_THETA = 10000.0
PEER_HEADS = 8
N_KEYS = 128
PEER_TOPK = 16
N_MOD = 6
EPS = 1e-6
LOG2_E = 1.4426950408889634
LANES = 128
BF16_SUBLANES = 16
VMEM_LIMIT_BYTES = 56 * 1024 * 1024

_NT = (((1,), (1,)), ((), ()))


def _cparams(*sem, flags=None):
    return pltpu.CompilerParams(dimension_semantics=sem, vmem_limit_bytes=VMEM_LIMIT_BYTES, flags=flags)


def _mod_row(row0, n_lat, seq, n_batch):
    return jnp.where(row0 < n_lat, row0 // seq, n_batch)


def _mod_vec(mod_ref, b, k, d):
    return mod_ref[pl.ds(b, 1), k * d:(k + 1) * d]


def _adaln_kernel(c_ref, w_ref, b_ref, o_ref):
    cv = c_ref[...]
    s = cv * jax.nn.sigmoid(cv)
    o_ref[0] = jnp.dot(s, w_ref[0], preferred_element_type=F32) + b_ref[0]


def _adaln(cvec, ada_w, ada_b, tn=1024):
    depth, d, n = ada_w.shape
    rows = cvec.shape[0]
    return pl.pallas_call(
        _adaln_kernel,
        grid=(depth, n // tn),
        in_specs=[pl.BlockSpec((rows, d), lambda l, j: (0, 0)),
                  pl.BlockSpec((1, d, tn), lambda l, j: (l, 0, j)),
                  pl.BlockSpec((1, 1, tn), lambda l, j: (l, 0, j))],
        out_specs=pl.BlockSpec((1, rows, tn), lambda l, j: (l, 0, j)),
        out_shape=jax.ShapeDtypeStruct((depth, rows, n), F32),
        compiler_params=_cparams("arbitrary", "arbitrary"),
        name="adaln",
    )(cvec, ada_w, ada_b.reshape(depth, 1, n))


def _tile_of(lat_ref, ctx_ref, row0, n_lat):
    if ctx_ref is None:
        return lat_ref[...]
    return jnp.where(row0 < n_lat, lat_ref[...], ctx_ref[...])


def _row_specs(x_ctx, block, n_lat, col_of=None):
    n_lat_tiles = n_lat // block[0]
    col = (lambda *j: 0) if col_of is None else col_of
    if x_ctx is None:
        return [pl.BlockSpec(block, lambda i, *j: (i, col(*j)))]
    return [pl.BlockSpec(block, lambda i, *j: (jnp.minimum(i, n_lat_tiles - 1), col(*j))),
            pl.BlockSpec(block, lambda i, *j: (jnp.maximum(i - n_lat_tiles, 0), col(*j)))]


def _norm_mod_kernel(*refs, split, tm, n_lat, seq, n_batch, k_shift, k_scale):
    x_ref, xc_ref = (refs[0], refs[1]) if split else (refs[0], None)
    g_ref, mod_ref, o_ref, *maybe_ot_ref = refs[2 if split else 1:]
    d = x_ref.shape[1]
    row0 = pl.program_id(0) * tm
    b = _mod_row(row0, n_lat, seq, n_batch)
    xf = _tile_of(x_ref, xc_ref, row0, n_lat)
    y = xf * lax.rsqrt(jnp.mean(xf * xf, axis=-1, keepdims=True) + EPS)
    h = y * g_ref[...]
    shift = _mod_vec(mod_ref, b, k_shift, d)
    scale = _mod_vec(mod_ref, b, k_scale, d)
    h = h * (1.0 + scale) + shift
    o_ref[...] = h.astype(o_ref.dtype)
    for ot_ref in maybe_ot_ref:
        ot_ref[...] = h.T.astype(ot_ref.dtype)


def _norm_mod(x, g, mod, *, rows, n_lat, seq, n_batch, k_shift, k_scale, x_ctx=None,
              with_transpose=False, tm=512):
    d = x.shape[1]
    kern = functools.partial(_norm_mod_kernel, split=x_ctx is not None, tm=tm, n_lat=n_lat, seq=seq,
                             n_batch=n_batch, k_shift=k_shift, k_scale=k_scale)
    xs = [x] if x_ctx is None else [x, x_ctx]
    out_specs = [pl.BlockSpec((tm, d), lambda i: (i, 0))]
    out_shape = [jax.ShapeDtypeStruct((rows, d), BF16)]
    if with_transpose:
        out_specs.append(pl.BlockSpec((d, tm), lambda i: (0, i)))
        out_shape.append(jax.ShapeDtypeStruct((d, rows), BF16))
    out = pl.pallas_call(
        kern,
        grid=(rows // tm,),
        in_specs=_row_specs(x_ctx, (tm, d), n_lat) + [pl.BlockSpec((1, d), lambda i: (0, 0)),
                                                         pl.BlockSpec(mod.shape, lambda i: (0, 0))],
        out_specs=out_specs,
        out_shape=out_shape,
        compiler_params=_cparams("parallel"),
        name="norm_mod",
    )(*xs, g.reshape(1, d), mod)
    return out if with_transpose else out[0]


def _heads_norm_rope(acc, ones_bd, perm_bd, g, cos, sin, out_scale):
    n_heads = acc.shape[1] // HEAD_DIM
    ss = jnp.dot((acc * acc).astype(BF16), ones_bd, preferred_element_type=F32)
    y = acc * lax.rsqrt(ss * (1.0 / HEAD_DIM) + EPS) * g
    partner = jnp.dot(y.astype(BF16), perm_bd, preferred_element_type=F32)
    r = y * jnp.tile(cos, (1, n_heads)) + partner * jnp.tile(sin, (1, n_heads))
    if out_scale != 1.0:
        r = r * out_scale
    return r


def _q_kernel(h_ref, w_ref, ones_ref, perm_ref, g_ref, cos_ref, sin_ref, o_ref, *, out_scale):
    acc = jnp.dot(h_ref[...], w_ref[...], preferred_element_type=F32)
    r = _heads_norm_rope(acc, ones_ref[...], perm_ref[...], g_ref[...], cos_ref[...], sin_ref[...], out_scale)
    o_ref[...] = r.astype(o_ref.dtype)


def _kv_kernel(h_ref, wk_ref, wv_ref, ones_ref, perm_ref, g_ref, cos_ref, sin_ref, k_ref, v_ref):
    h = h_ref[...]
    acc = jnp.dot(h, wk_ref[...], preferred_element_type=F32)
    r = _heads_norm_rope(acc, ones_ref[...], perm_ref[...], g_ref[...], cos_ref[...], sin_ref[...], 1.0)
    k_ref[...] = r.astype(k_ref.dtype)
    vacc = jnp.dot(h, wv_ref[...], preferred_element_type=F32).astype(v_ref.dtype)
    ones = jnp.ones((vacc.shape[0], HEAD_DIM), v_ref.dtype)
    for g in range(vacc.shape[1] // HEAD_DIM):
        v_ref[:, 2 * g * HEAD_DIM:(2 * g + 1) * HEAD_DIM] = vacc[:, g * HEAD_DIM:(g + 1) * HEAD_DIM]
        v_ref[:, (2 * g + 1) * HEAD_DIM:(2 * g + 2) * HEAD_DIM] = ones


def _conv_in_kernel(h_ref, wu_ref, wb_ref, wc_ref, z_ref, gb_ref):
    h = h_ref[...]
    u = jnp.dot(h, wu_ref[...], preferred_element_type=F32)
    gc = jnp.dot(h, wc_ref[...], preferred_element_type=F32)
    z_ref[...] = (gc * u).astype(z_ref.dtype)
    gb_ref[...] = jnp.dot(h, wb_ref[...], preferred_element_type=F32).astype(gb_ref.dtype)


def _gate_kernel(h_ref, w_ref, b_ref, o_ref):
    acc = jnp.dot(h_ref[...], w_ref[...], preferred_element_type=F32)
    o_ref[...] = (0.5 + 0.5 * jnp.tanh(0.5 * (acc + b_ref[...]))).astype(o_ref.dtype)


def _rope_block(i, tm, n_lat, seq):
    return jnp.where(i * tm < n_lat, (i * tm % seq) // tm, seq // tm)


def _in_proj(h, w_in, q_g, k_g, b_gate, cos_t, sin_t, *, rows_q, rows_kv, n_lat, seq, d,
             tm=512, tn=512):
    kd = h.shape[1]
    kvw = d // KV_GROUPS
    q_end, k_end, v_end = d, d + kvw, d + 2 * kvw
    u_end, b_end, c_end = v_end + d, v_end + 2 * d, v_end + 3 * d
    hspec = pl.BlockSpec((tm, kd), lambda i, j: (i, 0))
    hd_spec = pl.BlockSpec((1, tn), lambda i, j: (0, 0))
    bd_spec = pl.BlockSpec((tn, tn), lambda i, j: (0, 0))
    heads = tn // HEAD_DIM
    lane = jnp.arange(HEAD_DIM)
    partner = jnp.where(lane % (HEAD_DIM // 2) < HEAD_DIM // 4, lane + HEAD_DIM // 4, lane - HEAD_DIM // 4)
    perm = (lane[:, None] == partner[None, :]).astype(BF16)
    ones_bd = jnp.kron(jnp.eye(heads, dtype=BF16), jnp.ones((HEAD_DIM, HEAD_DIM), BF16))
    perm_bd = jnp.kron(jnp.eye(heads, dtype=BF16), perm)
    tab_spec = pl.BlockSpec((tm, HEAD_DIM), lambda i, j: (_rope_block(i, tm, n_lat, seq), 0))

    def wspec(col0):
        return pl.BlockSpec((kd, tn), lambda i, j: (0, col0 // tn + j))

    def ospec():
        return pl.BlockSpec((tm, tn), lambda i, j: (i, j))

    q = pl.pallas_call(
        functools.partial(_q_kernel, out_scale=HEAD_DIM ** -0.5 * LOG2_E),
        grid=(rows_q // tm, d // tn),
        in_specs=[hspec, wspec(0), bd_spec, bd_spec, hd_spec, tab_spec, tab_spec],
        out_specs=ospec(),
        out_shape=jax.ShapeDtypeStruct((rows_q, d), BF16),
        compiler_params=_cparams("parallel", "arbitrary"),
        name="proj_q",
    )(h, w_in, ones_bd, perm_bd, jnp.tile(q_g, heads).reshape(1, tn), cos_t, sin_t)

    assert kvw == tn
    k, v = pl.pallas_call(
        _kv_kernel,
        grid=(rows_kv // tm, 1),
        in_specs=[hspec, wspec(q_end), wspec(k_end), bd_spec, bd_spec, hd_spec, tab_spec, tab_spec],
        out_specs=[ospec(), pl.BlockSpec((tm, 2 * tn), lambda i, j: (i, j))],
        out_shape=[jax.ShapeDtypeStruct((rows_kv, kvw), BF16), jax.ShapeDtypeStruct((rows_kv, 2 * kvw), BF16)],
        compiler_params=_cparams("parallel", "arbitrary"),
        name="proj_kv",
    )(h, w_in, w_in, ones_bd, perm_bd, jnp.tile(k_g, heads).reshape(1, tn), cos_t, sin_t)

    z, gb = pl.pallas_call(
        _conv_in_kernel,
        grid=(rows_q // tm, d // tn),
        in_specs=[hspec, wspec(v_end), wspec(u_end), wspec(b_end)],
        out_specs=[ospec(), ospec()],
        out_shape=[jax.ShapeDtypeStruct((rows_q, d), BF16)] * 2,
        compiler_params=_cparams("parallel", "arbitrary"),
        name="proj_conv",
    )(h, w_in, w_in, w_in)

    gates = pl.pallas_call(
        _gate_kernel,
        grid=(rows_q // tm, 2 * d // tn),
        in_specs=[hspec, wspec(c_end), pl.BlockSpec((1, tn), lambda i, j: (0, j))],
        out_specs=ospec(),
        out_shape=jax.ShapeDtypeStruct((rows_q, 2 * d), BF16),
        compiler_params=_cparams("parallel", "arbitrary"),
        name="proj_gates",
    )(h, w_in, b_gate.reshape(1, 2 * d))
    return q, k, v, z, gb, gates


ATTN_KEY_CHUNK = 256


def _attn_kernel(q_ref, kl_ref, vl_ref, kc_ref, vc_ref, o_ref, *, tq, n_lat):
    is_lat = pl.program_id(1) * tq < n_lat
    q = jnp.concatenate([q_ref[:, hh * HEAD_DIM:(hh + 1) * HEAD_DIM] for hh in range(KV_GROUPS)], axis=0)

    def step(carry, k, v):
        s = lax.dot_general(q, k, _NT, preferred_element_type=F32)
        m_new = jnp.max(s, axis=-1, keepdims=True)
        pv = lambda m: jnp.dot(jnp.exp2(s - m).astype(v.dtype), v, preferred_element_type=F32)
        if carry is None:
            return m_new, pv(m_new)
        m_old, acc = carry
        m_new = jnp.maximum(m_old, m_new)
        return m_new, acc * jnp.exp2(m_old - m_new) + pv(m_new)

    def run(with_lat):
        carry = step(None, kc_ref[...], vc_ref[...])
        if with_lat:
            for c in range(kl_ref.shape[0] // ATTN_KEY_CHUNK):
                rows = slice(c * ATTN_KEY_CHUNK, (c + 1) * ATTN_KEY_CHUNK)
                carry = step(carry, kl_ref[rows, :], vl_ref[rows, :])
        acc = carry[1]
        out = (acc[:, :HEAD_DIM] / acc[:, HEAD_DIM:]).astype(o_ref.dtype)
        for hh in range(KV_GROUPS):
            o_ref[:, hh * HEAD_DIM:(hh + 1) * HEAD_DIM] = out[hh * tq:(hh + 1) * tq, :]

    @pl.when(is_lat)
    def _():
        run(True)

    @pl.when(jnp.logical_not(is_lat))
    def _():
        run(False)


def _attention(q, k, v, *, rows_q, n_lat, seq, ctx_len, tq=256):
    assert ctx_len % tq == 0 or tq % ctx_len == 0
    assert tq <= ctx_len
    gw = KV_GROUPS * HEAD_DIM
    n_kv = k.shape[1] // HEAD_DIM

    def batch(i):
        return jnp.where(i * tq < n_lat, i * tq // seq, (i * tq - n_lat) // ctx_len)

    qspec = pl.BlockSpec((tq, gw), lambda g, i: (i, g))

    def lat_spec(width):
        return pl.BlockSpec((seq, width), lambda g, i: (batch(i), g))

    def ctx_spec(width):
        return pl.BlockSpec((ctx_len, width), lambda g, i: (n_lat // ctx_len + batch(i), g))

    return pl.pallas_call(
        functools.partial(_attn_kernel, tq=tq, n_lat=n_lat),
        grid=(n_kv, rows_q // tq),
        in_specs=[qspec, lat_spec(HEAD_DIM), lat_spec(2 * HEAD_DIM), ctx_spec(HEAD_DIM), ctx_spec(2 * HEAD_DIM)],
        out_specs=qspec,
        out_shape=jax.ShapeDtypeStruct((rows_q, q.shape[1]), BF16),
        compiler_params=_cparams("parallel", "arbitrary"),
        name="attention",
    )(q, k, v, k, v)


def _merge_kernel(attn_ref, z_ref, zp_ref, zn_ref, gb_ref, ga_ref, gc_ref, cw_ref,
                  wao_ref, wco_ref, o_ref, y_scr, *, tm, n_lat, seq, ctx_len):
    @pl.when(pl.program_id(1) == 0)
    def _():
        z = z_ref[...].astype(F32)
        loc = lax.broadcasted_iota(jnp.int32, (tm, 1), 0)
        row = pl.program_id(0) * tm + loc
        seqlen = jnp.where(row < n_lat, seq, ctx_len)
        pos = row & (seqlen - 1)
        halo_prev = zp_ref[...].astype(F32)[BF16_SUBLANES - 1:BF16_SUBLANES, :]
        halo_next = zn_ref[...].astype(F32)[0:1, :]
        z_prev = jnp.where(loc == 0, halo_prev, pltpu.roll(z, 1, 0))
        z_prev = jnp.where(pos == 0, 0.0, z_prev)
        z_next = jnp.where(loc == tm - 1, halo_next, pltpu.roll(z, tm - 1, 0))
        z_next = jnp.where(pos == seqlen - 1, 0.0, z_next)
        cw = cw_ref[...]
        conv = cw[0:1, :] * z_prev + cw[1:2, :] * z + cw[2:3, :] * z_next
        y_scr[...] = (gb_ref[...].astype(F32) * conv).astype(y_scr.dtype)

    acc_a = jnp.dot(attn_ref[...], wao_ref[...], preferred_element_type=F32)
    acc_c = jnp.dot(y_scr[...], wco_ref[...], preferred_element_type=F32)
    o_ref[...] = (ga_ref[...].astype(F32) * acc_a + gc_ref[...].astype(F32) * acc_c).astype(o_ref.dtype)


def _merge(attn, z, gb, gates, conv_w, w_ao, w_co, *, rows, n_lat, seq, ctx_len, tm=512, tn=1024):
    d = attn.shape[1]
    hb = BF16_SUBLANES
    n_hblk = z.shape[0] // hb
    row_spec = pl.BlockSpec((tm, d), lambda i, j: (i, 0))
    prev_spec = pl.BlockSpec((hb, d), lambda i, j: (jnp.maximum(i * (tm // hb) - 1, 0), 0))
    next_spec = pl.BlockSpec((hb, d), lambda i, j: (jnp.minimum((i + 1) * (tm // hb), n_hblk - 1), 0))
    wspec = pl.BlockSpec((d, tn), lambda i, j: (0, j))
    return pl.pallas_call(
        functools.partial(_merge_kernel, tm=tm, n_lat=n_lat, seq=seq, ctx_len=ctx_len),
        grid=(rows // tm, d // tn),
        in_specs=[row_spec, row_spec, prev_spec, next_spec, row_spec,
                  pl.BlockSpec((tm, tn), lambda i, j: (i, j)),
                  pl.BlockSpec((tm, tn), lambda i, j: (i, d // tn + j)),
                  pl.BlockSpec(conv_w.shape, lambda i, j: (0, 0)),
                  wspec, wspec],
        out_specs=pl.BlockSpec((tm, tn), lambda i, j: (i, j)),
        out_shape=jax.ShapeDtypeStruct((rows, d), BF16),
        scratch_shapes=[pltpu.VMEM((tm, d), BF16)],
        compiler_params=_cparams("parallel", "arbitrary"),
        name="merge",
    )(attn, z, z, z, gb, gates, gates, conv_w, w_ao, w_co)


def _resid_proj_kernel(a_ref, w_ref, *refs, split, tm, tn, n_lat, seq, n_batch, k_gate, d):
    x_ref, xc_ref = (refs[0], refs[1]) if split else (refs[0], None)
    mod_ref, o_ref = refs[2 if split else 1:]
    row0 = pl.program_id(0) * tm
    b = _mod_row(row0, n_lat, seq, n_batch)
    col0 = pl.multiple_of(k_gate * d + pl.program_id(1) * tn, LANES)
    gate = mod_ref[pl.ds(b, 1), pl.ds(col0, tn)]
    acc = jnp.dot(a_ref[...], w_ref[...], preferred_element_type=F32)
    o_ref[...] = _tile_of(x_ref, xc_ref, row0, n_lat) + gate * acc


def _resid_proj(a, w, x, mod, *, rows, n_lat, seq, n_batch, k_gate, x_ctx=None, tm=1024, tn=1024):
    d = a.shape[1]
    xs = [x] if x_ctx is None else [x, x_ctx]
    return pl.pallas_call(
        functools.partial(_resid_proj_kernel, split=x_ctx is not None, tm=tm, tn=tn, n_lat=n_lat, seq=seq,
                          n_batch=n_batch, k_gate=k_gate, d=d),
        grid=(rows // tm, d // tn),
        in_specs=[pl.BlockSpec((tm, d), lambda i, j: (i, 0)),
                  pl.BlockSpec((d, tn), lambda i, j: (0, j))]
        + _row_specs(x_ctx, (tm, tn), n_lat, col_of=lambda j: j)
        + [pl.BlockSpec(mod.shape, lambda i, j: (0, 0))],
        out_specs=pl.BlockSpec((tm, tn), lambda i, j: (i, j)),
        out_shape=jax.ShapeDtypeStruct((rows, d), F32),
        compiler_params=_cparams("parallel", "arbitrary"),
        name="resid_proj",
    )(a, w, *xs, mod)


N_TOP = PEER_TOPK + 1
TOP_ROWS = 24
N_CAND = 96


F32_SUBLANES = 8


def _merge_exchange_pairs(n):
    pairs = []
    p = 1
    while p < n:
        k = p
        while k >= 1:
            for j in range(k % p, n - k, 2 * k):
                for i in range(min(k, n - j - k)):
                    if (i + j) // (2 * p) == (i + j + k) // (2 * p):
                        pairs.append((i + j, i + j + k))
            k //= 2
        p *= 2
    return pairs


def _top_values(s, k, out_ref):
    n = s.shape[0] // F32_SUBLANES
    r = [s[i * F32_SUBLANES:(i + 1) * F32_SUBLANES, :] for i in range(n)]
    for a, b in _merge_exchange_pairs(n):
        r[a], r[b] = jnp.maximum(r[a], r[b]), jnp.minimum(r[a], r[b])
    prev = mx = None
    for t in range(k):
        prev = mx
        mx = jnp.max(r[0], axis=0, keepdims=True)
        if out_ref is not None:
            out_ref[t:t + 1, :] = mx
        pop = r[0] == mx
        for i in range(min(n, k - 1 - t)):
            r[i] = jnp.where(pop, r[i + 1] if i + 1 < n else -jnp.inf, r[i])
    return prev, mx


def _candidate_sums(v1_ref, v2_ref, cand_ref):
    cand_ref[0:TOP_ROWS, :] = v1_ref[0:1, :] + v2_ref[...]
    for a in range(1, 8):
        cand_ref[16 + 8 * a:24 + 8 * a, :] = v1_ref[a:a + 1, :] + v2_ref[0:8, :]
    cand_ref[80:96, :] = v1_ref[8:TOP_ROWS, :] + v2_ref[0:1, :]


def _peer_score_kernel(h_ref, wq_ref, k1_ref, k2_ref, th_ref, e1_ref, e2_ref,
                       v1_scr, v2_scr, cand_scr, *, n_i):
    qp = jnp.dot(h_ref[...], wq_ref[...], preferred_element_type=F32)
    pad = jnp.full((TOP_ROWS - N_TOP, v1_scr.shape[1]), -jnp.inf, F32)
    v1_scr[N_TOP:TOP_ROWS, :] = pad
    v2_scr[N_TOP:TOP_ROWS, :] = pad
    for h in range(PEER_HEADS):
        q1 = qp[:, (2 * h) * N_KEYS:(2 * h + 1) * N_KEYS]
        q2 = qp[:, (2 * h + 1) * N_KEYS:(2 * h + 2) * N_KEYS]
        s1 = lax.dot_general(k1_ref[h], q1, _NT, preferred_element_type=F32) * LOG2_E
        s2 = lax.dot_general(k2_ref[h], q2, _NT, preferred_element_type=F32) * LOG2_E
        _top_values(s1, N_TOP, v1_scr)
        _top_values(s2, N_TOP, v2_scr)
        _candidate_sums(v1_scr, v2_scr, cand_scr)
        cand = cand_scr[...]
        c16, c17 = _top_values(cand, N_TOP, None)
        tau = 0.5 * (c16 + c17)
        top1 = v1_scr[0:1, :]
        top2 = v2_scr[0:1, :]
        z = jnp.sum(jnp.where(cand >= c16, jnp.exp2(cand - (top1 + top2)), 0.0), axis=0, keepdims=True)
        theta = jnp.exp2(tau - s1 - top2)
        e1 = jnp.exp2(s1 - (top1 + jnp.log2(z) + 1.0))
        for g in range(N_KEYS // n_i):
            th_ref[g, h * n_i:(h + 1) * n_i, :] = theta[g * n_i:(g + 1) * n_i, :]
            e1_ref[g, h * n_i:(h + 1) * n_i, :] = e1[g * n_i:(g + 1) * n_i, :]
        e2_ref[h] = jnp.exp2(s2 - top2)


def _peer_scores(h2, wq, k1, k2, *, rows, n_i, tm=256):
    d = h2.shape[1]
    n_grp = N_KEYS // n_i
    grouped = pl.BlockSpec((n_grp, PEER_HEADS * n_i, tm), lambda i: (0, 0, i))
    by_head = pl.BlockSpec((PEER_HEADS, N_KEYS, tm), lambda i: (0, 0, i))
    grouped_shape = jax.ShapeDtypeStruct((n_grp, PEER_HEADS * n_i, rows), F32)
    by_head_shape = jax.ShapeDtypeStruct((PEER_HEADS, N_KEYS, rows), F32)
    return pl.pallas_call(
        functools.partial(_peer_score_kernel, n_i=n_i),
        grid=(rows // tm,),
        in_specs=[pl.BlockSpec((tm, d), lambda i: (i, 0)),
                  pl.BlockSpec(wq.shape, lambda i: (0, 0)),
                  pl.BlockSpec(k1.shape, lambda i: (0, 0, 0)),
                  pl.BlockSpec(k2.shape, lambda i: (0, 0, 0))],
        out_specs=[grouped, grouped, by_head],
        out_shape=[grouped_shape, grouped_shape, by_head_shape],
        scratch_shapes=[pltpu.VMEM((TOP_ROWS, tm), F32), pltpu.VMEM((TOP_ROWS, tm), F32),
                        pltpu.VMEM((N_CAND, tm), F32)],
        compiler_params=_cparams("parallel"),
        name="peer_scores",
    )(h2, wq, k1, k2)


def _twice_gelu(x):
    return x * (1.0 + lax.erf(x * (0.5 ** 0.5)))


def _peer_mix_kernel(ht_ref, u_ref, vt_ref, th_ref, e1_ref, e2_ref, x_ref, mod_ref, *rest,
                     tt, te, n_lat, seq, n_batch, k_gate, final_norm):
    if final_norm:
        fg_ref, o_ref, acc_ref, a_scr, g_scr, w_scr = rest
    else:
        o_ref, acc_ref, a_scr, g_scr, w_scr = rest
    j = pl.program_id(1)
    d = x_ref.shape[1]
    n_i = te // N_KEYS

    @pl.when(j == 0)
    def _():
        acc_ref[...] = jnp.zeros_like(acc_ref)

    a_scr[...] = jnp.dot(u_ref[...], ht_ref[...], preferred_element_type=F32)

    tails = []
    for ii in range(n_i):
        for tc in range(tt // LANES):
            cs = slice(tc * LANES, (tc + 1) * LANES)
            for r in range(N_KEYS // F32_SUBLANES):
                ks = slice(r * F32_SUBLANES, (r + 1) * F32_SUBLANES)
                gate = tails[-GATE_CHAINS] * 0.0 if len(tails) >= GATE_CHAINS else None
                for h in range(PEER_HEADS):
                    row = slice(h * n_i + ii, h * n_i + ii + 1)
                    e2 = e2_ref[h, ks, cs]
                    wgt = jnp.where(e2 >= th_ref[0, row, cs], e2, 0.0) * e1_ref[0, row, cs]
                    gate = wgt if gate is None else gate + wgt
                tails.append(gate)
                g_scr[ii * N_KEYS + r * F32_SUBLANES:ii * N_KEYS + (r + 1) * F32_SUBLANES, cs] = gate

    w_scr[...] = (g_scr[...] * _twice_gelu(a_scr[...])).astype(w_scr.dtype)
    acc_ref[...] += jnp.dot(vt_ref[...], w_scr[...], preferred_element_type=F32)

    @pl.when(j == pl.num_programs(1) - 1)
    def _():
        b = _mod_row(pl.program_id(0) * tt, n_lat, seq, n_batch)
        xn = x_ref[...] + _mod_vec(mod_ref, b, k_gate, d) * acc_ref[...].T
        if final_norm:
            xn = xn * lax.rsqrt(jnp.mean(xn * xn, axis=-1, keepdims=True) + EPS) * fg_ref[...]
        o_ref[...] = xn


PEER_TE = 512
GATE_CHAINS = 2


def _peer_mix(h2t, u, vt, th, e1, e2, x, mod, final_g, *, rows, n_lat, seq, n_batch, k_gate,
              tt=512, te=PEER_TE):
    d = x.shape[1]
    n_tiles = u.shape[0] // te
    final_norm = final_g is not None
    n_i = te // N_KEYS
    grouped = pl.BlockSpec((1, PEER_HEADS * n_i, tt), lambda t, j: (j, 0, t))
    by_head = pl.BlockSpec((PEER_HEADS, N_KEYS, tt), lambda t, j: (0, 0, t))
    in_specs = [pl.BlockSpec((d, tt), lambda t, j: (0, t)),
                pl.BlockSpec((te, d), lambda t, j: (j, 0)),
                pl.BlockSpec((d, te), lambda t, j: (0, j)),
                grouped, grouped, by_head,
                pl.BlockSpec((tt, d), lambda t, j: (t, 0)),
                pl.BlockSpec(mod.shape, lambda t, j: (0, 0))]
    args = [h2t, u, vt, th, e1, e2, x, mod]
    if final_norm:
        in_specs.append(pl.BlockSpec((1, d), lambda t, j: (0, 0)))
        args.append(final_g.reshape(1, d))
    return pl.pallas_call(
        functools.partial(_peer_mix_kernel, tt=tt, te=te, n_lat=n_lat, seq=seq, n_batch=n_batch,
                          k_gate=k_gate, final_norm=final_norm),
        grid=(rows // tt, n_tiles),
        in_specs=in_specs,
        out_specs=pl.BlockSpec((tt, d), lambda t, j: (t, 0)),
        out_shape=jax.ShapeDtypeStruct((rows, d), F32),
        scratch_shapes=[pltpu.VMEM((d, tt), F32), pltpu.VMEM((te, tt), F32), pltpu.VMEM((te, tt), F32),
                        pltpu.VMEM((te, tt), BF16)],
        compiler_params=_cparams("parallel", "arbitrary"),
        name="peer_mix",
    )(*args)


def _rope_tables(seq, pad_rows):
    half = HEAD_DIM // 2
    t = jnp.arange(seq)
    row = (t // GRID_W).astype(F32)
    col = (t % GRID_W).astype(F32)
    inv = ROPE_THETA ** (-jnp.arange(0, half, 2, dtype=F32) / half)
    ang_r = row[:, None] * inv[None, :]
    ang_c = col[:, None] * inv[None, :]
    cos = jnp.concatenate([jnp.cos(ang_r), jnp.cos(ang_r), jnp.cos(ang_c), jnp.cos(ang_c)], axis=-1)
    sin = jnp.concatenate([-jnp.sin(ang_r), jnp.sin(ang_r), -jnp.sin(ang_c), jnp.sin(ang_c)], axis=-1)
    cos = jnp.concatenate([cos, jnp.ones((pad_rows, HEAD_DIM), F32)], axis=0)
    sin = jnp.concatenate([sin, jnp.zeros((pad_rows, HEAD_DIM), F32)], axis=0)
    return cos, sin


def kernel(x, c, ctx, c_ctx, ada_w, ada_b, norm1_g, norm2_g, w_in, b_gate, q_norm_g, k_norm_g,
           conv_w, w_attn_out, w_conv_out, w_o, peer_wq, peer_k1, peer_k2, peer_u, peer_v, final_g):
    n_batch, seq, d = x.shape
    ctx_len = ctx.shape[1]
    depth = ada_w.shape[0]
    n_lat = n_batch * seq
    n_all = n_lat + n_batch * ctx_len
    assert seq & (seq - 1) == 0 and ctx_len & (ctx_len - 1) == 0
    tm = 1024
    assert seq % tm == 0 and (n_batch * ctx_len) % tm == 0 and seq % GRID_W == 0

    xa = x.reshape(n_lat, d)
    xc = ctx.reshape(n_batch * ctx_len, d)
    pad = (-(n_batch + 1)) % 8
    cvec = jnp.concatenate([c, c_ctx[None, :], jnp.zeros((pad, d), F32)], axis=0)
    mod_all = _adaln(cvec, ada_w, ada_b)
    cos_t, sin_t = _rope_tables(seq, tm)
    geom = dict(n_lat=n_lat, seq=seq, n_batch=n_batch)

    for l in range(depth):
        last = l == depth - 1
        rows = n_lat if last else n_all
        mod = mod_all[l]
        w_in_l = w_in[l].astype(BF16)

        h = _norm_mod(xa, norm1_g[l], mod, rows=n_all, k_shift=0, k_scale=1, x_ctx=xc, **geom)
        q, k, v, z, gb, gates = _in_proj(h, w_in_l, q_norm_g[l], k_norm_g[l], b_gate[l], cos_t, sin_t,
                                         rows_q=rows, rows_kv=n_all, n_lat=n_lat, seq=seq, d=d, tm=tm)
        attn = _attention(q, k, v, rows_q=rows, n_lat=n_lat, seq=seq, ctx_len=ctx_len)
        merged = _merge(attn, z, gb, gates, conv_w[l], w_attn_out[l].astype(BF16),
                        w_conv_out[l].astype(BF16), rows=rows, n_lat=n_lat, seq=seq, ctx_len=ctx_len)
        xa = _resid_proj(merged, w_o[l].astype(BF16), xa, mod, rows=rows, k_gate=2, x_ctx=xc, **geom)
        xc = None

        h2, h2t = _norm_mod(xa, norm2_g[l], mod, rows=rows, k_shift=3, k_scale=4, with_transpose=True, **geom)
        th, e1, e2 = _peer_scores(h2, peer_wq[l].astype(BF16), peer_k1[l], peer_k2[l], rows=rows,
                                  n_i=PEER_TE // N_KEYS)
        xa = _peer_mix(h2t, peer_u[l].astype(BF16), peer_v[l].T.astype(BF16), th, e1, e2, xa, mod,
                       final_g if last else None, rows=rows, k_gate=5, **geom)
    return xa.reshape(n_batch, seq, d)
```

```python
import functools

import jax
import jax.numpy as jnp
from jax import lax
from jax.experimental import pallas as pl
from jax.experimental.pallas import tpu as pltpu

F32 = jnp.float32
BF16 = jnp.bfloat16

HEAD_DIM = 128
KV_GROUPS = 4
GRID_W = 64
ROPE_THETA = 10000.0
PEER_HEADS = 8
N_KEYS = 128
PEER_TOPK = 16
N_MOD = 6
EPS = 1e-6
LOG2_E = 1.4426950408889634
LANES = 128
BF16_SUBLANES = 16
VMEM_LIMIT_BYTES = 56 * 1024 * 1024

_NT = (((1,), (1,)), ((), ()))


def _cparams(*sem, flags=None):
    return pltpu.CompilerParams(dimension_semantics=sem, vmem_limit_bytes=VMEM_LIMIT_BYTES, flags=flags)


def _mod_row(row0, n_lat, seq, n_batch):
    return jnp.where(row0 < n_lat, row0 // seq, n_batch)


def _mod_vec(mod_ref, b, k, d):
    return mod_ref[pl.ds(b, 1), k * d:(k + 1) * d]


def _adaln_kernel(c_ref, w_ref, b_ref, o_ref):
    cv = c_ref[...]
    s = cv * jax.nn.sigmoid(cv)
    o_ref[0] = jnp.dot(s, w_ref[0], preferred_element_type=F32) + b_ref[0]


def _adaln(cvec, ada_w, ada_b, tn=1024):
    depth, d, n = ada_w.shape
    rows = cvec.shape[0]
    return pl.pallas_call(
        _adaln_kernel,
        grid=(depth, n // tn),
        in_specs=[pl.BlockSpec((rows, d), lambda l, j: (0, 0)),
                  pl.BlockSpec((1, d, tn), lambda l, j: (l, 0, j)),
                  pl.BlockSpec((1, 1, tn), lambda l, j: (l, 0, j))],
        out_specs=pl.BlockSpec((1, rows, tn), lambda l, j: (l, 0, j)),
        out_shape=jax.ShapeDtypeStruct((depth, rows, n), F32),
        compiler_params=_cparams("arbitrary", "arbitrary"),
        name="adaln",
    )(cvec, ada_w, ada_b.reshape(depth, 1, n))


def _tile_of(lat_ref, ctx_ref, row0, n_lat):
    if ctx_ref is None:
        return lat_ref[...]
    return jnp.where(row0 < n_lat, lat_ref[...], ctx_ref[...])


def _row_specs(x_ctx, block, n_lat, col_of=None):
    n_lat_tiles = n_lat // block[0]
    col = (lambda *j: 0) if col_of is None else col_of
    if x_ctx is None:
        return [pl.BlockSpec(block, lambda i, *j: (i, col(*j)))]
    return [pl.BlockSpec(block, lambda i, *j: (jnp.minimum(i, n_lat_tiles - 1), col(*j))),
            pl.BlockSpec(block, lambda i, *j: (jnp.maximum(i - n_lat_tiles, 0), col(*j)))]


def _norm_mod_kernel(*refs, split, tm, n_lat, seq, n_batch, k_shift, k_scale):
    x_ref, xc_ref = (refs[0], refs[1]) if split else (refs[0], None)
    g_ref, mod_ref, o_ref, *maybe_ot_ref = refs[2 if split else 1:]
    d = x_ref.shape[1]
    row0 = pl.program_id(0) * tm
    b = _mod_row(row0, n_lat, seq, n_batch)
    xf = _tile_of(x_ref, xc_ref, row0, n_lat)
    y = xf * lax.rsqrt(jnp.mean(xf * xf, axis=-1, keepdims=True) + EPS)
    h = y * g_ref[...]
    shift = _mod_vec(mod_ref, b, k_shift, d)
    scale = _mod_vec(mod_ref, b, k_scale, d)
    h = h * (1.0 + scale) + shift
    o_ref[...] = h.astype(o_ref.dtype)
    for ot_ref in maybe_ot_ref:
        ot_ref[...] = h.T.astype(ot_ref.dtype)


def _norm_mod(x, g, mod, *, rows, n_lat, seq, n_batch, k_shift, k_scale, x_ctx=None,
              with_transpose=False, tm=512):
    d = x.shape[1]
    kern = functools.partial(_norm_mod_kernel, split=x_ctx is not None, tm=tm, n_lat=n_lat, seq=seq,
                             n_batch=n_batch, k_shift=k_shift, k_scale=k_scale)
    xs = [x] if x_ctx is None else [x, x_ctx]
    out_specs = [pl.BlockSpec((tm, d), lambda i: (i, 0))]
    out_shape = [jax.ShapeDtypeStruct((rows, d), BF16)]
    if with_transpose:
        out_specs.append(pl.BlockSpec((d, tm), lambda i: (0, i)))
        out_shape.append(jax.ShapeDtypeStruct((d, rows), BF16))
    out = pl.pallas_call(
        kern,
        grid=(rows // tm,),
        in_specs=_row_specs(x_ctx, (tm, d), n_lat) + [pl.BlockSpec((1, d), lambda i: (0, 0)),
                                                         pl.BlockSpec(mod.shape, lambda i: (0, 0))],
        out_specs=out_specs,
        out_shape=out_shape,
        compiler_params=_cparams("parallel"),
        name="norm_mod",
    )(*xs, g.reshape(1, d), mod)
    return out if with_transpose else out[0]


def _heads_norm_rope(acc, ones_bd, perm_bd, g, cos, sin, out_scale):
    n_heads = acc.shape[1] // HEAD_DIM
    ss = jnp.dot((acc * acc).astype(BF16), ones_bd, preferred_element_type=F32)
    y = acc * lax.rsqrt(ss * (1.0 / HEAD_DIM) + EPS) * g
    partner = jnp.dot(y.astype(BF16), perm_bd, preferred_element_type=F32)
    r = y * jnp.tile(cos, (1, n_heads)) + partner * jnp.tile(sin, (1, n_heads))
    if out_scale != 1.0:
        r = r * out_scale
    return r


def _q_kernel(h_ref, w_ref, ones_ref, perm_ref, g_ref, cos_ref, sin_ref, o_ref, *, out_scale):
    acc = jnp.dot(h_ref[...], w_ref[...], preferred_element_type=F32)
    r = _heads_norm_rope(acc, ones_ref[...], perm_ref[...], g_ref[...], cos_ref[...], sin_ref[...], out_scale)
    o_ref[...] = r.astype(o_ref.dtype)


def _kv_kernel(h_ref, wk_ref, wv_ref, ones_ref, perm_ref, g_ref, cos_ref, sin_ref, k_ref, v_ref):
    h = h_ref[...]
    acc = jnp.dot(h, wk_ref[...], preferred_element_type=F32)
    r = _heads_norm_rope(acc, ones_ref[...], perm_ref[...], g_ref[...], cos_ref[...], sin_ref[...], 1.0)
    k_ref[...] = r.astype(k_ref.dtype)
    vacc = jnp.dot(h, wv_ref[...], preferred_element_type=F32).astype(v_ref.dtype)
    ones = jnp.ones((vacc.shape[0], HEAD_DIM), v_ref.dtype)
    for g in range(vacc.shape[1] // HEAD_DIM):
        v_ref[:, 2 * g * HEAD_DIM:(2 * g + 1) * HEAD_DIM] = vacc[:, g * HEAD_DIM:(g + 1) * HEAD_DIM]
        v_ref[:, (2 * g + 1) * HEAD_DIM:(2 * g + 2) * HEAD_DIM] = ones


def _conv_in_kernel(h_ref, wu_ref, wb_ref, wc_ref, z_ref, gb_ref):
    h = h_ref[...]
    u = jnp.dot(h, wu_ref[...], preferred_element_type=F32)
    gc = jnp.dot(h, wc_ref[...], preferred_element_type=F32)
    z_ref[...] = (gc * u).astype(z_ref.dtype)
    gb_ref[...] = jnp.dot(h, wb_ref[...], preferred_element_type=F32).astype(gb_ref.dtype)


def _gate_kernel(h_ref, w_ref, b_ref, o_ref):
    acc = jnp.dot(h_ref[...], w_ref[...], preferred_element_type=F32)
    o_ref[...] = (0.5 + 0.5 * jnp.tanh(0.5 * (acc + b_ref[...]))).astype(o_ref.dtype)


def _rope_block(i, tm, n_lat, seq):
    return jnp.where(i * tm < n_lat, (i * tm % seq) // tm, seq // tm)


def _in_proj(h, w_in, q_g, k_g, b_gate, cos_t, sin_t, *, rows_q, rows_kv, n_lat, seq, d,
             tm=512, tn=512):
    kd = h.shape[1]
    kvw = d // KV_GROUPS
    q_end, k_end, v_end = d, d + kvw, d + 2 * kvw
    u_end, b_end, c_end = v_end + d, v_end + 2 * d, v_end + 3 * d
    hspec = pl.BlockSpec((tm, kd), lambda i, j: (i, 0))
    hd_spec = pl.BlockSpec((1, tn), lambda i, j: (0, 0))
    bd_spec = pl.BlockSpec((tn, tn), lambda i, j: (0, 0))
    heads = tn // HEAD_DIM
    lane = jnp.arange(HEAD_DIM)
    partner = jnp.where(lane % (HEAD_DIM // 2) < HEAD_DIM // 4, lane + HEAD_DIM // 4, lane - HEAD_DIM // 4)
    perm = (lane[:, None] == partner[None, :]).astype(BF16)
    ones_bd = jnp.kron(jnp.eye(heads, dtype=BF16), jnp.ones((HEAD_DIM, HEAD_DIM), BF16))
    perm_bd = jnp.kron(jnp.eye(heads, dtype=BF16), perm)
    tab_spec = pl.BlockSpec((tm, HEAD_DIM), lambda i, j: (_rope_block(i, tm, n_lat, seq), 0))

    def wspec(col0):
        return pl.BlockSpec((kd, tn), lambda i, j: (0, col0 // tn + j))

    def ospec():
        return pl.BlockSpec((tm, tn), lambda i, j: (i, j))

    q = pl.pallas_call(
        functools.partial(_q_kernel, out_scale=HEAD_DIM ** -0.5 * LOG2_E),
        grid=(rows_q // tm, d // tn),
        in_specs=[hspec, wspec(0), bd_spec, bd_spec, hd_spec, tab_spec, tab_spec],
        out_specs=ospec(),
        out_shape=jax.ShapeDtypeStruct((rows_q, d), BF16),
        compiler_params=_cparams("parallel", "arbitrary"),
        name="proj_q",
    )(h, w_in, ones_bd, perm_bd, jnp.tile(q_g, heads).reshape(1, tn), cos_t, sin_t)

    assert kvw == tn
    k, v = pl.pallas_call(
        _kv_kernel,
        grid=(rows_kv // tm, 1),
        in_specs=[hspec, wspec(q_end), wspec(k_end), bd_spec, bd_spec, hd_spec, tab_spec, tab_spec],
        out_specs=[ospec(), pl.BlockSpec((tm, 2 * tn), lambda i, j: (i, j))],
        out_shape=[jax.ShapeDtypeStruct((rows_kv, kvw), BF16), jax.ShapeDtypeStruct((rows_kv, 2 * kvw), BF16)],
        compiler_params=_cparams("parallel", "arbitrary"),
        name="proj_kv",
    )(h, w_in, w_in, ones_bd, perm_bd, jnp.tile(k_g, heads).reshape(1, tn), cos_t, sin_t)

    z, gb = pl.pallas_call(
        _conv_in_kernel,
        grid=(rows_q // tm, d // tn),
        in_specs=[hspec, wspec(v_end), wspec(u_end), wspec(b_end)],
        out_specs=[ospec(), ospec()],
        out_shape=[jax.ShapeDtypeStruct((rows_q, d), BF16)] * 2,
        compiler_params=_cparams("parallel", "arbitrary"),
        name="proj_conv",
    )(h, w_in, w_in, w_in)

    gates = pl.pallas_call(
        _gate_kernel,
        grid=(rows_q // tm, 2 * d // tn),
        in_specs=[hspec, wspec(c_end), pl.BlockSpec((1, tn), lambda i, j: (0, j))],
        out_specs=ospec(),
        out_shape=jax.ShapeDtypeStruct((rows_q, 2 * d), BF16),
        compiler_params=_cparams("parallel", "arbitrary"),
        name="proj_gates",
    )(h, w_in, b_gate.reshape(1, 2 * d))
    return q, k, v, z, gb, gates


ATTN_KEY_CHUNK = 256


def _attn_kernel(q_ref, kl_ref, vl_ref, kc_ref, vc_ref, o_ref, *, tq, n_lat):
    is_lat = pl.program_id(1) * tq < n_lat
    q = jnp.concatenate([q_ref[:, hh * HEAD_DIM:(hh + 1) * HEAD_DIM] for hh in range(KV_GROUPS)], axis=0)

    def step(carry, k, v):
        s = lax.dot_general(q, k, _NT, preferred_element_type=F32)
        m_new = jnp.max(s, axis=-1, keepdims=True)
        pv = lambda m: jnp.dot(jnp.exp2(s - m).astype(v.dtype), v, preferred_element_type=F32)
        if carry is None:
            return m_new, pv(m_new)
        m_old, acc = carry
        m_new = jnp.maximum(m_old, m_new)
        return m_new, acc * jnp.exp2(m_old - m_new) + pv(m_new)

    def run(with_lat):
        carry = step(None, kc_ref[...], vc_ref[...])
        if with_lat:
            for c in range(kl_ref.shape[0] // ATTN_KEY_CHUNK):
                rows = slice(c * ATTN_KEY_CHUNK, (c + 1) * ATTN_KEY_CHUNK)
                carry = step(carry, kl_ref[rows, :], vl_ref[rows, :])
        acc = carry[1]
        out = (acc[:, :HEAD_DIM] / acc[:, HEAD_DIM:]).astype(o_ref.dtype)
        for hh in range(KV_GROUPS):
            o_ref[:, hh * HEAD_DIM:(hh + 1) * HEAD_DIM] = out[hh * tq:(hh + 1) * tq, :]

    @pl.when(is_lat)
    def _():
        run(True)

    @pl.when(jnp.logical_not(is_lat))
    def _():
        run(False)


def _attention(q, k, v, *, rows_q, n_lat, seq, ctx_len, tq=256):
    assert ctx_len % tq == 0 or tq % ctx_len == 0
    assert tq <= ctx_len
    gw = KV_GROUPS * HEAD_DIM
    n_kv = k.shape[1] // HEAD_DIM

    def batch(i):
        return jnp.where(i * tq < n_lat, i * tq // seq, (i * tq - n_lat) // ctx_len)

    qspec = pl.BlockSpec((tq, gw), lambda g, i: (i, g))

    def lat_spec(width):
        return pl.BlockSpec((seq, width), lambda g, i: (batch(i), g))

    def ctx_spec(width):
        return pl.BlockSpec((ctx_len, width), lambda g, i: (n_lat // ctx_len + batch(i), g))

    return pl.pallas_call(
        functools.partial(_attn_kernel, tq=tq, n_lat=n_lat),
        grid=(n_kv, rows_q // tq),
        in_specs=[qspec, lat_spec(HEAD_DIM), lat_spec(2 * HEAD_DIM), ctx_spec(HEAD_DIM), ctx_spec(2 * HEAD_DIM)],
        out_specs=qspec,
        out_shape=jax.ShapeDtypeStruct((rows_q, q.shape[1]), BF16),
        compiler_params=_cparams("parallel", "arbitrary"),
        name="attention",
    )(q, k, v, k, v)


def _merge_kernel(attn_ref, z_ref, zp_ref, zn_ref, gb_ref, ga_ref, gc_ref, cw_ref,
                  wao_ref, wco_ref, o_ref, y_scr, *, tm, n_lat, seq, ctx_len):
    @pl.when(pl.program_id(1) == 0)
    def _():
        z = z_ref[...].astype(F32)
        loc = lax.broadcasted_iota(jnp.int32, (tm, 1), 0)
        row = pl.program_id(0) * tm + loc
        seqlen = jnp.where(row < n_lat, seq, ctx_len)
        pos = row & (seqlen - 1)
        halo_prev = zp_ref[...].astype(F32)[BF16_SUBLANES - 1:BF16_SUBLANES, :]
        halo_next = zn_ref[...].astype(F32)[0:1, :]
        z_prev = jnp.where(loc == 0, halo_prev, pltpu.roll(z, 1, 0))
        z_prev = jnp.where(pos == 0, 0.0, z_prev)
        z_next = jnp.where(loc == tm - 1, halo_next, pltpu.roll(z, tm - 1, 0))
        z_next = jnp.where(pos == seqlen - 1, 0.0, z_next)
        cw = cw_ref[...]
        conv = cw[0:1, :] * z_prev + cw[1:2, :] * z + cw[2:3, :] * z_next
        y_scr[...] = (gb_ref[...].astype(F32) * conv).astype(y_scr.dtype)

    acc_a = jnp.dot(attn_ref[...], wao_ref[...], preferred_element_type=F32)
    acc_c = jnp.dot(y_scr[...], wco_ref[...], preferred_element_type=F32)
    o_ref[...] = (ga_ref[...].astype(F32) * acc_a + gc_ref[...].astype(F32) * acc_c).astype(o_ref.dtype)


def _merge(attn, z, gb, gates, conv_w, w_ao, w_co, *, rows, n_lat, seq, ctx_len, tm=512, tn=1024):
    d = attn.shape[1]
    hb = BF16_SUBLANES
    n_hblk = z.shape[0] // hb
    row_spec = pl.BlockSpec((tm, d), lambda i, j: (i, 0))
    prev_spec = pl.BlockSpec((hb, d), lambda i, j: (jnp.maximum(i * (tm // hb) - 1, 0), 0))
    next_spec = pl.BlockSpec((hb, d), lambda i, j: (jnp.minimum((i + 1) * (tm // hb), n_hblk - 1), 0))
    wspec = pl.BlockSpec((d, tn), lambda i, j: (0, j))
    return pl.pallas_call(
        functools.partial(_merge_kernel, tm=tm, n_lat=n_lat, seq=seq, ctx_len=ctx_len),
        grid=(rows // tm, d // tn),
        in_specs=[row_spec, row_spec, prev_spec, next_spec, row_spec,
                  pl.BlockSpec((tm, tn), lambda i, j: (i, j)),
                  pl.BlockSpec((tm, tn), lambda i, j: (i, d // tn + j)),
                  pl.BlockSpec(conv_w.shape, lambda i, j: (0, 0)),
                  wspec, wspec],
        out_specs=pl.BlockSpec((tm, tn), lambda i, j: (i, j)),
        out_shape=jax.ShapeDtypeStruct((rows, d), BF16),
        scratch_shapes=[pltpu.VMEM((tm, d), BF16)],
        compiler_params=_cparams("parallel", "arbitrary"),
        name="merge",
    )(attn, z, z, z, gb, gates, gates, conv_w, w_ao, w_co)


def _resid_proj_kernel(a_ref, w_ref, *refs, split, tm, tn, n_lat, seq, n_batch, k_gate, d):
    x_ref, xc_ref = (refs[0], refs[1]) if split else (refs[0], None)
    mod_ref, o_ref = refs[2 if split else 1:]
    row0 = pl.program_id(0) * tm
    b = _mod_row(row0, n_lat, seq, n_batch)
    col0 = pl.multiple_of(k_gate * d + pl.program_id(1) * tn, LANES)
    gate = mod_ref[pl.ds(b, 1), pl.ds(col0, tn)]
    acc = jnp.dot(a_ref[...], w_ref[...], preferred_element_type=F32)
    o_ref[...] = _tile_of(x_ref, xc_ref, row0, n_lat) + gate * acc


def _resid_proj(a, w, x, mod, *, rows, n_lat, seq, n_batch, k_gate, x_ctx=None, tm=1024, tn=1024):
    d = a.shape[1]
    xs = [x] if x_ctx is None else [x, x_ctx]
    return pl.pallas_call(
        functools.partial(_resid_proj_kernel, split=x_ctx is not None, tm=tm, tn=tn, n_lat=n_lat, seq=seq,
                          n_batch=n_batch, k_gate=k_gate, d=d),
        grid=(rows // tm, d // tn),
        in_specs=[pl.BlockSpec((tm, d), lambda i, j: (i, 0)),
                  pl.BlockSpec((d, tn), lambda i, j: (0, j))]
        + _row_specs(x_ctx, (tm, tn), n_lat, col_of=lambda j: j)
        + [pl.BlockSpec(mod.shape, lambda i, j: (0, 0))],
        out_specs=pl.BlockSpec((tm, tn), lambda i, j: (i, j)),
        out_shape=jax.ShapeDtypeStruct((rows, d), F32),
        compiler_params=_cparams("parallel", "arbitrary"),
        name="resid_proj",
    )(a, w, *xs, mod)


N_TOP = PEER_TOPK + 1
TOP_ROWS = 24
N_CAND = 96


F32_SUBLANES = 8


def _merge_exchange_pairs(n):
    pairs = []
    p = 1
    while p < n:
        k = p
        while k >= 1:
            for j in range(k % p, n - k, 2 * k):
                for i in range(min(k, n - j - k)):
                    if (i + j) // (2 * p) == (i + j + k) // (2 * p):
                        pairs.append((i + j, i + j + k))
            k //= 2
        p *= 2
    return pairs


def _top_values(s, k, out_ref):
    n = s.shape[0] // F32_SUBLANES
    r = [s[i * F32_SUBLANES:(i + 1) * F32_SUBLANES, :] for i in range(n)]
    for a, b in _merge_exchange_pairs(n):
        r[a], r[b] = jnp.maximum(r[a], r[b]), jnp.minimum(r[a], r[b])
    prev = mx = None
    for t in range(k):
        prev = mx
        mx = jnp.max(r[0], axis=0, keepdims=True)
        if out_ref is not None:
            out_ref[t:t + 1, :] = mx
        pop = r[0] == mx
        for i in range(min(n, k - 1 - t)):
            r[i] = jnp.where(pop, r[i + 1] if i + 1 < n else -jnp.inf, r[i])
    return prev, mx


def _candidate_sums(v1_ref, v2_ref, cand_ref):
    cand_ref[0:TOP_ROWS, :] = v1_ref[0:1, :] + v2_ref[...]
    for a in range(1, 8):
        cand_ref[16 + 8 * a:24 + 8 * a, :] = v1_ref[a:a + 1, :] + v2_ref[0:8, :]
    cand_ref[80:96, :] = v1_ref[8:TOP_ROWS, :] + v2_ref[0:1, :]


def _peer_score_kernel(h_ref, wq_ref, k1_ref, k2_ref, th_ref, e1_ref, e2_ref,
                       v1_scr, v2_scr, cand_scr, *, n_i):
    qp = jnp.dot(h_ref[...], wq_ref[...], preferred_element_type=F32)
    pad = jnp.full((TOP_ROWS - N_TOP, v1_scr.shape[1]), -jnp.inf, F32)
    v1_scr[N_TOP:TOP_ROWS, :] = pad
    v2_scr[N_TOP:TOP_ROWS, :] = pad
    for h in range(PEER_HEADS):
        q1 = qp[:, (2 * h) * N_KEYS:(2 * h + 1) * N_KEYS]
        q2 = qp[:, (2 * h + 1) * N_KEYS:(2 * h + 2) * N_KEYS]
        s1 = lax.dot_general(k1_ref[h], q1, _NT, preferred_element_type=F32) * LOG2_E
        s2 = lax.dot_general(k2_ref[h], q2, _NT, preferred_element_type=F32) * LOG2_E
        _top_values(s1, N_TOP, v1_scr)
        _top_values(s2, N_TOP, v2_scr)
        _candidate_sums(v1_scr, v2_scr, cand_scr)
        cand = cand_scr[...]
        c16, c17 = _top_values(cand, N_TOP, None)
        tau = 0.5 * (c16 + c17)
        top1 = v1_scr[0:1, :]
        top2 = v2_scr[0:1, :]
        z = jnp.sum(jnp.where(cand >= c16, jnp.exp2(cand - (top1 + top2)), 0.0), axis=0, keepdims=True)
        theta = jnp.exp2(tau - s1 - top2)
        e1 = jnp.exp2(s1 - (top1 + jnp.log2(z) + 1.0))
        for g in range(N_KEYS // n_i):
            th_ref[g, h * n_i:(h + 1) * n_i, :] = theta[g * n_i:(g + 1) * n_i, :]
            e1_ref[g, h * n_i:(h + 1) * n_i, :] = e1[g * n_i:(g + 1) * n_i, :]
        e2_ref[h] = jnp.exp2(s2 - top2)


def _peer_scores(h2, wq, k1, k2, *, rows, n_i, tm=256):
    d = h2.shape[1]
    n_grp = N_KEYS // n_i
    grouped = pl.BlockSpec((n_grp, PEER_HEADS * n_i, tm), lambda i: (0, 0, i))
    by_head = pl.BlockSpec((PEER_HEADS, N_KEYS, tm), lambda i: (0, 0, i))
    grouped_shape = jax.ShapeDtypeStruct((n_grp, PEER_HEADS * n_i, rows), F32)
    by_head_shape = jax.ShapeDtypeStruct((PEER_HEADS, N_KEYS, rows), F32)
    return pl.pallas_call(
        functools.partial(_peer_score_kernel, n_i=n_i),
        grid=(rows // tm,),
        in_specs=[pl.BlockSpec((tm, d), lambda i: (i, 0)),
                  pl.BlockSpec(wq.shape, lambda i: (0, 0)),
                  pl.BlockSpec(k1.shape, lambda i: (0, 0, 0)),
                  pl.BlockSpec(k2.shape, lambda i: (0, 0, 0))],
        out_specs=[grouped, grouped, by_head],
        out_shape=[grouped_shape, grouped_shape, by_head_shape],
        scratch_shapes=[pltpu.VMEM((TOP_ROWS, tm), F32), pltpu.VMEM((TOP_ROWS, tm), F32),
                        pltpu.VMEM((N_CAND, tm), F32)],
        compiler_params=_cparams("parallel"),
        name="peer_scores",
    )(h2, wq, k1, k2)


def _twice_gelu(x):
    return x * (1.0 + lax.erf(x * (0.5 ** 0.5)))


def _peer_mix_kernel(ht_ref, u_ref, vt_ref, th_ref, e1_ref, e2_ref, x_ref, mod_ref, *rest,
                     tt, te, n_lat, seq, n_batch, k_gate, final_norm):
    if final_norm:
        fg_ref, o_ref, acc_ref, a_scr, g_scr, w_scr = rest
    else:
        o_ref, acc_ref, a_scr, g_scr, w_scr = rest
    j = pl.program_id(1)
    d = x_ref.shape[1]
    n_i = te // N_KEYS

    @pl.when(j == 0)
    def _():
        acc_ref[...] = jnp.zeros_like(acc_ref)

    a_scr[...] = jnp.dot(u_ref[...], ht_ref[...], preferred_element_type=F32)

    tails = []
    for ii in range(n_i):
        for tc in range(tt // LANES):
            cs = slice(tc * LANES, (tc + 1) * LANES)
            for r in range(N_KEYS // F32_SUBLANES):
                ks = slice(r * F32_SUBLANES, (r + 1) * F32_SUBLANES)
                gate = tails[-GATE_CHAINS] * 0.0 if len(tails) >= GATE_CHAINS else None
                for h in range(PEER_HEADS):
                    row = slice(h * n_i + ii, h * n_i + ii + 1)
                    e2 = e2_ref[h, ks, cs]
                    wgt = jnp.where(e2 >= th_ref[0, row, cs], e2, 0.0) * e1_ref[0, row, cs]
                    gate = wgt if gate is None else gate + wgt
                tails.append(gate)
                g_scr[ii * N_KEYS + r * F32_SUBLANES:ii * N_KEYS + (r + 1) * F32_SUBLANES, cs] = gate

    w_scr[...] = (g_scr[...] * _twice_gelu(a_scr[...])).astype(w_scr.dtype)
    acc_ref[...] += jnp.dot(vt_ref[...], w_scr[...], preferred_element_type=F32)

    @pl.when(j == pl.num_programs(1) - 1)
    def _():
        b = _mod_row(pl.program_id(0) * tt, n_lat, seq, n_batch)
        xn = x_ref[...] + _mod_vec(mod_ref, b, k_gate, d) * acc_ref[...].T
        if final_norm:
            xn = xn * lax.rsqrt(jnp.mean(xn * xn, axis=-1, keepdims=True) + EPS) * fg_ref[...]
        o_ref[...] = xn


PEER_TE = 1024
GATE_CHAINS = 2


def _peer_mix(h2t, u, vt, th, e1, e2, x, mod, final_g, *, rows, n_lat, seq, n_batch, k_gate,
              tt=512, te=PEER_TE):
    d = x.shape[1]
    n_tiles = u.shape[0] // te
    final_norm = final_g is not None
    n_i = te // N_KEYS
    grouped = pl.BlockSpec((1, PEER_HEADS * n_i, tt), lambda t, j: (j, 0, t))
    once = pl.Buffered(1)
    by_head = pl.BlockSpec((PEER_HEADS, N_KEYS, tt), lambda t, j: (0, 0, t))
    in_specs = [pl.BlockSpec((d, tt), lambda t, j: (0, t)),
                pl.BlockSpec((te, d), lambda t, j: (j, 0)),
                pl.BlockSpec((d, te), lambda t, j: (0, j)),
                grouped, grouped, by_head,
                pl.BlockSpec((tt, d), lambda t, j: (t, 0), pipeline_mode=once),
                pl.BlockSpec(mod.shape, lambda t, j: (0, 0), pipeline_mode=once)]
    args = [h2t, u, vt, th, e1, e2, x, mod]
    if final_norm:
        in_specs.append(pl.BlockSpec((1, d), lambda t, j: (0, 0)))
        args.append(final_g.reshape(1, d))
    return pl.pallas_call(
        functools.partial(_peer_mix_kernel, tt=tt, te=te, n_lat=n_lat, seq=seq, n_batch=n_batch,
                          k_gate=k_gate, final_norm=final_norm),
        grid=(rows // tt, n_tiles),
        in_specs=in_specs,
        out_specs=pl.BlockSpec((tt, d), lambda t, j: (t, 0)),
        out_shape=jax.ShapeDtypeStruct((rows, d), F32),
        scratch_shapes=[pltpu.VMEM((d, tt), F32), pltpu.VMEM((te, tt), F32), pltpu.VMEM((te, tt), F32),
                        pltpu.VMEM((te, tt), BF16)],
        compiler_params=_cparams("parallel", "arbitrary"),
        name="peer_mix",
    )(*args)


def _rope_tables(seq, pad_rows):
    half = HEAD_DIM // 2
    t = jnp.arange(seq)
    row = (t // GRID_W).astype(F32)
    col = (t % GRID_W).astype(F32)
    inv = ROPE_THETA ** (-jnp.arange(0, half, 2, dtype=F32) / half)
    ang_r = row[:, None] * inv[None, :]
    ang_c = col[:, None] * inv[None, :]
    cos = jnp.concatenate([jnp.cos(ang_r), jnp.cos(ang_r), jnp.cos(ang_c), jnp.cos(ang_c)], axis=-1)
    sin = jnp.concatenate([-jnp.sin(ang_r), jnp.sin(ang_r), -jnp.sin(ang_c), jnp.sin(ang_c)], axis=-1)
    cos = jnp.concatenate([cos, jnp.ones((pad_rows, HEAD_DIM), F32)], axis=0)
    sin = jnp.concatenate([sin, jnp.zeros((pad_rows, HEAD_DIM), F32)], axis=0)
    return cos, sin


def kernel(x, c, ctx, c_ctx, ada_w, ada_b, norm1_g, norm2_g, w_in, b_gate, q_norm_g, k_norm_g,
           conv_w, w_attn_out, w_conv_out, w_o, peer_wq, peer_k1, peer_k2, peer_u, peer_v, final_g):
    n_batch, seq, d = x.shape
    ctx_len = ctx.shape[1]
    depth = ada_w.shape[0]
    n_lat = n_batch * seq
    n_all = n_lat + n_batch * ctx_len
    assert seq & (seq - 1) == 0 and ctx_len & (ctx_len - 1) == 0
    tm = 1024
    assert seq % tm == 0 and (n_batch * ctx_len) % tm == 0 and seq % GRID_W == 0

    xa = x.reshape(n_lat, d)
    xc = ctx.reshape(n_batch * ctx_len, d)
    pad = (-(n_batch + 1)) % 8
    cvec = jnp.concatenate([c, c_ctx[None, :], jnp.zeros((pad, d), F32)], axis=0)
    mod_all = _adaln(cvec, ada_w, ada_b)
    cos_t, sin_t = _rope_tables(seq, tm)
    geom = dict(n_lat=n_lat, seq=seq, n_batch=n_batch)

    for l in range(depth):
        last = l == depth - 1
        rows = n_lat if last else n_all
        mod = mod_all[l]
        w_in_l = w_in[l].astype(BF16)

        h = _norm_mod(xa, norm1_g[l], mod, rows=n_all, k_shift=0, k_scale=1, x_ctx=xc, **geom)
        q, k, v, z, gb, gates = _in_proj(h, w_in_l, q_norm_g[l], k_norm_g[l], b_gate[l], cos_t, sin_t,
                                         rows_q=rows, rows_kv=n_all, n_lat=n_lat, seq=seq, d=d, tm=tm)
        attn = _attention(q, k, v, rows_q=rows, n_lat=n_lat, seq=seq, ctx_len=ctx_len)
        merged = _merge(attn, z, gb, gates, conv_w[l], w_attn_out[l].astype(BF16),
                        w_conv_out[l].astype(BF16), rows=rows, n_lat=n_lat, seq=seq, ctx_len=ctx_len)
        xa = _resid_proj(merged, w_o[l].astype(BF16), xa, mod, rows=rows, k_gate=2, x_ctx=xc, **geom)
        xc = None

        h2, h2t = _norm_mod(xa, norm2_g[l], mod, rows=rows, k_shift=3, k_scale=4, with_transpose=True, **geom)
        th, e1, e2 = _peer_scores(h2, peer_wq[l].astype(BF16), peer_k1[l], peer_k2[l], rows=rows,
                                  n_i=PEER_TE // N_KEYS)
        xa = _peer_mix(h2t, peer_u[l].astype(BF16), peer_v[l].T.astype(BF16), th, e1, e2, xa, mod,
                       final_g if last else None, rows=rows, k_gate=5, **geom)
    return xa.reshape(n_batch, seq, d)
```

```python
import functools

import jax
import jax.numpy as jnp
from jax import lax
from jax.experimental import pallas as pl
from jax.experimental.pallas import tpu as pltpu

F32 = jnp.float32
BF16 = jnp.bfloat16

HEAD_DIM = 128
KV_GROUPS = 4
GRID_W = 64
ROPE_THETA = 10000.0
PEER_HEADS = 8
N_KEYS = 128
PEER_TOPK = 16
N_MOD = 6
EPS = 1e-6
LOG2_E = 1.4426950408889634
LANES = 128
BF16_SUBLANES = 16
VMEM_LIMIT_BYTES = 56 * 1024 * 1024

_NT = (((1,), (1,)), ((), ()))


def _cparams(*sem, flags=None):
    return pltpu.CompilerParams(dimension_semantics=sem, vmem_limit_bytes=VMEM_LIMIT_BYTES, flags=flags)


def _mod_row(row0, n_lat, seq, n_batch):
    return jnp.where(row0 < n_lat, row0 // seq, n_batch)


def _mod_vec(mod_ref, b, k, d):
    return mod_ref[pl.ds(b, 1), k * d:(k + 1) * d]


def _adaln_kernel(c_ref, w_ref, b_ref, o_ref):
    cv = c_ref[...]
    s = cv * jax.nn.sigmoid(cv)
    o_ref[0] = jnp.dot(s, w_ref[0], preferred_element_type=F32) + b_ref[0]


def _adaln(cvec, ada_w, ada_b, tn=1024):
    depth, d, n = ada_w.shape
    rows = cvec.shape[0]
    return pl.pallas_call(
        _adaln_kernel,
        grid=(depth, n // tn),
        in_specs=[pl.BlockSpec((rows, d), lambda l, j: (0, 0)),
                  pl.BlockSpec((1, d, tn), lambda l, j: (l, 0, j)),
                  pl.BlockSpec((1, 1, tn), lambda l, j: (l, 0, j))],
        out_specs=pl.BlockSpec((1, rows, tn), lambda l, j: (l, 0, j)),
        out_shape=jax.ShapeDtypeStruct((depth, rows, n), F32),
        compiler_params=_cparams("arbitrary", "arbitrary"),
        name="adaln",
    )(cvec, ada_w, ada_b.reshape(depth, 1, n))


def _cast_kernel(w_ref, o_ref, *, transpose):
    w = w_ref[0]
    o_ref[...] = (w.T if transpose else w).astype(o_ref.dtype)


def _to_bf16(w, layer, *, transpose=False, tr=1024, tc=1024):
    _, rows, cols = w.shape
    in_spec = pl.BlockSpec((1, tr, tc), lambda i, j: (layer, i, j))
    if transpose:
        out_spec, out_shape = pl.BlockSpec((tc, tr), lambda i, j: (j, i)), (cols, rows)
    else:
        out_spec, out_shape = pl.BlockSpec((tr, tc), lambda i, j: (i, j)), (rows, cols)
    return pl.pallas_call(
        functools.partial(_cast_kernel, transpose=transpose),
        grid=(rows // tr, cols // tc),
        in_specs=[in_spec],
        out_specs=out_spec,
        out_shape=jax.ShapeDtypeStruct(out_shape, BF16),
        compiler_params=_cparams("parallel", "parallel"),
        name="cast_bf16",
    )(w)


def _tile_of(lat_ref, ctx_ref, row0, n_lat):
    if ctx_ref is None:
        return lat_ref[...]
    return jnp.where(row0 < n_lat, lat_ref[...], ctx_ref[...])


def _row_specs(x_ctx, block, n_lat, col_of=None):
    n_lat_tiles = n_lat // block[0]
    col = (lambda *j: 0) if col_of is None else col_of
    if x_ctx is None:
        return [pl.BlockSpec(block, lambda i, *j: (i, col(*j)))]
    return [pl.BlockSpec(block, lambda i, *j: (jnp.minimum(i, n_lat_tiles - 1), col(*j))),
            pl.BlockSpec(block, lambda i, *j: (jnp.maximum(i - n_lat_tiles, 0), col(*j)))]


def _norm_mod_kernel(*refs, split, tm, n_lat, seq, n_batch, k_shift, k_scale):
    x_ref, xc_ref = (refs[0], refs[1]) if split else (refs[0], None)
    g_ref, mod_ref, o_ref, *maybe_ot_ref = refs[2 if split else 1:]
    d = x_ref.shape[1]
    row0 = pl.program_id(0) * tm
    b = _mod_row(row0, n_lat, seq, n_batch)
    xf = _tile_of(x_ref, xc_ref, row0, n_lat)
    y = xf * lax.rsqrt(jnp.mean(xf * xf, axis=-1, keepdims=True) + EPS)
    h = y * g_ref[...]
    shift = _mod_vec(mod_ref, b, k_shift, d)
    scale = _mod_vec(mod_ref, b, k_scale, d)
    h = h * (1.0 + scale) + shift
    o_ref[...] = h.astype(o_ref.dtype)
    for ot_ref in maybe_ot_ref:
        ot_ref[...] = h.T.astype(ot_ref.dtype)


def _norm_mod(x, g, mod, *, rows, n_lat, seq, n_batch, k_shift, k_scale, x_ctx=None,
              with_transpose=False, tm=512):
    d = x.shape[1]
    kern = functools.partial(_norm_mod_kernel, split=x_ctx is not None, tm=tm, n_lat=n_lat, seq=seq,
                             n_batch=n_batch, k_shift=k_shift, k_scale=k_scale)
    xs = [x] if x_ctx is None else [x, x_ctx]
    out_specs = [pl.BlockSpec((tm, d), lambda i: (i, 0))]
    out_shape = [jax.ShapeDtypeStruct((rows, d), BF16)]
    if with_transpose:
        out_specs.append(pl.BlockSpec((d, tm), lambda i: (0, i)))
        out_shape.append(jax.ShapeDtypeStruct((d, rows), BF16))
    out = pl.pallas_call(
        kern,
        grid=(rows // tm,),
        in_specs=_row_specs(x_ctx, (tm, d), n_lat) + [pl.BlockSpec((1, d), lambda i: (0, 0)),
                                                         pl.BlockSpec(mod.shape, lambda i: (0, 0))],
        out_specs=out_specs,
        out_shape=out_shape,
        compiler_params=_cparams("parallel"),
        name="norm_mod",
    )(*xs, g.reshape(1, d), mod)
    return out if with_transpose else out[0]


def _heads_norm_rope(acc, ones_bd, perm_bd, g, cos, sin, out_scale):
    n_heads = acc.shape[1] // HEAD_DIM
    ss = jnp.dot((acc * acc).astype(BF16), ones_bd, preferred_element_type=F32)
    y = acc * lax.rsqrt(ss * (1.0 / HEAD_DIM) + EPS) * g
    partner = jnp.dot(y.astype(BF16), perm_bd, preferred_element_type=F32)
    r = y * jnp.tile(cos, (1, n_heads)) + partner * jnp.tile(sin, (1, n_heads))
    if out_scale != 1.0:
        r = r * out_scale
    return r


def _q_kernel(h_ref, w_ref, ones_ref, perm_ref, g_ref, cos_ref, sin_ref, o_ref, *, out_scale):
    acc = jnp.dot(h_ref[...], w_ref[...], preferred_element_type=F32)
    r = _heads_norm_rope(acc, ones_ref[...], perm_ref[...], g_ref[...], cos_ref[...], sin_ref[...], out_scale)
    o_ref[...] = r.astype(o_ref.dtype)


def _kv_kernel(h_ref, wk_ref, wv_ref, ones_ref, perm_ref, g_ref, cos_ref, sin_ref, k_ref, v_ref):
    h = h_ref[...]
    acc = jnp.dot(h, wk_ref[...], preferred_element_type=F32)
    r = _heads_norm_rope(acc, ones_ref[...], perm_ref[...], g_ref[...], cos_ref[...], sin_ref[...], 1.0)
    k_ref[...] = r.astype(k_ref.dtype)
    vacc = jnp.dot(h, wv_ref[...], preferred_element_type=F32).astype(v_ref.dtype)
    ones = jnp.ones((vacc.shape[0], HEAD_DIM), v_ref.dtype)
    for g in range(vacc.shape[1] // HEAD_DIM):
        v_ref[:, 2 * g * HEAD_DIM:(2 * g + 1) * HEAD_DIM] = vacc[:, g * HEAD_DIM:(g + 1) * HEAD_DIM]
        v_ref[:, (2 * g + 1) * HEAD_DIM:(2 * g + 2) * HEAD_DIM] = ones


def _conv_in_kernel(h_ref, wu_ref, wb_ref, wc_ref, z_ref, gb_ref):
    h = h_ref[...]
    u = jnp.dot(h, wu_ref[...], preferred_element_type=F32)
    gc = jnp.dot(h, wc_ref[...], preferred_element_type=F32)
    z_ref[...] = (gc * u).astype(z_ref.dtype)
    gb_ref[...] = jnp.dot(h, wb_ref[...], preferred_element_type=F32).astype(gb_ref.dtype)


def _gate_kernel(h_ref, w_ref, b_ref, o_ref):
    acc = jnp.dot(h_ref[...], w_ref[...], preferred_element_type=F32)
    o_ref[...] = (0.5 + 0.5 * jnp.tanh(0.5 * (acc + b_ref[...]))).astype(o_ref.dtype)


def _rope_block(i, tm, n_lat, seq):
    return jnp.where(i * tm < n_lat, (i * tm % seq) // tm, seq // tm)


def _in_proj(h, w_in, q_g, k_g, b_gate, cos_t, sin_t, *, rows_q, rows_kv, n_lat, seq, d,
             tm=512, tn=512):
    kd = h.shape[1]
    kvw = d // KV_GROUPS
    q_end, k_end, v_end = d, d + kvw, d + 2 * kvw
    u_end, b_end, c_end = v_end + d, v_end + 2 * d, v_end + 3 * d
    hspec = pl.BlockSpec((tm, kd), lambda i, j: (i, 0))
    hd_spec = pl.BlockSpec((1, tn), lambda i, j: (0, 0))
    bd_spec = pl.BlockSpec((tn, tn), lambda i, j: (0, 0))
    heads = tn // HEAD_DIM
    lane = jnp.arange(HEAD_DIM)
    partner = jnp.where(lane % (HEAD_DIM // 2) < HEAD_DIM // 4, lane + HEAD_DIM // 4, lane - HEAD_DIM // 4)
    perm = (lane[:, None] == partner[None, :]).astype(BF16)
    ones_bd = jnp.kron(jnp.eye(heads, dtype=BF16), jnp.ones((HEAD_DIM, HEAD_DIM), BF16))
    perm_bd = jnp.kron(jnp.eye(heads, dtype=BF16), perm)
    tab_spec = pl.BlockSpec((tm, HEAD_DIM), lambda i, j: (_rope_block(i, tm, n_lat, seq), 0))

    def wspec(col0):
        return pl.BlockSpec((kd, tn), lambda i, j: (0, col0 // tn + j))

    def ospec():
        return pl.BlockSpec((tm, tn), lambda i, j: (i, j))

    q = pl.pallas_call(
        functools.partial(_q_kernel, out_scale=HEAD_DIM ** -0.5 * LOG2_E),
        grid=(rows_q // tm, d // tn),
        in_specs=[hspec, wspec(0), bd_spec, bd_spec, hd_spec, tab_spec, tab_spec],
        out_specs=ospec(),
        out_shape=jax.ShapeDtypeStruct((rows_q, d), BF16),
        compiler_params=_cparams("parallel", "arbitrary"),
        name="proj_q",
    )(h, w_in, ones_bd, perm_bd, jnp.tile(q_g, heads).reshape(1, tn), cos_t, sin_t)

    assert kvw == tn
    k, v = pl.pallas_call(
        _kv_kernel,
        grid=(rows_kv // tm, 1),
        in_specs=[hspec, wspec(q_end), wspec(k_end), bd_spec, bd_spec, hd_spec, tab_spec, tab_spec],
        out_specs=[ospec(), pl.BlockSpec((tm, 2 * tn), lambda i, j: (i, j))],
        out_shape=[jax.ShapeDtypeStruct((rows_kv, kvw), BF16), jax.ShapeDtypeStruct((rows_kv, 2 * kvw), BF16)],
        compiler_params=_cparams("parallel", "arbitrary"),
        name="proj_kv",
    )(h, w_in, w_in, ones_bd, perm_bd, jnp.tile(k_g, heads).reshape(1, tn), cos_t, sin_t)

    z, gb = pl.pallas_call(
        _conv_in_kernel,
        grid=(rows_q // tm, d // tn),
        in_specs=[hspec, wspec(v_end), wspec(u_end), wspec(b_end)],
        out_specs=[ospec(), ospec()],
        out_shape=[jax.ShapeDtypeStruct((rows_q, d), BF16)] * 2,
        compiler_params=_cparams("parallel", "arbitrary"),
        name="proj_conv",
    )(h, w_in, w_in, w_in)

    gates = pl.pallas_call(
        _gate_kernel,
        grid=(rows_q // tm, 2 * d // tn),
        in_specs=[hspec, wspec(c_end), pl.BlockSpec((1, tn), lambda i, j: (0, j))],
        out_specs=ospec(),
        out_shape=jax.ShapeDtypeStruct((rows_q, 2 * d), BF16),
        compiler_params=_cparams("parallel", "arbitrary"),
        name="proj_gates",
    )(h, w_in, b_gate.reshape(1, 2 * d))
    return q, k, v, z, gb, gates


ATTN_KEY_CHUNK = 256


def _attn_kernel(q_ref, kl_ref, vl_ref, kc_ref, vc_ref, o_ref, *, tq, n_lat):
    is_lat = pl.program_id(1) * tq < n_lat
    q = jnp.concatenate([q_ref[:, hh * HEAD_DIM:(hh + 1) * HEAD_DIM] for hh in range(KV_GROUPS)], axis=0)

    def step(carry, k, v):
        s = lax.dot_general(q, k, _NT, preferred_element_type=F32)
        m_new = jnp.max(s, axis=-1, keepdims=True)
        pv = lambda m: jnp.dot(jnp.exp2(s - m).astype(v.dtype), v, preferred_element_type=F32)
        if carry is None:
            return m_new, pv(m_new)
        m_old, acc = carry
        m_new = jnp.maximum(m_old, m_new)
        return m_new, acc * jnp.exp2(m_old - m_new) + pv(m_new)

    def run(with_lat):
        carry = step(None, kc_ref[...], vc_ref[...])
        if with_lat:
            for c in range(kl_ref.shape[0] // ATTN_KEY_CHUNK):
                rows = slice(c * ATTN_KEY_CHUNK, (c + 1) * ATTN_KEY_CHUNK)
                carry = step(carry, kl_ref[rows, :], vl_ref[rows, :])
        acc = carry[1]
        out = (acc[:, :HEAD_DIM] / acc[:, HEAD_DIM:]).astype(o_ref.dtype)
        for hh in range(KV_GROUPS):
            o_ref[:, hh * HEAD_DIM:(hh + 1) * HEAD_DIM] = out[hh * tq:(hh + 1) * tq, :]

    @pl.when(is_lat)
    def _():
        run(True)

    @pl.when(jnp.logical_not(is_lat))
    def _():
        run(False)


def _attention(q, k, v, *, rows_q, n_lat, seq, ctx_len, tq=256):
    assert ctx_len % tq == 0 or tq % ctx_len == 0
    assert tq <= ctx_len
    gw = KV_GROUPS * HEAD_DIM
    n_kv = k.shape[1] // HEAD_DIM

    def batch(i):
        return jnp.where(i * tq < n_lat, i * tq // seq, (i * tq - n_lat) // ctx_len)

    qspec = pl.BlockSpec((tq, gw), lambda g, i: (i, g))

    def lat_spec(width):
        return pl.BlockSpec((seq, width), lambda g, i: (batch(i), g))

    def ctx_spec(width):
        return pl.BlockSpec((ctx_len, width), lambda g, i: (n_lat // ctx_len + batch(i), g))

    return pl.pallas_call(
        functools.partial(_attn_kernel, tq=tq, n_lat=n_lat),
        grid=(n_kv, rows_q // tq),
        in_specs=[qspec, lat_spec(HEAD_DIM), lat_spec(2 * HEAD_DIM), ctx_spec(HEAD_DIM), ctx_spec(2 * HEAD_DIM)],
        out_specs=qspec,
        out_shape=jax.ShapeDtypeStruct((rows_q, q.shape[1]), BF16),
        compiler_params=_cparams("parallel", "arbitrary"),
        name="attention",
    )(q, k, v, k, v)


def _merge_kernel(attn_ref, z_ref, zp_ref, zn_ref, gb_ref, ga_ref, gc_ref, cw_ref,
                  wao_ref, wco_ref, o_ref, y_scr, *, tm, n_lat, seq, ctx_len):
    @pl.when(pl.program_id(1) == 0)
    def _():
        z = z_ref[...].astype(F32)
        loc = lax.broadcasted_iota(jnp.int32, (tm, 1), 0)
        row = pl.program_id(0) * tm + loc
        seqlen = jnp.where(row < n_lat, seq, ctx_len)
        pos = row & (seqlen - 1)
        halo_prev = zp_ref[...].astype(F32)[BF16_SUBLANES - 1:BF16_SUBLANES, :]
        halo_next = zn_ref[...].astype(F32)[0:1, :]
        z_prev = jnp.where(loc == 0, halo_prev, pltpu.roll(z, 1, 0))
        z_prev = jnp.where(pos == 0, 0.0, z_prev)
        z_next = jnp.where(loc == tm - 1, halo_next, pltpu.roll(z, tm - 1, 0))
        z_next = jnp.where(pos == seqlen - 1, 0.0, z_next)
        cw = cw_ref[...]
        conv = cw[0:1, :] * z_prev + cw[1:2, :] * z + cw[2:3, :] * z_next
        y_scr[...] = (gb_ref[...].astype(F32) * conv).astype(y_scr.dtype)

    acc_a = jnp.dot(attn_ref[...], wao_ref[...], preferred_element_type=F32)
    acc_c = jnp.dot(y_scr[...], wco_ref[...], preferred_element_type=F32)
    o_ref[...] = (ga_ref[...].astype(F32) * acc_a + gc_ref[...].astype(F32) * acc_c).astype(o_ref.dtype)


def _merge(attn, z, gb, gates, conv_w, w_ao, w_co, *, rows, n_lat, seq, ctx_len, tm=512, tn=1024):
    d = attn.shape[1]
    hb = BF16_SUBLANES
    n_hblk = z.shape[0] // hb
    row_spec = pl.BlockSpec((tm, d), lambda i, j: (i, 0))
    prev_spec = pl.BlockSpec((hb, d), lambda i, j: (jnp.maximum(i * (tm // hb) - 1, 0), 0))
    next_spec = pl.BlockSpec((hb, d), lambda i, j: (jnp.minimum((i + 1) * (tm // hb), n_hblk - 1), 0))
    wspec = pl.BlockSpec((d, tn), lambda i, j: (0, j))
    return pl.pallas_call(
        functools.partial(_merge_kernel, tm=tm, n_lat=n_lat, seq=seq, ctx_len=ctx_len),
        grid=(rows // tm, d // tn),
        in_specs=[row_spec, row_spec, prev_spec, next_spec, row_spec,
                  pl.BlockSpec((tm, tn), lambda i, j: (i, j)),
                  pl.BlockSpec((tm, tn), lambda i, j: (i, d // tn + j)),
                  pl.BlockSpec(conv_w.shape, lambda i, j: (0, 0)),
                  wspec, wspec],
        out_specs=pl.BlockSpec((tm, tn), lambda i, j: (i, j)),
        out_shape=jax.ShapeDtypeStruct((rows, d), BF16),
        scratch_shapes=[pltpu.VMEM((tm, d), BF16)],
        compiler_params=_cparams("parallel", "arbitrary"),
        name="merge",
    )(attn, z, z, z, gb, gates, gates, conv_w, w_ao, w_co)


def _resid_proj_kernel(a_ref, w_ref, *refs, split, tm, tn, n_lat, seq, n_batch, k_gate, d):
    x_ref, xc_ref = (refs[0], refs[1]) if split else (refs[0], None)
    mod_ref, o_ref = refs[2 if split else 1:]
    row0 = pl.program_id(0) * tm
    b = _mod_row(row0, n_lat, seq, n_batch)
    col0 = pl.multiple_of(k_gate * d + pl.program_id(1) * tn, LANES)
    gate = mod_ref[pl.ds(b, 1), pl.ds(col0, tn)]
    acc = jnp.dot(a_ref[...], w_ref[...], preferred_element_type=F32)
    o_ref[...] = _tile_of(x_ref, xc_ref, row0, n_lat) + gate * acc


def _resid_proj(a, w, x, mod, *, rows, n_lat, seq, n_batch, k_gate, x_ctx=None, tm=1024, tn=1024):
    d = a.shape[1]
    xs = [x] if x_ctx is None else [x, x_ctx]
    return pl.pallas_call(
        functools.partial(_resid_proj_kernel, split=x_ctx is not None, tm=tm, tn=tn, n_lat=n_lat, seq=seq,
                          n_batch=n_batch, k_gate=k_gate, d=d),
        grid=(rows // tm, d // tn),
        in_specs=[pl.BlockSpec((tm, d), lambda i, j: (i, 0)),
                  pl.BlockSpec((d, tn), lambda i, j: (0, j))]
        + _row_specs(x_ctx, (tm, tn), n_lat, col_of=lambda j: j)
        + [pl.BlockSpec(mod.shape, lambda i, j: (0, 0))],
        out_specs=pl.BlockSpec((tm, tn), lambda i, j: (i, j)),
        out_shape=jax.ShapeDtypeStruct((rows, d), F32),
        compiler_params=_cparams("parallel", "arbitrary"),
        name="resid_proj",
    )(a, w, *xs, mod)


N_TOP = PEER_TOPK + 1
TOP_ROWS = 24
N_CAND = 96


F32_SUBLANES = 8


def _merge_exchange_pairs(n):
    pairs = []
    p = 1
    while p < n:
        k = p
        while k >= 1:
            for j in range(k % p, n - k, 2 * k):
                for i in range(min(k, n - j - k)):
                    if (i + j) // (2 * p) == (i + j + k) // (2 * p):
                        pairs.append((i + j, i + j + k))
            k //= 2
        p *= 2
    return pairs


def _top_values(s, k, out_ref):
    n = s.shape[0] // F32_SUBLANES
    r = [s[i * F32_SUBLANES:(i + 1) * F32_SUBLANES, :] for i in range(n)]
    for a, b in _merge_exchange_pairs(n):
        r[a], r[b] = jnp.maximum(r[a], r[b]), jnp.minimum(r[a], r[b])
    prev = mx = None
    for t in range(k):
        prev = mx
        mx = jnp.max(r[0], axis=0, keepdims=True)
        if out_ref is not None:
            out_ref[t:t + 1, :] = mx
        pop = r[0] == mx
        for i in range(min(n, k - 1 - t)):
            r[i] = jnp.where(pop, r[i + 1] if i + 1 < n else -jnp.inf, r[i])
    return prev, mx


def _candidate_sums(v1_ref, v2_ref, cand_ref):
    cand_ref[0:TOP_ROWS, :] = v1_ref[0:1, :] + v2_ref[...]
    for a in range(1, 8):
        cand_ref[16 + 8 * a:24 + 8 * a, :] = v1_ref[a:a + 1, :] + v2_ref[0:8, :]
    cand_ref[80:96, :] = v1_ref[8:TOP_ROWS, :] + v2_ref[0:1, :]


def _peer_score_kernel(h_ref, wq_ref, k1_ref, k2_ref, th_ref, e1_ref, e2_ref,
                       v1_scr, v2_scr, cand_scr, *, n_i):
    qp = jnp.dot(h_ref[...], wq_ref[...], preferred_element_type=F32)
    pad = jnp.full((TOP_ROWS - N_TOP, v1_scr.shape[1]), -jnp.inf, F32)
    v1_scr[N_TOP:TOP_ROWS, :] = pad
    v2_scr[N_TOP:TOP_ROWS, :] = pad
    for h in range(PEER_HEADS):
        q1 = qp[:, (2 * h) * N_KEYS:(2 * h + 1) * N_KEYS]
        q2 = qp[:, (2 * h + 1) * N_KEYS:(2 * h + 2) * N_KEYS]
        s1 = lax.dot_general(k1_ref[h], q1, _NT, preferred_element_type=F32) * LOG2_E
        s2 = lax.dot_general(k2_ref[h], q2, _NT, preferred_element_type=F32) * LOG2_E
        _top_values(s1, N_TOP, v1_scr)
        _top_values(s2, N_TOP, v2_scr)
        _candidate_sums(v1_scr, v2_scr, cand_scr)
        cand = cand_scr[...]
        c16, c17 = _top_values(cand, N_TOP, None)
        tau = 0.5 * (c16 + c17)
        top1 = v1_scr[0:1, :]
        top2 = v2_scr[0:1, :]
        z = jnp.sum(jnp.where(cand >= c16, jnp.exp2(cand - (top1 + top2)), 0.0), axis=0, keepdims=True)
        theta = jnp.exp2(tau - s1 - top2)
        e1 = jnp.exp2(s1 - (top1 + jnp.log2(z) + 1.0))
        for g in range(N_KEYS // n_i):
            th_ref[g, h * n_i:(h + 1) * n_i, :] = theta[g * n_i:(g + 1) * n_i, :]
            e1_ref[g, h * n_i:(h + 1) * n_i, :] = e1[g * n_i:(g + 1) * n_i, :]
        e2_ref[h] = jnp.exp2(s2 - top2)


def _peer_scores(h2, wq, k1, k2, *, rows, n_i, tm=256):
    d = h2.shape[1]
    n_grp = N_KEYS // n_i
    grouped = pl.BlockSpec((n_grp, PEER_HEADS * n_i, tm), lambda i: (0, 0, i))
    by_head = pl.BlockSpec((PEER_HEADS, N_KEYS, tm), lambda i: (0, 0, i))
    grouped_shape = jax.ShapeDtypeStruct((n_grp, PEER_HEADS * n_i, rows), F32)
    by_head_shape = jax.ShapeDtypeStruct((PEER_HEADS, N_KEYS, rows), F32)
    return pl.pallas_call(
        functools.partial(_peer_score_kernel, n_i=n_i),
        grid=(rows // tm,),
        in_specs=[pl.BlockSpec((tm, d), lambda i: (i, 0)),
                  pl.BlockSpec(wq.shape, lambda i: (0, 0)),
                  pl.BlockSpec(k1.shape, lambda i: (0, 0, 0)),
                  pl.BlockSpec(k2.shape, lambda i: (0, 0, 0))],
        out_specs=[grouped, grouped, by_head],
        out_shape=[grouped_shape, grouped_shape, by_head_shape],
        scratch_shapes=[pltpu.VMEM((TOP_ROWS, tm), F32), pltpu.VMEM((TOP_ROWS, tm), F32),
                        pltpu.VMEM((N_CAND, tm), F32)],
        compiler_params=_cparams("parallel"),
        name="peer_scores",
    )(h2, wq, k1, k2)


def _twice_gelu(x):
    return x * (1.0 + lax.erf(x * (0.5 ** 0.5)))


def _peer_mix_kernel(ht_ref, u_ref, vt_ref, th_ref, e1_ref, e2_ref, x_ref, mod_ref, *rest,
                     tt, te, n_lat, seq, n_batch, k_gate, final_norm):
    if final_norm:
        fg_ref, o_ref, acc_ref, a_scr, g_scr, w_scr = rest
    else:
        o_ref, acc_ref, a_scr, g_scr, w_scr = rest
    j = pl.program_id(1)
    d = x_ref.shape[1]
    n_i = te // N_KEYS

    @pl.when(j == 0)
    def _():
        acc_ref[...] = jnp.zeros_like(acc_ref)

    a_scr[...] = jnp.dot(u_ref[...], ht_ref[...], preferred_element_type=F32)

    tails = []
    for ii in range(n_i):
        for tc in range(tt // LANES):
            cs = slice(tc * LANES, (tc + 1) * LANES)
            for r in range(N_KEYS // F32_SUBLANES):
                ks = slice(r * F32_SUBLANES, (r + 1) * F32_SUBLANES)
                gate = tails[-GATE_CHAINS] * 0.0 if len(tails) >= GATE_CHAINS else None
                for h in range(PEER_HEADS):
                    row = slice(h * n_i + ii, h * n_i + ii + 1)
                    e2 = e2_ref[h, ks, cs]
                    wgt = jnp.where(e2 >= th_ref[0, row, cs], e2, 0.0) * e1_ref[0, row, cs]
                    gate = wgt if gate is None else gate + wgt
                tails.append(gate)
                g_scr[ii * N_KEYS + r * F32_SUBLANES:ii * N_KEYS + (r + 1) * F32_SUBLANES, cs] = gate

    w_scr[...] = (g_scr[...] * _twice_gelu(a_scr[...])).astype(w_scr.dtype)
    acc_ref[...] += jnp.dot(vt_ref[...], w_scr[...], preferred_element_type=F32)

    @pl.when(j == pl.num_programs(1) - 1)
    def _():
        b = _mod_row(pl.program_id(0) * tt, n_lat, seq, n_batch)
        xn = x_ref[...] + _mod_vec(mod_ref, b, k_gate, d) * acc_ref[...].T
        if final_norm:
            xn = xn * lax.rsqrt(jnp.mean(xn * xn, axis=-1, keepdims=True) + EPS) * fg_ref[...]
        o_ref[...] = xn


PEER_TE = 512
GATE_CHAINS = 2


def _peer_mix(h2t, u, vt, th, e1, e2, x, mod, final_g, *, rows, n_lat, seq, n_batch, k_gate,
              tt=512, te=PEER_TE):
    d = x.shape[1]
    n_tiles = u.shape[0] // te
    final_norm = final_g is not None
    n_i = te // N_KEYS
    grouped = pl.BlockSpec((1, PEER_HEADS * n_i, tt), lambda t, j: (j, 0, t))
    by_head = pl.BlockSpec((PEER_HEADS, N_KEYS, tt), lambda t, j: (0, 0, t))
    in_specs = [pl.BlockSpec((d, tt), lambda t, j: (0, t)),
                pl.BlockSpec((te, d), lambda t, j: (j, 0)),
                pl.BlockSpec((d, te), lambda t, j: (0, j)),
                grouped, grouped, by_head,
                pl.BlockSpec((tt, d), lambda t, j: (t, 0)),
                pl.BlockSpec(mod.shape, lambda t, j: (0, 0))]
    args = [h2t, u, vt, th, e1, e2, x, mod]
    if final_norm:
        in_specs.append(pl.BlockSpec((1, d), lambda t, j: (0, 0)))
        args.append(final_g.reshape(1, d))
    return pl.pallas_call(
        functools.partial(_peer_mix_kernel, tt=tt, te=te, n_lat=n_lat, seq=seq, n_batch=n_batch,
                          k_gate=k_gate, final_norm=final_norm),
        grid=(rows // tt, n_tiles),
        in_specs=in_specs,
        out_specs=pl.BlockSpec((tt, d), lambda t, j: (t, 0)),
        out_shape=jax.ShapeDtypeStruct((rows, d), F32),
        scratch_shapes=[pltpu.VMEM((d, tt), F32), pltpu.VMEM((te, tt), F32), pltpu.VMEM((te, tt), F32),
                        pltpu.VMEM((te, tt), BF16)],
        compiler_params=_cparams("parallel", "arbitrary"),
        name="peer_mix",
    )(*args)


def _rope_tables(seq, pad_rows):
    half = HEAD_DIM // 2
    t = jnp.arange(seq)
    row = (t // GRID_W).astype(F32)
    col = (t % GRID_W).astype(F32)
    inv = ROPE_THETA ** (-jnp.arange(0, half, 2, dtype=F32) / half)
    ang_r = row[:, None] * inv[None, :]
    ang_c = col[:, None] * inv[None, :]
    cos = jnp.concatenate([jnp.cos(ang_r), jnp.cos(ang_r), jnp.cos(ang_c), jnp.cos(ang_c)], axis=-1)
    sin = jnp.concatenate([-jnp.sin(ang_r), jnp.sin(ang_r), -jnp.sin(ang_c), jnp.sin(ang_c)], axis=-1)
    cos = jnp.concatenate([cos, jnp.ones((pad_rows, HEAD_DIM), F32)], axis=0)
    sin = jnp.concatenate([sin, jnp.zeros((pad_rows, HEAD_DIM), F32)], axis=0)
    return cos, sin


def kernel(x, c, ctx, c_ctx, ada_w, ada_b, norm1_g, norm2_g, w_in, b_gate, q_norm_g, k_norm_g,
           conv_w, w_attn_out, w_conv_out, w_o, peer_wq, peer_k1, peer_k2, peer_u, peer_v, final_g):
    n_batch, seq, d = x.shape
    ctx_len = ctx.shape[1]
    depth = ada_w.shape[0]
    n_lat = n_batch * seq
    n_all = n_lat + n_batch * ctx_len
    assert seq & (seq - 1) == 0 and ctx_len & (ctx_len - 1) == 0
    tm = 1024
    assert seq % tm == 0 and (n_batch * ctx_len) % tm == 0 and seq % GRID_W == 0

    xa = x.reshape(n_lat, d)
    xc = ctx.reshape(n_batch * ctx_len, d)
    pad = (-(n_batch + 1)) % 8
    cvec = jnp.concatenate([c, c_ctx[None, :], jnp.zeros((pad, d), F32)], axis=0)
    mod_all = _adaln(cvec, ada_w, ada_b)
    cos_t, sin_t = _rope_tables(seq, tm)
    geom = dict(n_lat=n_lat, seq=seq, n_batch=n_batch)

    for l in range(depth):
        last = l == depth - 1
        rows = n_lat if last else n_all
        mod = mod_all[l]
        w_in_l = _to_bf16(w_in, l)

        h = _norm_mod(xa, norm1_g[l], mod, rows=n_all, k_shift=0, k_scale=1, x_ctx=xc, **geom)
        q, k, v, z, gb, gates = _in_proj(h, w_in_l, q_norm_g[l], k_norm_g[l], b_gate[l], cos_t, sin_t,
                                         rows_q=rows, rows_kv=n_all, n_lat=n_lat, seq=seq, d=d, tm=tm)
        attn = _attention(q, k, v, rows_q=rows, n_lat=n_lat, seq=seq, ctx_len=ctx_len)
        merged = _merge(attn, z, gb, gates, conv_w[l], _to_bf16(w_attn_out, l), _to_bf16(w_conv_out, l),
                        rows=rows, n_lat=n_lat, seq=seq, ctx_len=ctx_len)
        xa = _resid_proj(merged, _to_bf16(w_o, l), xa, mod, rows=rows, k_gate=2, x_ctx=xc, **geom)
        xc = None

        h2, h2t = _norm_mod(xa, norm2_g[l], mod, rows=rows, k_shift=3, k_scale=4, with_transpose=True, **geom)
        th, e1, e2 = _peer_scores(h2, _to_bf16(peer_wq, l), peer_k1[l], peer_k2[l], rows=rows,
                                  n_i=PEER_TE // N_KEYS)
        xa = _peer_mix(h2t, _to_bf16(peer_u, l), _to_bf16(peer_v, l, transpose=True), th, e1, e2, xa, mod,
                       final_g if last else None, rows=rows, k_gate=5, **geom)
    return xa.reshape(n_batch, seq, d)
```

```python
import functools

import jax
import jax.numpy as jnp
from jax import lax
from jax.experimental import pallas as pl
from jax.experimental.pallas import tpu as pltpu

F32 = jnp.float32
BF16 = jnp.bfloat16

HEAD_DIM = 128
KV_GROUPS = 4
GRID_W = 64
ROPE_THETA = 10000.0
PEER_HEADS = 8
N_KEYS = 128
PEER_TOPK = 16
N_MOD = 6
EPS = 1e-6
LOG2_E = 1.4426950408889634
LANES = 128
BF16_SUBLANES = 16
VMEM_LIMIT_BYTES = 56 * 1024 * 1024

_NT = (((1,), (1,)), ((), ()))


def _cparams(*sem, flags=None):
    return pltpu.CompilerParams(dimension_semantics=sem, vmem_limit_bytes=VMEM_LIMIT_BYTES, flags=flags)


def _mod_row(row0, n_lat, seq, n_batch):
    return jnp.where(row0 < n_lat, row0 // seq, n_batch)


def _mod_vec(mod_ref, b, k, d):
    return mod_ref[pl.ds(b, 1), k * d:(k + 1) * d]


def _adaln_kernel(c_ref, w_ref, b_ref, o_ref):
    cv = c_ref[...]
    s = cv * jax.nn.sigmoid(cv)
    o_ref[0] = jnp.dot(s, w_ref[0], preferred_element_type=F32) + b_ref[0]


def _adaln(cvec, ada_w, ada_b, tn=1024):
    depth, d, n = ada_w.shape
    rows = cvec.shape[0]
    return pl.pallas_call(
        _adaln_kernel,
        grid=(depth, n // tn),
        in_specs=[pl.BlockSpec((rows, d), lambda l, j: (0, 0)),
                  pl.BlockSpec((1, d, tn), lambda l, j: (l, 0, j)),
                  pl.BlockSpec((1, 1, tn), lambda l, j: (l, 0, j))],
        out_specs=pl.BlockSpec((1, rows, tn), lambda l, j: (l, 0, j)),
        out_shape=jax.ShapeDtypeStruct((depth, rows, n), F32),
        compiler_params=_cparams("arbitrary", "arbitrary"),
        name="adaln",
    )(cvec, ada_w, ada_b.reshape(depth, 1, n))


def _cast_kernel(w_ref, o_ref, *, transpose):
    w = w_ref[0]
    o_ref[...] = (w.T if transpose else w).astype(o_ref.dtype)


def _to_bf16(w, layer, *, transpose=False, tr=1024, tc=1024):
    _, rows, cols = w.shape
    in_spec = pl.BlockSpec((1, tr, tc), lambda i, j: (layer, i, j))
    if transpose:
        out_spec, out_shape = pl.BlockSpec((tc, tr), lambda i, j: (j, i)), (cols, rows)
    else:
        out_spec, out_shape = pl.BlockSpec((tr, tc), lambda i, j: (i, j)), (rows, cols)
    return pl.pallas_call(
        functools.partial(_cast_kernel, transpose=transpose),
        grid=(rows // tr, cols // tc),
        in_specs=[in_spec],
        out_specs=out_spec,
        out_shape=jax.ShapeDtypeStruct(out_shape, BF16),
        compiler_params=_cparams("parallel", "parallel"),
        name="cast_bf16",
    )(w)


def _tile_of(lat_ref, ctx_ref, row0, n_lat):
    if ctx_ref is None:
        return lat_ref[...]
    return jnp.where(row0 < n_lat, lat_ref[...], ctx_ref[...])


def _row_specs(x_ctx, block, n_lat, col_of=None):
    n_lat_tiles = n_lat // block[0]
    col = (lambda *j: 0) if col_of is None else col_of
    if x_ctx is None:
        return [pl.BlockSpec(block, lambda i, *j: (i, col(*j)))]
    return [pl.BlockSpec(block, lambda i, *j: (jnp.minimum(i, n_lat_tiles - 1), col(*j))),
            pl.BlockSpec(block, lambda i, *j: (jnp.maximum(i - n_lat_tiles, 0), col(*j)))]


def _modulated_norm(xf, g, mod_ref, b, k_shift, k_scale):
    d = xf.shape[1]
    y = xf * lax.rsqrt(jnp.mean(xf * xf, axis=-1, keepdims=True) + EPS)
    return (y * g) * (1.0 + _mod_vec(mod_ref, b, k_scale, d)) + _mod_vec(mod_ref, b, k_shift, d)


def _norm_mod_kernel(*refs, split, tm, n_lat, seq, n_batch, k_shift, k_scale):
    x_ref, xc_ref = (refs[0], refs[1]) if split else (refs[0], None)
    g_ref, mod_ref, o_ref = refs[2 if split else 1:]
    row0 = pl.program_id(0) * tm
    b = _mod_row(row0, n_lat, seq, n_batch)
    xf = _tile_of(x_ref, xc_ref, row0, n_lat)
    o_ref[...] = _modulated_norm(xf, g_ref[...], mod_ref, b, k_shift, k_scale).astype(o_ref.dtype)


def _norm_mod(x, g, mod, *, rows, n_lat, seq, n_batch, k_shift, k_scale, x_ctx=None, tm=512):
    d = x.shape[1]
    kern = functools.partial(_norm_mod_kernel, split=x_ctx is not None, tm=tm, n_lat=n_lat, seq=seq,
                             n_batch=n_batch, k_shift=k_shift, k_scale=k_scale)
    xs = [x] if x_ctx is None else [x, x_ctx]
    return pl.pallas_call(
        kern,
        grid=(rows // tm,),
        in_specs=_row_specs(x_ctx, (tm, d), n_lat) + [pl.BlockSpec((1, d), lambda i: (0, 0)),
                                                         pl.BlockSpec(mod.shape, lambda i: (0, 0))],
        out_specs=pl.BlockSpec((tm, d), lambda i: (i, 0)),
        out_shape=jax.ShapeDtypeStruct((rows, d), BF16),
        compiler_params=_cparams("parallel"),
        name="norm_mod",
    )(*xs, g.reshape(1, d), mod)


def _heads_norm_rope(acc, ones_bd, perm_bd, g, cos, sin, out_scale):
    n_heads = acc.shape[1] // HEAD_DIM
    ss = jnp.dot((acc * acc).astype(BF16), ones_bd, preferred_element_type=F32)
    y = acc * lax.rsqrt(ss * (1.0 / HEAD_DIM) + EPS) * g
    partner = jnp.dot(y.astype(BF16), perm_bd, preferred_element_type=F32)
    r = y * jnp.tile(cos, (1, n_heads)) + partner * jnp.tile(sin, (1, n_heads))
    if out_scale != 1.0:
        r = r * out_scale
    return r


def _q_kernel(h_ref, w_ref, ones_ref, perm_ref, g_ref, cos_ref, sin_ref, o_ref, *, out_scale):
    acc = jnp.dot(h_ref[...], w_ref[...], preferred_element_type=F32)
    r = _heads_norm_rope(acc, ones_ref[...], perm_ref[...], g_ref[...], cos_ref[...], sin_ref[...], out_scale)
    o_ref[...] = r.astype(o_ref.dtype)


def _kv_kernel(h_ref, wk_ref, wv_ref, ones_ref, perm_ref, g_ref, cos_ref, sin_ref, k_ref, v_ref):
    h = h_ref[...]
    acc = jnp.dot(h, wk_ref[...], preferred_element_type=F32)
    r = _heads_norm_rope(acc, ones_ref[...], perm_ref[...], g_ref[...], cos_ref[...], sin_ref[...], 1.0)
    k_ref[...] = r.astype(k_ref.dtype)
    vacc = jnp.dot(h, wv_ref[...], preferred_element_type=F32).astype(v_ref.dtype)
    ones = jnp.ones((vacc.shape[0], HEAD_DIM), v_ref.dtype)
    for g in range(vacc.shape[1] // HEAD_DIM):
        v_ref[:, 2 * g * HEAD_DIM:(2 * g + 1) * HEAD_DIM] = vacc[:, g * HEAD_DIM:(g + 1) * HEAD_DIM]
        v_ref[:, (2 * g + 1) * HEAD_DIM:(2 * g + 2) * HEAD_DIM] = ones


def _conv_in_kernel(h_ref, wu_ref, wb_ref, wc_ref, z_ref, gb_ref):
    h = h_ref[...]
    u = jnp.dot(h, wu_ref[...], preferred_element_type=F32)
    gc = jnp.dot(h, wc_ref[...], preferred_element_type=F32)
    z_ref[...] = (gc * u).astype(z_ref.dtype)
    gb_ref[...] = jnp.dot(h, wb_ref[...], preferred_element_type=F32).astype(gb_ref.dtype)


def _gate_kernel(h_ref, w_ref, b_ref, o_ref):
    acc = jnp.dot(h_ref[...], w_ref[...], preferred_element_type=F32)
    o_ref[...] = (0.5 + 0.5 * jnp.tanh(0.5 * (acc + b_ref[...]))).astype(o_ref.dtype)


def _rope_block(i, tm, n_lat, seq):
    return jnp.where(i * tm < n_lat, (i * tm % seq) // tm, seq // tm)


def _in_proj(h, w_in, q_g, k_g, b_gate, cos_t, sin_t, *, rows_q, rows_kv, n_lat, seq, d,
             tm=512, tn=512):
    kd = h.shape[1]
    kvw = d // KV_GROUPS
    q_end, k_end, v_end = d, d + kvw, d + 2 * kvw
    u_end, b_end, c_end = v_end + d, v_end + 2 * d, v_end + 3 * d
    hspec = pl.BlockSpec((tm, kd), lambda i, j: (i, 0))
    hd_spec = pl.BlockSpec((1, tn), lambda i, j: (0, 0))
    bd_spec = pl.BlockSpec((tn, tn), lambda i, j: (0, 0))
    heads = tn // HEAD_DIM
    lane = jnp.arange(HEAD_DIM)
    partner = jnp.where(lane % (HEAD_DIM // 2) < HEAD_DIM // 4, lane + HEAD_DIM // 4, lane - HEAD_DIM // 4)
    perm = (lane[:, None] == partner[None, :]).astype(BF16)
    ones_bd = jnp.kron(jnp.eye(heads, dtype=BF16), jnp.ones((HEAD_DIM, HEAD_DIM), BF16))
    perm_bd = jnp.kron(jnp.eye(heads, dtype=BF16), perm)
    tab_spec = pl.BlockSpec((tm, HEAD_DIM), lambda i, j: (_rope_block(i, tm, n_lat, seq), 0))

    def wspec(col0):
        return pl.BlockSpec((kd, tn), lambda i, j: (0, col0 // tn + j))

    def ospec():
        return pl.BlockSpec((tm, tn), lambda i, j: (i, j))

    q = pl.pallas_call(
        functools.partial(_q_kernel, out_scale=HEAD_DIM ** -0.5 * LOG2_E),
        grid=(rows_q // tm, d // tn),
        in_specs=[hspec, wspec(0), bd_spec, bd_spec, hd_spec, tab_spec, tab_spec],
        out_specs=ospec(),
        out_shape=jax.ShapeDtypeStruct((rows_q, d), BF16),
        compiler_params=_cparams("parallel", "arbitrary"),
        name="proj_q",
    )(h, w_in, ones_bd, perm_bd, jnp.tile(q_g, heads).reshape(1, tn), cos_t, sin_t)

    assert kvw == tn
    k, v = pl.pallas_call(
        _kv_kernel,
        grid=(rows_kv // tm, 1),
        in_specs=[hspec, wspec(q_end), wspec(k_end), bd_spec, bd_spec, hd_spec, tab_spec, tab_spec],
        out_specs=[ospec(), pl.BlockSpec((tm, 2 * tn), lambda i, j: (i, j))],
        out_shape=[jax.ShapeDtypeStruct((rows_kv, kvw), BF16), jax.ShapeDtypeStruct((rows_kv, 2 * kvw), BF16)],
        compiler_params=_cparams("parallel", "arbitrary"),
        name="proj_kv",
    )(h, w_in, w_in, ones_bd, perm_bd, jnp.tile(k_g, heads).reshape(1, tn), cos_t, sin_t)

    z, gb = pl.pallas_call(
        _conv_in_kernel,
        grid=(rows_q // tm, d // tn),
        in_specs=[hspec, wspec(v_end), wspec(u_end), wspec(b_end)],
        out_specs=[ospec(), ospec()],
        out_shape=[jax.ShapeDtypeStruct((rows_q, d), BF16)] * 2,
        compiler_params=_cparams("parallel", "arbitrary"),
        name="proj_conv",
    )(h, w_in, w_in, w_in)

    gates = pl.pallas_call(
        _gate_kernel,
        grid=(rows_q // tm, 2 * d // tn),
        in_specs=[hspec, wspec(c_end), pl.BlockSpec((1, tn), lambda i, j: (0, j))],
        out_specs=ospec(),
        out_shape=jax.ShapeDtypeStruct((rows_q, 2 * d), BF16),
        compiler_params=_cparams("parallel", "arbitrary"),
        name="proj_gates",
    )(h, w_in, b_gate.reshape(1, 2 * d))
    return q, k, v, z, gb, gates


ATTN_KEY_CHUNK = 256


def _attn_kernel(q_ref, kl_ref, vl_ref, kc_ref, vc_ref, o_ref, *, tq, n_lat):
    is_lat = pl.program_id(1) * tq < n_lat
    q = jnp.concatenate([q_ref[:, hh * HEAD_DIM:(hh + 1) * HEAD_DIM] for hh in range(KV_GROUPS)], axis=0)

    def step(carry, k, v):
        s = lax.dot_general(q, k, _NT, preferred_element_type=F32)
        m_new = jnp.max(s, axis=-1, keepdims=True)
        pv = lambda m: jnp.dot(jnp.exp2(s - m).astype(v.dtype), v, preferred_element_type=F32)
        if carry is None:
            return m_new, pv(m_new)
        m_old, acc = carry
        m_new = jnp.maximum(m_old, m_new)
        return m_new, acc * jnp.exp2(m_old - m_new) + pv(m_new)

    def run(with_lat):
        carry = step(None, kc_ref[...], vc_ref[...])
        if with_lat:
            for c in range(kl_ref.shape[0] // ATTN_KEY_CHUNK):
                rows = slice(c * ATTN_KEY_CHUNK, (c + 1) * ATTN_KEY_CHUNK)
                carry = step(carry, kl_ref[rows, :], vl_ref[rows, :])
        acc = carry[1]
        out = (acc[:, :HEAD_DIM] / acc[:, HEAD_DIM:]).astype(o_ref.dtype)
        for hh in range(KV_GROUPS):
            o_ref[:, hh * HEAD_DIM:(hh + 1) * HEAD_DIM] = out[hh * tq:(hh + 1) * tq, :]

    @pl.when(is_lat)
    def _():
        run(True)

    @pl.when(jnp.logical_not(is_lat))
    def _():
        run(False)


def _attention(q, k, v, *, rows_q, n_lat, seq, ctx_len, tq=256):
    assert ctx_len % tq == 0 or tq % ctx_len == 0
    assert tq <= ctx_len
    gw = KV_GROUPS * HEAD_DIM
    n_kv = k.shape[1] // HEAD_DIM

    def batch(i):
        return jnp.where(i * tq < n_lat, i * tq // seq, (i * tq - n_lat) // ctx_len)

    qspec = pl.BlockSpec((tq, gw), lambda g, i: (i, g))

    def lat_spec(width):
        return pl.BlockSpec((seq, width), lambda g, i: (batch(i), g))

    def ctx_spec(width):
        return pl.BlockSpec((ctx_len, width), lambda g, i: (n_lat // ctx_len + batch(i), g))

    return pl.pallas_call(
        functools.partial(_attn_kernel, tq=tq, n_lat=n_lat),
        grid=(n_kv, rows_q // tq),
        in_specs=[qspec, lat_spec(HEAD_DIM), lat_spec(2 * HEAD_DIM), ctx_spec(HEAD_DIM), ctx_spec(2 * HEAD_DIM)],
        out_specs=qspec,
        out_shape=jax.ShapeDtypeStruct((rows_q, q.shape[1]), BF16),
        compiler_params=_cparams("parallel", "arbitrary"),
        name="attention",
    )(q, k, v, k, v)


def _merge_kernel(attn_ref, z_ref, zp_ref, zn_ref, gb_ref, ga_ref, gc_ref, cw_ref,
                  wao_ref, wco_ref, o_ref, y_scr, *, tm, n_lat, seq, ctx_len):
    @pl.when(pl.program_id(1) == 0)
    def _():
        z = z_ref[...].astype(F32)
        loc = lax.broadcasted_iota(jnp.int32, (tm, 1), 0)
        row = pl.program_id(0) * tm + loc
        seqlen = jnp.where(row < n_lat, seq, ctx_len)
        pos = row & (seqlen - 1)
        halo_prev = zp_ref[...].astype(F32)[BF16_SUBLANES - 1:BF16_SUBLANES, :]
        halo_next = zn_ref[...].astype(F32)[0:1, :]
        z_prev = jnp.where(loc == 0, halo_prev, pltpu.roll(z, 1, 0))
        z_prev = jnp.where(pos == 0, 0.0, z_prev)
        z_next = jnp.where(loc == tm - 1, halo_next, pltpu.roll(z, tm - 1, 0))
        z_next = jnp.where(pos == seqlen - 1, 0.0, z_next)
        cw = cw_ref[...]
        conv = cw[0:1, :] * z_prev + cw[1:2, :] * z + cw[2:3, :] * z_next
        y_scr[...] = (gb_ref[...].astype(F32) * conv).astype(y_scr.dtype)

    acc_a = jnp.dot(attn_ref[...], wao_ref[...], preferred_element_type=F32)
    acc_c = jnp.dot(y_scr[...], wco_ref[...], preferred_element_type=F32)
    o_ref[...] = (ga_ref[...].astype(F32) * acc_a + gc_ref[...].astype(F32) * acc_c).astype(o_ref.dtype)


def _merge(attn, z, gb, gates, conv_w, w_ao, w_co, *, rows, n_lat, seq, ctx_len, tm=512, tn=1024):
    d = attn.shape[1]
    hb = BF16_SUBLANES
    n_hblk = z.shape[0] // hb
    row_spec = pl.BlockSpec((tm, d), lambda i, j: (i, 0))
    prev_spec = pl.BlockSpec((hb, d), lambda i, j: (jnp.maximum(i * (tm // hb) - 1, 0), 0))
    next_spec = pl.BlockSpec((hb, d), lambda i, j: (jnp.minimum((i + 1) * (tm // hb), n_hblk - 1), 0))
    wspec = pl.BlockSpec((d, tn), lambda i, j: (0, j))
    return pl.pallas_call(
        functools.partial(_merge_kernel, tm=tm, n_lat=n_lat, seq=seq, ctx_len=ctx_len),
        grid=(rows // tm, d // tn),
        in_specs=[row_spec, row_spec, prev_spec, next_spec, row_spec,
                  pl.BlockSpec((tm, tn), lambda i, j: (i, j)),
                  pl.BlockSpec((tm, tn), lambda i, j: (i, d // tn + j)),
                  pl.BlockSpec(conv_w.shape, lambda i, j: (0, 0)),
                  wspec, wspec],
        out_specs=pl.BlockSpec((tm, tn), lambda i, j: (i, j)),
        out_shape=jax.ShapeDtypeStruct((rows, d), BF16),
        scratch_shapes=[pltpu.VMEM((tm, d), BF16)],
        compiler_params=_cparams("parallel", "arbitrary"),
        name="merge",
    )(attn, z, z, z, gb, gates, gates, conv_w, w_ao, w_co)


def _resid_proj_kernel(a_ref, w_ref, *refs, split, tm, n_lat, seq, n_batch, k_gate, k_shift, k_scale):
    x_ref, xc_ref = (refs[0], refs[1]) if split else (refs[0], None)
    g_ref, mod_ref, o_ref, h_ref, ht_ref = refs[2 if split else 1:]
    d = o_ref.shape[1]
    row0 = pl.program_id(0) * tm
    b = _mod_row(row0, n_lat, seq, n_batch)
    acc = jnp.dot(a_ref[...], w_ref[...], preferred_element_type=F32)
    xn = _tile_of(x_ref, xc_ref, row0, n_lat) + _mod_vec(mod_ref, b, k_gate, d) * acc
    o_ref[...] = xn
    h = _modulated_norm(xn, g_ref[...], mod_ref, b, k_shift, k_scale)
    h_ref[...] = h.astype(h_ref.dtype)
    ht_ref[...] = h.T.astype(ht_ref.dtype)


def _resid_proj(a, w, x, g_next, mod, *, rows, n_lat, seq, n_batch, k_gate, k_shift, k_scale, x_ctx=None, tm=512):
    d = a.shape[1]
    xs = [x] if x_ctx is None else [x, x_ctx]
    return pl.pallas_call(
        functools.partial(_resid_proj_kernel, split=x_ctx is not None, tm=tm, n_lat=n_lat, seq=seq,
                          n_batch=n_batch, k_gate=k_gate, k_shift=k_shift, k_scale=k_scale),
        grid=(rows // tm,),
        in_specs=[pl.BlockSpec((tm, d), lambda i: (i, 0)),
                  pl.BlockSpec((d, d), lambda i: (0, 0))]
        + _row_specs(x_ctx, (tm, d), n_lat)
        + [pl.BlockSpec((1, d), lambda i: (0, 0)),
           pl.BlockSpec(mod.shape, lambda i: (0, 0))],
        out_specs=[pl.BlockSpec((tm, d), lambda i: (i, 0)),
                   pl.BlockSpec((tm, d), lambda i: (i, 0)),
                   pl.BlockSpec((d, tm), lambda i: (0, i))],
        out_shape=[jax.ShapeDtypeStruct((rows, d), F32),
                   jax.ShapeDtypeStruct((rows, d), BF16),
                   jax.ShapeDtypeStruct((d, rows), BF16)],
        compiler_params=_cparams("parallel"),
        name="resid_proj",
    )(a, w, *xs, g_next.reshape(1, d), mod)


N_TOP = PEER_TOPK + 1
TOP_ROWS = 24
N_CAND = 96


F32_SUBLANES = 8


def _merge_exchange_pairs(n):
    pairs = []
    p = 1
    while p < n:
        k = p
        while k >= 1:
            for j in range(k % p, n - k, 2 * k):
                for i in range(min(k, n - j - k)):
                    if (i + j) // (2 * p) == (i + j + k) // (2 * p):
                        pairs.append((i + j, i + j + k))
            k //= 2
        p *= 2
    return pairs


def _top_values(s, k, out_ref):
    n = s.shape[0] // F32_SUBLANES
    r = [s[i * F32_SUBLANES:(i + 1) * F32_SUBLANES, :] for i in range(n)]
    for a, b in _merge_exchange_pairs(n):
        r[a], r[b] = jnp.maximum(r[a], r[b]), jnp.minimum(r[a], r[b])
    prev = mx = None
    for t in range(k):
        prev = mx
        mx = jnp.max(r[0], axis=0, keepdims=True)
        if out_ref is not None:
            out_ref[t:t + 1, :] = mx
        pop = r[0] == mx
        for i in range(min(n, k - 1 - t)):
            r[i] = jnp.where(pop, r[i + 1] if i + 1 < n else -jnp.inf, r[i])
    return prev, mx


def _candidate_sums(v1_ref, v2_ref, cand_ref):
    cand_ref[0:TOP_ROWS, :] = v1_ref[0:1, :] + v2_ref[...]
    for a in range(1, 8):
        cand_ref[16 + 8 * a:24 + 8 * a, :] = v1_ref[a:a + 1, :] + v2_ref[0:8, :]
    cand_ref[80:96, :] = v1_ref[8:TOP_ROWS, :] + v2_ref[0:1, :]


def _peer_score_kernel(h_ref, wq_ref, k1_ref, k2_ref, th_ref, e1_ref, e2_ref,
                       v1_scr, v2_scr, cand_scr, *, n_i):
    qp = jnp.dot(h_ref[...], wq_ref[...], preferred_element_type=F32)
    pad = jnp.full((TOP_ROWS - N_TOP, v1_scr.shape[1]), -jnp.inf, F32)
    v1_scr[N_TOP:TOP_ROWS, :] = pad
    v2_scr[N_TOP:TOP_ROWS, :] = pad
    for h in range(PEER_HEADS):
        q1 = qp[:, (2 * h) * N_KEYS:(2 * h + 1) * N_KEYS]
        q2 = qp[:, (2 * h + 1) * N_KEYS:(2 * h + 2) * N_KEYS]
        s1 = lax.dot_general(k1_ref[h], q1, _NT, preferred_element_type=F32) * LOG2_E
        s2 = lax.dot_general(k2_ref[h], q2, _NT, preferred_element_type=F32) * LOG2_E
        _top_values(s1, N_TOP, v1_scr)
        _top_values(s2, N_TOP, v2_scr)
        _candidate_sums(v1_scr, v2_scr, cand_scr)
        cand = cand_scr[...]
        c16, c17 = _top_values(cand, N_TOP, None)
        tau = 0.5 * (c16 + c17)
        top1 = v1_scr[0:1, :]
        top2 = v2_scr[0:1, :]
        z = jnp.sum(jnp.where(cand >= c16, jnp.exp2(cand - (top1 + top2)), 0.0), axis=0, keepdims=True)
        theta = jnp.exp2(tau - s1 - top2)
        e1 = jnp.exp2(s1 - (top1 + jnp.log2(z) + 1.0))
        for g in range(N_KEYS // n_i):
            th_ref[g, h * n_i:(h + 1) * n_i, :] = theta[g * n_i:(g + 1) * n_i, :]
            e1_ref[g, h * n_i:(h + 1) * n_i, :] = e1[g * n_i:(g + 1) * n_i, :]
        e2_ref[h] = jnp.exp2(s2 - top2)


def _peer_scores(h2, wq, k1, k2, *, rows, n_i, tm=256):
    d = h2.shape[1]
    n_grp = N_KEYS // n_i
    grouped = pl.BlockSpec((n_grp, PEER_HEADS * n_i, tm), lambda i: (0, 0, i))
    by_head = pl.BlockSpec((PEER_HEADS, N_KEYS, tm), lambda i: (0, 0, i))
    grouped_shape = jax.ShapeDtypeStruct((n_grp, PEER_HEADS * n_i, rows), F32)
    by_head_shape = jax.ShapeDtypeStruct((PEER_HEADS, N_KEYS, rows), F32)
    return pl.pallas_call(
        functools.partial(_peer_score_kernel, n_i=n_i),
        grid=(rows // tm,),
        in_specs=[pl.BlockSpec((tm, d), lambda i: (i, 0)),
                  pl.BlockSpec(wq.shape, lambda i: (0, 0)),
                  pl.BlockSpec(k1.shape, lambda i: (0, 0, 0)),
                  pl.BlockSpec(k2.shape, lambda i: (0, 0, 0))],
        out_specs=[grouped, grouped, by_head],
        out_shape=[grouped_shape, grouped_shape, by_head_shape],
        scratch_shapes=[pltpu.VMEM((TOP_ROWS, tm), F32), pltpu.VMEM((TOP_ROWS, tm), F32),
                        pltpu.VMEM((N_CAND, tm), F32)],
        compiler_params=_cparams("parallel"),
        name="peer_scores",
    )(h2, wq, k1, k2)


def _twice_gelu(x):
    return x * (1.0 + lax.erf(x * (0.5 ** 0.5)))


def _peer_mix_kernel(ht_ref, u_ref, vt_ref, th_ref, e1_ref, e2_ref, x_ref, mod_ref, *rest,
                     tt, te, n_lat, seq, n_batch, k_gate, final_norm):
    if final_norm:
        fg_ref, o_ref, acc_ref, a_scr, g_scr, w_scr = rest
    else:
        gn_ref, modn_ref, o_ref, hn_ref, acc_ref, a_scr, g_scr, w_scr = rest
    j = pl.program_id(1)
    d = x_ref.shape[1]
    n_i = te // N_KEYS

    @pl.when(j == 0)
    def _():
        acc_ref[...] = jnp.zeros_like(acc_ref)

    a_scr[...] = jnp.dot(u_ref[...], ht_ref[...], preferred_element_type=F32)

    tails = []
    for ii in range(n_i):
        for tc in range(tt // LANES):
            cs = slice(tc * LANES, (tc + 1) * LANES)
            for r in range(N_KEYS // F32_SUBLANES):
                ks = slice(r * F32_SUBLANES, (r + 1) * F32_SUBLANES)
                gate = tails[-GATE_CHAINS] * 0.0 if len(tails) >= GATE_CHAINS else None
                for h in range(PEER_HEADS):
                    row = slice(h * n_i + ii, h * n_i + ii + 1)
                    e2 = e2_ref[h, ks, cs]
                    wgt = jnp.where(e2 >= th_ref[0, row, cs], e2, 0.0) * e1_ref[0, row, cs]
                    gate = wgt if gate is None else gate + wgt
                tails.append(gate)
                g_scr[ii * N_KEYS + r * F32_SUBLANES:ii * N_KEYS + (r + 1) * F32_SUBLANES, cs] = gate

    w_scr[...] = (g_scr[...] * _twice_gelu(a_scr[...])).astype(w_scr.dtype)
    acc_ref[...] += jnp.dot(vt_ref[...], w_scr[...], preferred_element_type=F32)

    @pl.when(j == pl.num_programs(1) - 1)
    def _():
        b = _mod_row(pl.program_id(0) * tt, n_lat, seq, n_batch)
        xn = x_ref[...] + _mod_vec(mod_ref, b, k_gate, d) * acc_ref[...].T
        if final_norm:
            xn = xn * lax.rsqrt(jnp.mean(xn * xn, axis=-1, keepdims=True) + EPS) * fg_ref[...]
        else:
            hn_ref[...] = _modulated_norm(xn, gn_ref[...], modn_ref, b, 0, 1).astype(hn_ref.dtype)
        o_ref[...] = xn


PEER_TE = 512
GATE_CHAINS = 2


def _peer_mix(h2t, u, vt, th, e1, e2, x, mod, final_g, next_g, next_mod, *, rows, n_lat, seq, n_batch, k_gate,
              tt=512, te=PEER_TE):
    d = x.shape[1]
    n_tiles = u.shape[0] // te
    final_norm = final_g is not None
    n_i = te // N_KEYS
    grouped = pl.BlockSpec((1, PEER_HEADS * n_i, tt), lambda t, j: (j, 0, t))
    by_head = pl.BlockSpec((PEER_HEADS, N_KEYS, tt), lambda t, j: (0, 0, t))
    in_specs = [pl.BlockSpec((d, tt), lambda t, j: (0, t)),
                pl.BlockSpec((te, d), lambda t, j: (j, 0)),
                pl.BlockSpec((d, te), lambda t, j: (0, j)),
                grouped, grouped, by_head,
                pl.BlockSpec((tt, d), lambda t, j: (t, 0)),
                pl.BlockSpec(mod.shape, lambda t, j: (0, 0))]
    args = [h2t, u, vt, th, e1, e2, x, mod]
    row_spec = pl.BlockSpec((tt, d), lambda t, j: (t, 0))
    if final_norm:
        in_specs.append(pl.BlockSpec((1, d), lambda t, j: (0, 0)))
        args.append(final_g.reshape(1, d))
        out_specs, out_shape = row_spec, jax.ShapeDtypeStruct((rows, d), F32)
    else:
        in_specs += [pl.BlockSpec((1, d), lambda t, j: (0, 0)), pl.BlockSpec(next_mod.shape, lambda t, j: (0, 0))]
        args += [next_g.reshape(1, d), next_mod]
        out_specs = [row_spec, row_spec]
        out_shape = [jax.ShapeDtypeStruct((rows, d), F32), jax.ShapeDtypeStruct((rows, d), BF16)]
    return pl.pallas_call(
        functools.partial(_peer_mix_kernel, tt=tt, te=te, n_lat=n_lat, seq=seq, n_batch=n_batch,
                          k_gate=k_gate, final_norm=final_norm),
        grid=(rows // tt, n_tiles),
        in_specs=in_specs,
        out_specs=out_specs,
        out_shape=out_shape,
        scratch_shapes=[pltpu.VMEM((d, tt), F32), pltpu.VMEM((te, tt), F32), pltpu.VMEM((te, tt), F32),
                        pltpu.VMEM((te, tt), BF16)],
        compiler_params=_cparams("parallel", "arbitrary"),
        name="peer_mix",
    )(*args)


def _rope_tables(seq, pad_rows):
    half = HEAD_DIM // 2
    t = jnp.arange(seq)
    row = (t // GRID_W).astype(F32)
    col = (t % GRID_W).astype(F32)
    inv = ROPE_THETA ** (-jnp.arange(0, half, 2, dtype=F32) / half)
    ang_r = row[:, None] * inv[None, :]
    ang_c = col[:, None] * inv[None, :]
    cos = jnp.concatenate([jnp.cos(ang_r), jnp.cos(ang_r), jnp.cos(ang_c), jnp.cos(ang_c)], axis=-1)
    sin = jnp.concatenate([-jnp.sin(ang_r), jnp.sin(ang_r), -jnp.sin(ang_c), jnp.sin(ang_c)], axis=-1)
    cos = jnp.concatenate([cos, jnp.ones((pad_rows, HEAD_DIM), F32)], axis=0)
    sin = jnp.concatenate([sin, jnp.zeros((pad_rows, HEAD_DIM), F32)], axis=0)
    return cos, sin


def kernel(x, c, ctx, c_ctx, ada_w, ada_b, norm1_g, norm2_g, w_in, b_gate, q_norm_g, k_norm_g,
           conv_w, w_attn_out, w_conv_out, w_o, peer_wq, peer_k1, peer_k2, peer_u, peer_v, final_g):
    n_batch, seq, d = x.shape
    ctx_len = ctx.shape[1]
    depth = ada_w.shape[0]
    n_lat = n_batch * seq
    n_all = n_lat + n_batch * ctx_len
    assert seq & (seq - 1) == 0 and ctx_len & (ctx_len - 1) == 0
    tm = 1024
    assert seq % tm == 0 and (n_batch * ctx_len) % tm == 0 and seq % GRID_W == 0

    xa = x.reshape(n_lat, d)
    xc = ctx.reshape(n_batch * ctx_len, d)
    pad = (-(n_batch + 1)) % 8
    cvec = jnp.concatenate([c, c_ctx[None, :], jnp.zeros((pad, d), F32)], axis=0)
    mod_all = _adaln(cvec, ada_w, ada_b)
    cos_t, sin_t = _rope_tables(seq, tm)
    geom = dict(n_lat=n_lat, seq=seq, n_batch=n_batch)

    for l in range(depth):
        last = l == depth - 1
        rows = n_lat if last else n_all
        mod = mod_all[l]
        w_in_l = _to_bf16(w_in, l)

        if l == 0:
            h = _norm_mod(xa, norm1_g[l], mod, rows=n_all, k_shift=0, k_scale=1, x_ctx=xc, **geom)
        q, k, v, z, gb, gates = _in_proj(h, w_in_l, q_norm_g[l], k_norm_g[l], b_gate[l], cos_t, sin_t,
                                         rows_q=rows, rows_kv=n_all, n_lat=n_lat, seq=seq, d=d, tm=tm)
        attn = _attention(q, k, v, rows_q=rows, n_lat=n_lat, seq=seq, ctx_len=ctx_len)
        merged = _merge(attn, z, gb, gates, conv_w[l], _to_bf16(w_attn_out, l), _to_bf16(w_conv_out, l),
                        rows=rows, n_lat=n_lat, seq=seq, ctx_len=ctx_len)
        xa, h2, h2t = _resid_proj(merged, _to_bf16(w_o, l), xa, norm2_g[l], mod, rows=rows, k_gate=2,
                                  k_shift=3, k_scale=4, x_ctx=xc, **geom)
        xc = None
        th, e1, e2 = _peer_scores(h2, _to_bf16(peer_wq, l), peer_k1[l], peer_k2[l], rows=rows,
                                  n_i=PEER_TE // N_KEYS)
        mix = functools.partial(_peer_mix, h2t, _to_bf16(peer_u, l), _to_bf16(peer_v, l, transpose=True),
                                th, e1, e2, xa, mod, rows=rows, k_gate=5, **geom)
        if last:
            xa = mix(final_g, None, None)
        else:
            xa, h = mix(None, norm1_g[l + 1], mod_all[l + 1])
    return xa.reshape(n_batch, seq, d)
```

```python
import functools

import jax
import jax.numpy as jnp
from jax import lax
from jax.experimental import pallas as pl
from jax.experimental.pallas import tpu as pltpu

F32 = jnp.float32
BF16 = jnp.bfloat16

HEAD_DIM = 128
KV_GROUPS = 4
GRID_W = 64
ROPE_THETA = 10000.0
PEER_HEADS = 8
N_KEYS = 128
PEER_TOPK = 16
EPS = 1e-6
LOG2_E = 1.4426950408889634
LANES = 128
BF16_SUBLANES = 16
VMEM_LIMIT_BYTES = 56 * 1024 * 1024

_NT = (((1,), (1,)), ((), ()))


def _cparams(*sem):
    return pltpu.CompilerParams(dimension_semantics=sem, vmem_limit_bytes=VMEM_LIMIT_BYTES)


def _mod_row(row0, n_lat, seq, n_batch):
    return jnp.where(row0 < n_lat, row0 // seq, n_batch)


def _mod_vec(mod_ref, b, k, d):
    return mod_ref[pl.ds(b, 1), k * d:(k + 1) * d]


def _adaln_kernel(c_ref, w_ref, b_ref, o_ref):
    cv = c_ref[...]
    s = cv * jax.nn.sigmoid(cv)
    o_ref[0] = jnp.dot(s, w_ref[0], preferred_element_type=F32) + b_ref[0]


def _adaln(cvec, ada_w, ada_b, tn=1024):
    depth, d, n = ada_w.shape
    rows = cvec.shape[0]
    return pl.pallas_call(
        _adaln_kernel,
        grid=(depth, n // tn),
        in_specs=[pl.BlockSpec((rows, d), lambda l, j: (0, 0)),
                  pl.BlockSpec((1, d, tn), lambda l, j: (l, 0, j)),
                  pl.BlockSpec((1, 1, tn), lambda l, j: (l, 0, j))],
        out_specs=pl.BlockSpec((1, rows, tn), lambda l, j: (l, 0, j)),
        out_shape=jax.ShapeDtypeStruct((depth, rows, n), F32),
        compiler_params=_cparams("arbitrary", "arbitrary"),
        name="adaln",
    )(cvec, ada_w, ada_b.reshape(depth, 1, n))


def _cast_kernel(w_ref, o_ref, *, transpose):
    w = w_ref[0]
    o_ref[...] = (w.T if transpose else w).astype(o_ref.dtype)


def _to_bf16(w, layer, *, transpose=False, tr=1024, tc=1024):
    _, rows, cols = w.shape
    in_spec = pl.BlockSpec((1, tr, tc), lambda i, j: (layer, i, j))
    if transpose:
        out_spec, out_shape = pl.BlockSpec((tc, tr), lambda i, j: (j, i)), (cols, rows)
    else:
        out_spec, out_shape = pl.BlockSpec((tr, tc), lambda i, j: (i, j)), (rows, cols)
    return pl.pallas_call(
        functools.partial(_cast_kernel, transpose=transpose),
        grid=(rows // tr, cols // tc),
        in_specs=[in_spec],
        out_specs=out_spec,
        out_shape=jax.ShapeDtypeStruct(out_shape, BF16),
        compiler_params=_cparams("parallel", "parallel"),
        name="cast_bf16",
    )(w)


def _tile_of(lat_ref, ctx_ref, row0, n_lat):
    if ctx_ref is None:
        return lat_ref[...]
    return jnp.where(row0 < n_lat, lat_ref[...], ctx_ref[...])


def _row_specs(x_ctx, block, n_lat, col_of=None):
    n_lat_tiles = n_lat // block[0]
    col = (lambda *j: 0) if col_of is None else col_of
    if x_ctx is None:
        return [pl.BlockSpec(block, lambda i, *j: (i, col(*j)))]
    return [pl.BlockSpec(block, lambda i, *j: (jnp.minimum(i, n_lat_tiles - 1), col(*j))),
            pl.BlockSpec(block, lambda i, *j: (jnp.maximum(i - n_lat_tiles, 0), col(*j)))]


def _modulated_norm(xf, g, mod_ref, b, k_shift, k_scale):
    d = xf.shape[1]
    y = xf * lax.rsqrt(jnp.mean(xf * xf, axis=-1, keepdims=True) + EPS)
    return (y * g) * (1.0 + _mod_vec(mod_ref, b, k_scale, d)) + _mod_vec(mod_ref, b, k_shift, d)


def _norm_mod_kernel(*refs, split, tm, n_lat, seq, n_batch, k_shift, k_scale):
    x_ref, xc_ref = (refs[0], refs[1]) if split else (refs[0], None)
    g_ref, mod_ref, o_ref = refs[2 if split else 1:]
    row0 = pl.program_id(0) * tm
    b = _mod_row(row0, n_lat, seq, n_batch)
    xf = _tile_of(x_ref, xc_ref, row0, n_lat)
    o_ref[...] = _modulated_norm(xf, g_ref[...], mod_ref, b, k_shift, k_scale).astype(o_ref.dtype)


def _norm_mod(x, g, mod, *, rows, n_lat, seq, n_batch, k_shift, k_scale, x_ctx=None, tm=512):
    d = x.shape[1]
    kern = functools.partial(_norm_mod_kernel, split=x_ctx is not None, tm=tm, n_lat=n_lat, seq=seq,
                             n_batch=n_batch, k_shift=k_shift, k_scale=k_scale)
    xs = [x] if x_ctx is None else [x, x_ctx]
    return pl.pallas_call(
        kern,
        grid=(rows // tm,),
        in_specs=_row_specs(x_ctx, (tm, d), n_lat) + [pl.BlockSpec((1, d), lambda i: (0, 0)),
                                                         pl.BlockSpec(mod.shape, lambda i: (0, 0))],
        out_specs=pl.BlockSpec((tm, d), lambda i: (i, 0)),
        out_shape=jax.ShapeDtypeStruct((rows, d), BF16),
        compiler_params=_cparams("parallel"),
        name="norm_mod",
    )(*xs, g.reshape(1, d), mod)


def _heads_norm_rope(acc, ones_bd, perm_bd, g, cos, sin, out_scale):
    n_heads = acc.shape[1] // HEAD_DIM
    ss = jnp.dot((acc * acc).astype(BF16), ones_bd, preferred_element_type=F32)
    y = acc * lax.rsqrt(ss * (1.0 / HEAD_DIM) + EPS) * g
    partner = jnp.dot(y.astype(BF16), perm_bd, preferred_element_type=F32)
    r = y * jnp.tile(cos, (1, n_heads)) + partner * jnp.tile(sin, (1, n_heads))
    if out_scale != 1.0:
        r = r * out_scale
    return r


def _q_kernel(h_ref, w_ref, ones_ref, perm_ref, g_ref, cos_ref, sin_ref, o_ref, *, out_scale):
    acc = jnp.dot(h_ref[...], w_ref[...], preferred_element_type=F32)
    r = _heads_norm_rope(acc, ones_ref[...], perm_ref[...], g_ref[...], cos_ref[...], sin_ref[...], out_scale)
    o_ref[...] = r.astype(o_ref.dtype)


def _kv_kernel(h_ref, wk_ref, wv_ref, ones_ref, perm_ref, g_ref, cos_ref, sin_ref, k_ref, v_ref):
    h = h_ref[...]
    acc = jnp.dot(h, wk_ref[...], preferred_element_type=F32)
    r = _heads_norm_rope(acc, ones_ref[...], perm_ref[...], g_ref[...], cos_ref[...], sin_ref[...], 1.0)
    k_ref[...] = r.astype(k_ref.dtype)
    vacc = jnp.dot(h, wv_ref[...], preferred_element_type=F32).astype(v_ref.dtype)
    ones = jnp.ones((vacc.shape[0], HEAD_DIM), v_ref.dtype)
    for g in range(vacc.shape[1] // HEAD_DIM):
        v_ref[:, 2 * g * HEAD_DIM:(2 * g + 1) * HEAD_DIM] = vacc[:, g * HEAD_DIM:(g + 1) * HEAD_DIM]
        v_ref[:, (2 * g + 1) * HEAD_DIM:(2 * g + 2) * HEAD_DIM] = ones


def _conv_in_kernel(h_ref, wu_ref, wb_ref, wc_ref, z_ref, gb_ref):
    h = h_ref[...]
    u = jnp.dot(h, wu_ref[...], preferred_element_type=F32)
    gc = jnp.dot(h, wc_ref[...], preferred_element_type=F32)
    z_ref[...] = (gc * u).astype(z_ref.dtype)
    gb_ref[...] = jnp.dot(h, wb_ref[...], preferred_element_type=F32).astype(gb_ref.dtype)


def _gate_kernel(h_ref, w_ref, b_ref, o_ref):
    acc = jnp.dot(h_ref[...], w_ref[...], preferred_element_type=F32)
    o_ref[...] = (0.5 + 0.5 * jnp.tanh(0.5 * (acc + b_ref[...]))).astype(o_ref.dtype)


def _rope_block(i, tm, n_lat, seq):
    return jnp.where(i * tm < n_lat, (i * tm % seq) // tm, seq // tm)


def _in_proj(h, w_in, q_g, k_g, b_gate, cos_t, sin_t, *, rows_q, rows_kv, n_lat, seq, d,
             tm=512, tn=512):
    kd = h.shape[1]
    kvw = d // KV_GROUPS
    q_end, k_end, v_end = d, d + kvw, d + 2 * kvw
    u_end, b_end, c_end = v_end + d, v_end + 2 * d, v_end + 3 * d
    hspec = pl.BlockSpec((tm, kd), lambda i, j: (i, 0))
    hd_spec = pl.BlockSpec((1, tn), lambda i, j: (0, 0))
    bd_spec = pl.BlockSpec((tn, tn), lambda i, j: (0, 0))
    heads = tn // HEAD_DIM
    lane = jnp.arange(HEAD_DIM)
    partner = jnp.where(lane % (HEAD_DIM // 2) < HEAD_DIM // 4, lane + HEAD_DIM // 4, lane - HEAD_DIM // 4)
    perm = (lane[:, None] == partner[None, :]).astype(BF16)
    ones_bd = jnp.kron(jnp.eye(heads, dtype=BF16), jnp.ones((HEAD_DIM, HEAD_DIM), BF16))
    perm_bd = jnp.kron(jnp.eye(heads, dtype=BF16), perm)
    tab_spec = pl.BlockSpec((tm, HEAD_DIM), lambda i, j: (_rope_block(i, tm, n_lat, seq), 0))

    def wspec(col0):
        return pl.BlockSpec((kd, tn), lambda i, j: (0, col0 // tn + j))

    def ospec():
        return pl.BlockSpec((tm, tn), lambda i, j: (i, j))

    q = pl.pallas_call(
        functools.partial(_q_kernel, out_scale=HEAD_DIM ** -0.5 * LOG2_E),
        grid=(rows_q // tm, d // tn),
        in_specs=[hspec, wspec(0), bd_spec, bd_spec, hd_spec, tab_spec, tab_spec],
        out_specs=ospec(),
        out_shape=jax.ShapeDtypeStruct((rows_q, d), BF16),
        compiler_params=_cparams("parallel", "arbitrary"),
        name="proj_q",
    )(h, w_in, ones_bd, perm_bd, jnp.tile(q_g, heads).reshape(1, tn), cos_t, sin_t)

    assert kvw == tn
    k, v = pl.pallas_call(
        _kv_kernel,
        grid=(rows_kv // tm, 1),
        in_specs=[hspec, wspec(q_end), wspec(k_end), bd_spec, bd_spec, hd_spec, tab_spec, tab_spec],
        out_specs=[ospec(), pl.BlockSpec((tm, 2 * tn), lambda i, j: (i, j))],
        out_shape=[jax.ShapeDtypeStruct((rows_kv, kvw), BF16), jax.ShapeDtypeStruct((rows_kv, 2 * kvw), BF16)],
        compiler_params=_cparams("parallel", "arbitrary"),
        name="proj_kv",
    )(h, w_in, w_in, ones_bd, perm_bd, jnp.tile(k_g, heads).reshape(1, tn), cos_t, sin_t)

    z, gb = pl.pallas_call(
        _conv_in_kernel,
        grid=(rows_q // tm, d // tn),
        in_specs=[hspec, wspec(v_end), wspec(u_end), wspec(b_end)],
        out_specs=[ospec(), ospec()],
        out_shape=[jax.ShapeDtypeStruct((rows_q, d), BF16)] * 2,
        compiler_params=_cparams("parallel", "arbitrary"),
        name="proj_conv",
    )(h, w_in, w_in, w_in)

    gates = pl.pallas_call(
        _gate_kernel,
        grid=(rows_q // tm, 2 * d // tn),
        in_specs=[hspec, wspec(c_end), pl.BlockSpec((1, tn), lambda i, j: (0, j))],
        out_specs=ospec(),
        out_shape=jax.ShapeDtypeStruct((rows_q, 2 * d), BF16),
        compiler_params=_cparams("parallel", "arbitrary"),
        name="proj_gates",
    )(h, w_in, b_gate.reshape(1, 2 * d))
    return q, k, v, z, gb, gates


ATTN_KEY_CHUNK = 256


def _attn_kernel(q_ref, kl_ref, vl_ref, kc_ref, vc_ref, o_ref, *, tq, n_lat):
    is_lat = pl.program_id(1) * tq < n_lat
    q = jnp.concatenate([q_ref[:, hh * HEAD_DIM:(hh + 1) * HEAD_DIM] for hh in range(KV_GROUPS)], axis=0)

    def step(carry, k, v):
        s = lax.dot_general(q, k, _NT, preferred_element_type=F32)
        m_new = jnp.max(s, axis=-1, keepdims=True)
        pv = lambda m: jnp.dot(jnp.exp2(s - m).astype(v.dtype), v, preferred_element_type=F32)
        if carry is None:
            return m_new, pv(m_new)
        m_old, acc = carry
        m_new = jnp.maximum(m_old, m_new)
        return m_new, acc * jnp.exp2(m_old - m_new) + pv(m_new)

    def run(with_lat):
        carry = step(None, kc_ref[...], vc_ref[...])
        if with_lat:
            for c in range(kl_ref.shape[0] // ATTN_KEY_CHUNK):
                rows = slice(c * ATTN_KEY_CHUNK, (c + 1) * ATTN_KEY_CHUNK)
                carry = step(carry, kl_ref[rows, :], vl_ref[rows, :])
        acc = carry[1]
        out = (acc[:, :HEAD_DIM] / acc[:, HEAD_DIM:]).astype(o_ref.dtype)
        for hh in range(KV_GROUPS):
            o_ref[:, hh * HEAD_DIM:(hh + 1) * HEAD_DIM] = out[hh * tq:(hh + 1) * tq, :]

    @pl.when(is_lat)
    def _():
        run(True)

    @pl.when(jnp.logical_not(is_lat))
    def _():
        run(False)


def _attention(q, k, v, *, rows_q, n_lat, seq, ctx_len, tq=256):
    assert ctx_len % tq == 0 or tq % ctx_len == 0
    assert tq <= ctx_len
    gw = KV_GROUPS * HEAD_DIM
    n_kv = k.shape[1] // HEAD_DIM

    def batch(i):
        return jnp.where(i * tq < n_lat, i * tq // seq, (i * tq - n_lat) // ctx_len)

    qspec = pl.BlockSpec((tq, gw), lambda g, i: (i, g))

    def lat_spec(width):
        return pl.BlockSpec((seq, width), lambda g, i: (batch(i), g))

    def ctx_spec(width):
        return pl.BlockSpec((ctx_len, width), lambda g, i: (n_lat // ctx_len + batch(i), g))

    return pl.pallas_call(
        functools.partial(_attn_kernel, tq=tq, n_lat=n_lat),
        grid=(n_kv, rows_q // tq),
        in_specs=[qspec, lat_spec(HEAD_DIM), lat_spec(2 * HEAD_DIM), ctx_spec(HEAD_DIM), ctx_spec(2 * HEAD_DIM)],
        out_specs=qspec,
        out_shape=jax.ShapeDtypeStruct((rows_q, q.shape[1]), BF16),
        compiler_params=_cparams("parallel", "arbitrary"),
        name="attention",
    )(q, k, v, k, v)


def _merge_kernel(attn_ref, z_ref, zp_ref, zn_ref, gb_ref, ga_ref, gc_ref, cw_ref,
                  wao_ref, wco_ref, o_ref, y_scr, *, tm, n_lat, seq, ctx_len):
    @pl.when(pl.program_id(1) == 0)
    def _():
        z = z_ref[...].astype(F32)
        loc = lax.broadcasted_iota(jnp.int32, (tm, 1), 0)
        row = pl.program_id(0) * tm + loc
        seqlen = jnp.where(row < n_lat, seq, ctx_len)
        pos = row & (seqlen - 1)
        halo_prev = zp_ref[...].astype(F32)[BF16_SUBLANES - 1:BF16_SUBLANES, :]
        halo_next = zn_ref[...].astype(F32)[0:1, :]
        z_prev = jnp.where(loc == 0, halo_prev, pltpu.roll(z, 1, 0))
        z_prev = jnp.where(pos == 0, 0.0, z_prev)
        z_next = jnp.where(loc == tm - 1, halo_next, pltpu.roll(z, tm - 1, 0))
        z_next = jnp.where(pos == seqlen - 1, 0.0, z_next)
        cw = cw_ref[...]
        conv = cw[0:1, :] * z_prev + cw[1:2, :] * z + cw[2:3, :] * z_next
        y_scr[...] = (gb_ref[...].astype(F32) * conv).astype(y_scr.dtype)

    acc_a = jnp.dot(attn_ref[...], wao_ref[...], preferred_element_type=F32)
    acc_c = jnp.dot(y_scr[...], wco_ref[...], preferred_element_type=F32)
    o_ref[...] = (ga_ref[...].astype(F32) * acc_a + gc_ref[...].astype(F32) * acc_c).astype(o_ref.dtype)


def _merge(attn, z, gb, gates, conv_w, w_ao, w_co, *, rows, n_lat, seq, ctx_len, tm=512, tn=1024):
    d = attn.shape[1]
    hb = BF16_SUBLANES
    n_hblk = z.shape[0] // hb
    row_spec = pl.BlockSpec((tm, d), lambda i, j: (i, 0))
    prev_spec = pl.BlockSpec((hb, d), lambda i, j: (jnp.maximum(i * (tm // hb) - 1, 0), 0))
    next_spec = pl.BlockSpec((hb, d), lambda i, j: (jnp.minimum((i + 1) * (tm // hb), n_hblk - 1), 0))
    wspec = pl.BlockSpec((d, tn), lambda i, j: (0, j))
    return pl.pallas_call(
        functools.partial(_merge_kernel, tm=tm, n_lat=n_lat, seq=seq, ctx_len=ctx_len),
        grid=(rows // tm, d // tn),
        in_specs=[row_spec, row_spec, prev_spec, next_spec, row_spec,
                  pl.BlockSpec((tm, tn), lambda i, j: (i, j)),
                  pl.BlockSpec((tm, tn), lambda i, j: (i, d // tn + j)),
                  pl.BlockSpec(conv_w.shape, lambda i, j: (0, 0)),
                  wspec, wspec],
        out_specs=pl.BlockSpec((tm, tn), lambda i, j: (i, j)),
        out_shape=jax.ShapeDtypeStruct((rows, d), BF16),
        scratch_shapes=[pltpu.VMEM((tm, d), BF16)],
        compiler_params=_cparams("parallel", "arbitrary"),
        name="merge",
    )(attn, z, z, z, gb, gates, gates, conv_w, w_ao, w_co)


def _resid_proj_kernel(a_ref, w_ref, *refs, split, tm, n_lat, seq, n_batch, k_gate, k_shift, k_scale):
    x_ref, xc_ref = (refs[0], refs[1]) if split else (refs[0], None)
    g_ref, mod_ref, o_ref, h_ref, ht_ref = refs[2 if split else 1:]
    d = o_ref.shape[1]
    row0 = pl.program_id(0) * tm
    b = _mod_row(row0, n_lat, seq, n_batch)
    acc = jnp.dot(a_ref[...], w_ref[...], preferred_element_type=F32)
    xn = _tile_of(x_ref, xc_ref, row0, n_lat) + _mod_vec(mod_ref, b, k_gate, d) * acc
    o_ref[...] = xn
    h = _modulated_norm(xn, g_ref[...], mod_ref, b, k_shift, k_scale)
    h_ref[...] = h.astype(h_ref.dtype)
    ht_ref[...] = h.T.astype(ht_ref.dtype)


def _resid_proj(a, w, x, g_next, mod, *, rows, n_lat, seq, n_batch, k_gate, k_shift, k_scale, x_ctx=None, tm=512):
    d = a.shape[1]
    xs = [x] if x_ctx is None else [x, x_ctx]
    return pl.pallas_call(
        functools.partial(_resid_proj_kernel, split=x_ctx is not None, tm=tm, n_lat=n_lat, seq=seq,
                          n_batch=n_batch, k_gate=k_gate, k_shift=k_shift, k_scale=k_scale),
        grid=(rows // tm,),
        in_specs=[pl.BlockSpec((tm, d), lambda i: (i, 0)),
                  pl.BlockSpec((d, d), lambda i: (0, 0))]
        + _row_specs(x_ctx, (tm, d), n_lat)
        + [pl.BlockSpec((1, d), lambda i: (0, 0)),
           pl.BlockSpec(mod.shape, lambda i: (0, 0))],
        out_specs=[pl.BlockSpec((tm, d), lambda i: (i, 0)),
                   pl.BlockSpec((tm, d), lambda i: (i, 0)),
                   pl.BlockSpec((d, tm), lambda i: (0, i))],
        out_shape=[jax.ShapeDtypeStruct((rows, d), F32),
                   jax.ShapeDtypeStruct((rows, d), BF16),
                   jax.ShapeDtypeStruct((d, rows), BF16)],
        compiler_params=_cparams("parallel"),
        name="resid_proj",
    )(a, w, *xs, g_next.reshape(1, d), mod)


N_TOP = PEER_TOPK + 1
TOP_ROWS = 24
N_CAND = 96


F32_SUBLANES = 8


def _merge_exchange_pairs(n):
    pairs = []
    p = 1
    while p < n:
        k = p
        while k >= 1:
            for j in range(k % p, n - k, 2 * k):
                for i in range(min(k, n - j - k)):
                    if (i + j) // (2 * p) == (i + j + k) // (2 * p):
                        pairs.append((i + j, i + j + k))
            k //= 2
        p *= 2
    return pairs


def _top_values(s, k, out_ref):
    n = s.shape[0] // F32_SUBLANES
    r = [s[i * F32_SUBLANES:(i + 1) * F32_SUBLANES, :] for i in range(n)]
    for a, b in _merge_exchange_pairs(n):
        r[a], r[b] = jnp.maximum(r[a], r[b]), jnp.minimum(r[a], r[b])
    prev = mx = None
    for t in range(k):
        prev = mx
        mx = jnp.max(r[0], axis=0, keepdims=True)
        if out_ref is not None:
            out_ref[t:t + 1, :] = mx
        pop = r[0] == mx
        for i in range(min(n, k - 1 - t)):
            r[i] = jnp.where(pop, r[i + 1] if i + 1 < n else -jnp.inf, r[i])
    return prev, mx


def _candidate_sums(v1_ref, v2_ref, cand_ref):
    cand_ref[0:TOP_ROWS, :] = v1_ref[0:1, :] + v2_ref[...]
    for a in range(1, 8):
        cand_ref[16 + 8 * a:24 + 8 * a, :] = v1_ref[a:a + 1, :] + v2_ref[0:8, :]
    cand_ref[80:96, :] = v1_ref[8:TOP_ROWS, :] + v2_ref[0:1, :]


def _peer_score_kernel(h_ref, wq_ref, k1_ref, k2_ref, th_ref, e1_ref, e2_ref,
                       v1_scr, v2_scr, cand_scr, *, n_i):
    qp = jnp.dot(h_ref[...], wq_ref[...], preferred_element_type=F32)
    pad = jnp.full((TOP_ROWS - N_TOP, v1_scr.shape[1]), -jnp.inf, F32)
    v1_scr[N_TOP:TOP_ROWS, :] = pad
    v2_scr[N_TOP:TOP_ROWS, :] = pad
    for h in range(PEER_HEADS):
        q1 = qp[:, (2 * h) * N_KEYS:(2 * h + 1) * N_KEYS]
        q2 = qp[:, (2 * h + 1) * N_KEYS:(2 * h + 2) * N_KEYS]
        s1 = lax.dot_general(k1_ref[h], q1, _NT, preferred_element_type=F32) * LOG2_E
        s2 = lax.dot_general(k2_ref[h], q2, _NT, preferred_element_type=F32) * LOG2_E
        _top_values(s1, N_TOP, v1_scr)
        _top_values(s2, N_TOP, v2_scr)
        _candidate_sums(v1_scr, v2_scr, cand_scr)
        cand = cand_scr[...]
        c16, c17 = _top_values(cand, N_TOP, None)
        tau = 0.5 * (c16 + c17)
        top1 = v1_scr[0:1, :]
        top2 = v2_scr[0:1, :]
        z = jnp.sum(jnp.where(cand >= c16, jnp.exp2(cand - (top1 + top2)), 0.0), axis=0, keepdims=True)
        theta = jnp.exp2(tau - s1 - top2)
        e1 = jnp.exp2(s1 - (top1 + jnp.log2(z) + 1.0))
        for g in range(N_KEYS // n_i):
            th_ref[g, h * n_i:(h + 1) * n_i, :] = theta[g * n_i:(g + 1) * n_i, :]
            e1_ref[g, h * n_i:(h + 1) * n_i, :] = e1[g * n_i:(g + 1) * n_i, :]
        e2_ref[h] = jnp.exp2(s2 - top2)


def _peer_scores(h2, wq, k1, k2, *, rows, n_i, tm=256):
    d = h2.shape[1]
    n_grp = N_KEYS // n_i
    grouped = pl.BlockSpec((n_grp, PEER_HEADS * n_i, tm), lambda i: (0, 0, i))
    by_head = pl.BlockSpec((PEER_HEADS, N_KEYS, tm), lambda i: (0, 0, i))
    grouped_shape = jax.ShapeDtypeStruct((n_grp, PEER_HEADS * n_i, rows), F32)
    by_head_shape = jax.ShapeDtypeStruct((PEER_HEADS, N_KEYS, rows), F32)
    return pl.pallas_call(
        functools.partial(_peer_score_kernel, n_i=n_i),
        grid=(rows // tm,),
        in_specs=[pl.BlockSpec((tm, d), lambda i: (i, 0)),
                  pl.BlockSpec(wq.shape, lambda i: (0, 0)),
                  pl.BlockSpec(k1.shape, lambda i: (0, 0, 0)),
                  pl.BlockSpec(k2.shape, lambda i: (0, 0, 0))],
        out_specs=[grouped, grouped, by_head],
        out_shape=[grouped_shape, grouped_shape, by_head_shape],
        scratch_shapes=[pltpu.VMEM((TOP_ROWS, tm), F32), pltpu.VMEM((TOP_ROWS, tm), F32),
                        pltpu.VMEM((N_CAND, tm), F32)],
        compiler_params=_cparams("parallel"),
        name="peer_scores",
    )(h2, wq, k1, k2)


def _twice_gelu(x):
    return x * (1.0 + lax.erf(x * (0.5 ** 0.5)))


def _peer_mix_kernel(ht_ref, u_ref, vt_ref, th_ref, e1_ref, e2_ref, x_ref, mod_ref, *rest,
                     tt, te, n_lat, seq, n_batch, k_gate, final_norm):
    if final_norm:
        fg_ref, o_ref, acc_ref, a_scr, g_scr, w_scr = rest
    else:
        gn_ref, modn_ref, o_ref, hn_ref, acc_ref, a_scr, g_scr, w_scr = rest
    j = pl.program_id(1)
    d = x_ref.shape[1]
    n_i = te // N_KEYS

    @pl.when(j == 0)
    def _():
        acc_ref[...] = jnp.zeros_like(acc_ref)

    a_scr[...] = jnp.dot(u_ref[...], ht_ref[...], preferred_element_type=F32)

    tails = []
    for ii in range(n_i):
        for tc in range(tt // LANES):
            cs = slice(tc * LANES, (tc + 1) * LANES)
            for r in range(N_KEYS // F32_SUBLANES):
                ks = slice(r * F32_SUBLANES, (r + 1) * F32_SUBLANES)
                gate = tails[-GATE_CHAINS] * 0.0 if len(tails) >= GATE_CHAINS else None
                for h in range(PEER_HEADS):
                    row = slice(h * n_i + ii, h * n_i + ii + 1)
                    e2 = e2_ref[h, ks, cs]
                    wgt = jnp.where(e2 >= th_ref[0, row, cs], e2, 0.0) * e1_ref[0, row, cs]
                    gate = wgt if gate is None else gate + wgt
                tails.append(gate)
                g_scr[ii * N_KEYS + r * F32_SUBLANES:ii * N_KEYS + (r + 1) * F32_SUBLANES, cs] = gate

    w_scr[...] = (g_scr[...] * _twice_gelu(a_scr[...])).astype(w_scr.dtype)
    acc_ref[...] += jnp.dot(vt_ref[...], w_scr[...], preferred_element_type=F32)

    @pl.when(j == pl.num_programs(1) - 1)
    def _():
        b = _mod_row(pl.program_id(0) * tt, n_lat, seq, n_batch)
        xn = x_ref[...] + _mod_vec(mod_ref, b, k_gate, d) * acc_ref[...].T
        if final_norm:
            xn = xn * lax.rsqrt(jnp.mean(xn * xn, axis=-1, keepdims=True) + EPS) * fg_ref[...]
        else:
            hn_ref[...] = _modulated_norm(xn, gn_ref[...], modn_ref, b, 0, 1).astype(hn_ref.dtype)
        o_ref[...] = xn


PEER_TE = 512
GATE_CHAINS = 2


def _peer_mix(h2t, u, vt, th, e1, e2, x, mod, final_g, next_g, next_mod, *, rows, n_lat, seq, n_batch, k_gate,
              tt=512, te=PEER_TE):
    d = x.shape[1]
    n_tiles = u.shape[0] // te
    final_norm = final_g is not None
    n_i = te // N_KEYS
    grouped = pl.BlockSpec((1, PEER_HEADS * n_i, tt), lambda t, j: (j, 0, t))
    by_head = pl.BlockSpec((PEER_HEADS, N_KEYS, tt), lambda t, j: (0, 0, t))
    in_specs = [pl.BlockSpec((d, tt), lambda t, j: (0, t)),
                pl.BlockSpec((te, d), lambda t, j: (j, 0)),
                pl.BlockSpec((d, te), lambda t, j: (0, j)),
                grouped, grouped, by_head,
                pl.BlockSpec((tt, d), lambda t, j: (t, 0)),
                pl.BlockSpec(mod.shape, lambda t, j: (0, 0))]
    args = [h2t, u, vt, th, e1, e2, x, mod]
    row_spec = pl.BlockSpec((tt, d), lambda t, j: (t, 0))
    if final_norm:
        in_specs.append(pl.BlockSpec((1, d), lambda t, j: (0, 0)))
        args.append(final_g.reshape(1, d))
        out_specs, out_shape = row_spec, jax.ShapeDtypeStruct((rows, d), F32)
    else:
        in_specs += [pl.BlockSpec((1, d), lambda t, j: (0, 0)), pl.BlockSpec(next_mod.shape, lambda t, j: (0, 0))]
        args += [next_g.reshape(1, d), next_mod]
        out_specs = [row_spec, row_spec]
        out_shape = [jax.ShapeDtypeStruct((rows, d), F32), jax.ShapeDtypeStruct((rows, d), BF16)]
    return pl.pallas_call(
        functools.partial(_peer_mix_kernel, tt=tt, te=te, n_lat=n_lat, seq=seq, n_batch=n_batch,
                          k_gate=k_gate, final_norm=final_norm),
        grid=(rows // tt, n_tiles),
        in_specs=in_specs,
        out_specs=out_specs,
        out_shape=out_shape,
        scratch_shapes=[pltpu.VMEM((d, tt), F32), pltpu.VMEM((te, tt), F32), pltpu.VMEM((te, tt), F32),
                        pltpu.VMEM((te, tt), BF16)],
        compiler_params=_cparams("parallel", "arbitrary"),
        name="peer_mix",
    )(*args)


def _rope_tables(seq, pad_rows):
    half = HEAD_DIM // 2
    t = jnp.arange(seq)
    row = (t // GRID_W).astype(F32)
    col = (t % GRID_W).astype(F32)
    inv = ROPE_THETA ** (-jnp.arange(0, half, 2, dtype=F32) / half)
    ang_r = row[:, None] * inv[None, :]
    ang_c = col[:, None] * inv[None, :]
    cos = jnp.concatenate([jnp.cos(ang_r), jnp.cos(ang_r), jnp.cos(ang_c), jnp.cos(ang_c)], axis=-1)
    sin = jnp.concatenate([-jnp.sin(ang_r), jnp.sin(ang_r), -jnp.sin(ang_c), jnp.sin(ang_c)], axis=-1)
    cos = jnp.concatenate([cos, jnp.ones((pad_rows, HEAD_DIM), F32)], axis=0)
    sin = jnp.concatenate([sin, jnp.zeros((pad_rows, HEAD_DIM), F32)], axis=0)
    return cos, sin


def kernel(x, c, ctx, c_ctx, ada_w, ada_b, norm1_g, norm2_g, w_in, b_gate, q_norm_g, k_norm_g,
           conv_w, w_attn_out, w_conv_out, w_o, peer_wq, peer_k1, peer_k2, peer_u, peer_v, final_g):
    n_batch, seq, d = x.shape
    ctx_len = ctx.shape[1]
    depth = ada_w.shape[0]
    n_lat = n_batch * seq
    n_all = n_lat + n_batch * ctx_len
    assert seq & (seq - 1) == 0 and ctx_len & (ctx_len - 1) == 0
    tm = 1024
    assert seq % tm == 0 and (n_batch * ctx_len) % tm == 0 and seq % GRID_W == 0

    xa = x.reshape(n_lat, d)
    xc = ctx.reshape(n_batch * ctx_len, d)
    pad = (-(n_batch + 1)) % 8
    cvec = jnp.concatenate([c, c_ctx[None, :], jnp.zeros((pad, d), F32)], axis=0)
    mod_all = _adaln(cvec, ada_w, ada_b)
    cos_t, sin_t = _rope_tables(seq, tm)
    geom = dict(n_lat=n_lat, seq=seq, n_batch=n_batch)

    for l in range(depth):
        last = l == depth - 1
        rows = n_lat if last else n_all
        mod = mod_all[l]
        w_in_l = _to_bf16(w_in, l)

        if l == 0:
            h = _norm_mod(xa, norm1_g[l], mod, rows=n_all, k_shift=0, k_scale=1, x_ctx=xc, **geom)
        q, k, v, z, gb, gates = _in_proj(h, w_in_l, q_norm_g[l], k_norm_g[l], b_gate[l], cos_t, sin_t,
                                         rows_q=rows, rows_kv=n_all, n_lat=n_lat, seq=seq, d=d, tm=tm)
        attn = _attention(q, k, v, rows_q=rows, n_lat=n_lat, seq=seq, ctx_len=ctx_len)
        merged = _merge(attn, z, gb, gates, conv_w[l], _to_bf16(w_attn_out, l), _to_bf16(w_conv_out, l),
                        rows=rows, n_lat=n_lat, seq=seq, ctx_len=ctx_len)
        xa, h2, h2t = _resid_proj(merged, _to_bf16(w_o, l), xa, norm2_g[l], mod, rows=rows, k_gate=2,
                                  k_shift=3, k_scale=4, x_ctx=xc, **geom)
        xc = None
        th, e1, e2 = _peer_scores(h2, _to_bf16(peer_wq, l), peer_k1[l], peer_k2[l], rows=rows,
                                  n_i=PEER_TE // N_KEYS)
        mix = functools.partial(_peer_mix, h2t, _to_bf16(peer_u, l), _to_bf16(peer_v, l, transpose=True),
                                th, e1, e2, xa, mod, rows=rows, k_gate=5, **geom)
        if last:
            xa = mix(final_g, None, None)
        else:
            xa, h = mix(None, norm1_g[l + 1], mod_all[l + 1])
    return xa.reshape(n_batch, seq, d)
```

```python
import functools

import jax
import jax.numpy as jnp
from jax import lax
from jax.experimental import pallas as pl
from jax.experimental.pallas import tpu as pltpu

F32 = jnp.float32
BF16 = jnp.bfloat16

HEAD_DIM = 128
KV_GROUPS = 4
GRID_W = 64
ROPE_THETA = 10000.0
PEER_HEADS = 8
N_KEYS = 128
PEER_TOPK = 16
EPS = 1e-6
LOG2_E = 1.4426950408889634
LANES = 128
BF16_SUBLANES = 16
VMEM_LIMIT_BYTES = 56 * 1024 * 1024

_NT = (((1,), (1,)), ((), ()))


def _cparams(*sem):
    return pltpu.CompilerParams(dimension_semantics=sem, vmem_limit_bytes=VMEM_LIMIT_BYTES)


def _mod_row(row0, n_lat, seq, n_batch):
    return jnp.where(row0 < n_lat, row0 // seq, n_batch)


def _mod_vec(mod_ref, b, k, d):
    return mod_ref[pl.ds(b, 1), k * d:(k + 1) * d]


def _adaln_kernel(c_ref, w_ref, b_ref, o_ref):
    cv = c_ref[...]
    s = cv * jax.nn.sigmoid(cv)
    o_ref[0] = jnp.dot(s, w_ref[0], preferred_element_type=F32) + b_ref[0]


def _adaln(cvec, ada_w, ada_b, tn=1024):
    depth, d, n = ada_w.shape
    rows = cvec.shape[0]
    return pl.pallas_call(
        _adaln_kernel,
        grid=(depth, n // tn),
        in_specs=[pl.BlockSpec((rows, d), lambda l, j: (0, 0)),
                  pl.BlockSpec((1, d, tn), lambda l, j: (l, 0, j)),
                  pl.BlockSpec((1, 1, tn), lambda l, j: (l, 0, j))],
        out_specs=pl.BlockSpec((1, rows, tn), lambda l, j: (l, 0, j)),
        out_shape=jax.ShapeDtypeStruct((depth, rows, n), F32),
        compiler_params=_cparams("arbitrary", "arbitrary"),
        name="adaln",
    )(cvec, ada_w, ada_b.reshape(depth, 1, n))


def _cast_kernel(w_ref, o_ref, *, transpose):
    w = w_ref[0]
    o_ref[...] = (w.T if transpose else w).astype(o_ref.dtype)


def _to_bf16(w, layer, *, transpose=False, tr=1024, tc=1024):
    _, rows, cols = w.shape
    in_spec = pl.BlockSpec((1, tr, tc), lambda i, j: (layer, i, j))
    if transpose:
        out_spec, out_shape = pl.BlockSpec((tc, tr), lambda i, j: (j, i)), (cols, rows)
    else:
        out_spec, out_shape = pl.BlockSpec((tr, tc), lambda i, j: (i, j)), (rows, cols)
    return pl.pallas_call(
        functools.partial(_cast_kernel, transpose=transpose),
        grid=(rows // tr, cols // tc),
        in_specs=[in_spec],
        out_specs=out_spec,
        out_shape=jax.ShapeDtypeStruct(out_shape, BF16),
        compiler_params=_cparams("parallel", "parallel"),
        name="cast_bf16",
    )(w)


def _tile_of(lat_ref, ctx_ref, row0, n_lat):
    if ctx_ref is None:
        return lat_ref[...]
    return jnp.where(row0 < n_lat, lat_ref[...], ctx_ref[...])


def _row_specs(x_ctx, block, n_lat, col_of=None):
    n_lat_tiles = n_lat // block[0]
    col = (lambda *j: 0) if col_of is None else col_of
    if x_ctx is None:
        return [pl.BlockSpec(block, lambda i, *j: (i, col(*j)))]
    return [pl.BlockSpec(block, lambda i, *j: (jnp.minimum(i, n_lat_tiles - 1), col(*j))),
            pl.BlockSpec(block, lambda i, *j: (jnp.maximum(i - n_lat_tiles, 0), col(*j)))]


def _modulated_norm(xf, g, mod_ref, b, k_shift, k_scale):
    d = xf.shape[1]
    y = xf * lax.rsqrt(jnp.mean(xf * xf, axis=-1, keepdims=True) + EPS)
    return (y * g) * (1.0 + _mod_vec(mod_ref, b, k_scale, d)) + _mod_vec(mod_ref, b, k_shift, d)


def _norm_mod_kernel(*refs, split, tm, n_lat, seq, n_batch, k_shift, k_scale):
    x_ref, xc_ref = (refs[0], refs[1]) if split else (refs[0], None)
    g_ref, mod_ref, o_ref = refs[2 if split else 1:]
    row0 = pl.program_id(0) * tm
    b = _mod_row(row0, n_lat, seq, n_batch)
    xf = _tile_of(x_ref, xc_ref, row0, n_lat)
    o_ref[...] = _modulated_norm(xf, g_ref[...], mod_ref, b, k_shift, k_scale).astype(o_ref.dtype)


def _norm_mod(x, g, mod, *, rows, n_lat, seq, n_batch, k_shift, k_scale, x_ctx=None, tm=512):
    d = x.shape[1]
    kern = functools.partial(_norm_mod_kernel, split=x_ctx is not None, tm=tm, n_lat=n_lat, seq=seq,
                             n_batch=n_batch, k_shift=k_shift, k_scale=k_scale)
    xs = [x] if x_ctx is None else [x, x_ctx]
    return pl.pallas_call(
        kern,
        grid=(rows // tm,),
        in_specs=_row_specs(x_ctx, (tm, d), n_lat) + [pl.BlockSpec((1, d), lambda i: (0, 0)),
                                                         pl.BlockSpec(mod.shape, lambda i: (0, 0))],
        out_specs=pl.BlockSpec((tm, d), lambda i: (i, 0)),
        out_shape=jax.ShapeDtypeStruct((rows, d), BF16),
        compiler_params=_cparams("parallel"),
        name="norm_mod",
    )(*xs, g.reshape(1, d), mod)


def _heads_norm_rope(acc, ones_bd, perm_bd, g, cos, sin, out_scale):
    n_heads = acc.shape[1] // HEAD_DIM
    ss = jnp.dot((acc * acc).astype(BF16), ones_bd, preferred_element_type=F32)
    y = acc * lax.rsqrt(ss * (1.0 / HEAD_DIM) + EPS) * g
    partner = jnp.dot(y.astype(BF16), perm_bd, preferred_element_type=F32)
    r = y * jnp.tile(cos, (1, n_heads)) + partner * jnp.tile(sin, (1, n_heads))
    if out_scale != 1.0:
        r = r * out_scale
    return r


def _q_kernel(h_ref, w_ref, ones_ref, perm_ref, g_ref, cos_ref, sin_ref, o_ref, *, out_scale):
    acc = jnp.dot(h_ref[...], w_ref[...], preferred_element_type=F32)
    r = _heads_norm_rope(acc, ones_ref[...], perm_ref[...], g_ref[...], cos_ref[...], sin_ref[...], out_scale)
    o_ref[...] = r.astype(o_ref.dtype)


def _kv_kernel(h_ref, wk_ref, wv_ref, ones_ref, perm_ref, g_ref, cos_ref, sin_ref, k_ref, v_ref):
    h = h_ref[...]
    acc = jnp.dot(h, wk_ref[...], preferred_element_type=F32)
    r = _heads_norm_rope(acc, ones_ref[...], perm_ref[...], g_ref[...], cos_ref[...], sin_ref[...], 1.0)
    k_ref[...] = r.astype(k_ref.dtype)
    vacc = jnp.dot(h, wv_ref[...], preferred_element_type=F32).astype(v_ref.dtype)
    ones = jnp.ones((vacc.shape[0], HEAD_DIM), v_ref.dtype)
    for g in range(vacc.shape[1] // HEAD_DIM):
        v_ref[:, 2 * g * HEAD_DIM:(2 * g + 1) * HEAD_DIM] = vacc[:, g * HEAD_DIM:(g + 1) * HEAD_DIM]
        v_ref[:, (2 * g + 1) * HEAD_DIM:(2 * g + 2) * HEAD_DIM] = ones


def _conv_in_kernel(h_ref, wu_ref, wb_ref, wc_ref, z_ref, gb_ref):
    h = h_ref[...]
    u = jnp.dot(h, wu_ref[...], preferred_element_type=F32)
    gc = jnp.dot(h, wc_ref[...], preferred_element_type=F32)
    z_ref[...] = (gc * u).astype(z_ref.dtype)
    gb_ref[...] = jnp.dot(h, wb_ref[...], preferred_element_type=F32).astype(gb_ref.dtype)


def _gate_kernel(h_ref, w_ref, b_ref, o_ref):
    acc = jnp.dot(h_ref[...], w_ref[...], preferred_element_type=F32)
    o_ref[...] = (0.5 + 0.5 * jnp.tanh(0.5 * (acc + b_ref[...]))).astype(o_ref.dtype)


def _rope_block(i, tm, n_lat, seq):
    return jnp.where(i * tm < n_lat, (i * tm % seq) // tm, seq // tm)


def _in_proj(h, w_in, q_g, k_g, b_gate, cos_t, sin_t, *, rows_q, rows_kv, n_lat, seq, d,
             tm=512, tn=512):
    kd = h.shape[1]
    kvw = d // KV_GROUPS
    q_end, k_end, v_end = d, d + kvw, d + 2 * kvw
    u_end, b_end, c_end = v_end + d, v_end + 2 * d, v_end + 3 * d
    hspec = pl.BlockSpec((tm, kd), lambda i, j: (i, 0))
    hd_spec = pl.BlockSpec((1, tn), lambda i, j: (0, 0))
    bd_spec = pl.BlockSpec((tn, tn), lambda i, j: (0, 0))
    heads = tn // HEAD_DIM
    lane = jnp.arange(HEAD_DIM)
    partner = jnp.where(lane % (HEAD_DIM // 2) < HEAD_DIM // 4, lane + HEAD_DIM // 4, lane - HEAD_DIM // 4)
    perm = (lane[:, None] == partner[None, :]).astype(BF16)
    ones_bd = jnp.kron(jnp.eye(heads, dtype=BF16), jnp.ones((HEAD_DIM, HEAD_DIM), BF16))
    perm_bd = jnp.kron(jnp.eye(heads, dtype=BF16), perm)
    tab_spec = pl.BlockSpec((tm, HEAD_DIM), lambda i, j: (_rope_block(i, tm, n_lat, seq), 0))

    def wspec(col0, width=tn):
        return pl.BlockSpec((kd, width), lambda i, j: (0, col0 // width + j))

    def ospec(width=tn):
        return pl.BlockSpec((tm, width), lambda i, j: (i, j))

    tw = 2 * tn

    q = pl.pallas_call(
        functools.partial(_q_kernel, out_scale=HEAD_DIM ** -0.5 * LOG2_E),
        grid=(rows_q // tm, d // tn),
        in_specs=[hspec, wspec(0), bd_spec, bd_spec, hd_spec, tab_spec, tab_spec],
        out_specs=ospec(),
        out_shape=jax.ShapeDtypeStruct((rows_q, d), BF16),
        compiler_params=_cparams("parallel", "arbitrary"),
        name="proj_q",
    )(h, w_in, ones_bd, perm_bd, jnp.tile(q_g, heads).reshape(1, tn), cos_t, sin_t)

    assert kvw == tn
    k, v = pl.pallas_call(
        _kv_kernel,
        grid=(rows_kv // tm, 1),
        in_specs=[hspec, wspec(q_end), wspec(k_end), bd_spec, bd_spec, hd_spec, tab_spec, tab_spec],
        out_specs=[ospec(), pl.BlockSpec((tm, 2 * tn), lambda i, j: (i, j))],
        out_shape=[jax.ShapeDtypeStruct((rows_kv, kvw), BF16), jax.ShapeDtypeStruct((rows_kv, 2 * kvw), BF16)],
        compiler_params=_cparams("parallel", "arbitrary"),
        name="proj_kv",
    )(h, w_in, w_in, ones_bd, perm_bd, jnp.tile(k_g, heads).reshape(1, tn), cos_t, sin_t)

    z, gb = pl.pallas_call(
        _conv_in_kernel,
        grid=(rows_q // tm, d // tw),
        in_specs=[hspec, wspec(v_end, tw), wspec(u_end, tw), wspec(b_end, tw)],
        out_specs=[ospec(tw), ospec(tw)],
        out_shape=[jax.ShapeDtypeStruct((rows_q, d), BF16)] * 2,
        compiler_params=_cparams("parallel", "arbitrary"),
        name="proj_conv",
    )(h, w_in, w_in, w_in)

    gates = pl.pallas_call(
        _gate_kernel,
        grid=(rows_q // tm, 2 * d // tw),
        in_specs=[hspec, wspec(c_end, tw), pl.BlockSpec((1, tw), lambda i, j: (0, j))],
        out_specs=ospec(tw),
        out_shape=jax.ShapeDtypeStruct((rows_q, 2 * d), BF16),
        compiler_params=_cparams("parallel", "arbitrary"),
        name="proj_gates",
    )(h, w_in, b_gate.reshape(1, 2 * d))
    return q, k, v, z, gb, gates


ATTN_KEY_CHUNK = 256


def _attn_kernel(q_ref, kl_ref, vl_ref, kc_ref, vc_ref, o_ref, *, tq, n_lat):
    is_lat = pl.program_id(1) * tq < n_lat
    q = jnp.concatenate([q_ref[:, hh * HEAD_DIM:(hh + 1) * HEAD_DIM] for hh in range(KV_GROUPS)], axis=0)

    def step(carry, k, v):
        s = lax.dot_general(q, k, _NT, preferred_element_type=F32)
        m_new = jnp.max(s, axis=-1, keepdims=True)
        pv = lambda m: jnp.dot(jnp.exp2(s - m).astype(v.dtype), v, preferred_element_type=F32)
        if carry is None:
            return m_new, pv(m_new)
        m_old, acc = carry
        m_new = jnp.maximum(m_old, m_new)
        return m_new, acc * jnp.exp2(m_old - m_new) + pv(m_new)

    def run(with_lat):
        carry = step(None, kc_ref[...], vc_ref[...])
        if with_lat:
            for c in range(kl_ref.shape[0] // ATTN_KEY_CHUNK):
                rows = slice(c * ATTN_KEY_CHUNK, (c + 1) * ATTN_KEY_CHUNK)
                carry = step(carry, kl_ref[rows, :], vl_ref[rows, :])
        acc = carry[1]
        out = (acc[:, :HEAD_DIM] / acc[:, HEAD_DIM:]).astype(o_ref.dtype)
        for hh in range(KV_GROUPS):
            o_ref[:, hh * HEAD_DIM:(hh + 1) * HEAD_DIM] = out[hh * tq:(hh + 1) * tq, :]

    @pl.when(is_lat)
    def _():
        run(True)

    @pl.when(jnp.logical_not(is_lat))
    def _():
        run(False)


def _attention(q, k, v, *, rows_q, n_lat, seq, ctx_len, tq=256):
    assert ctx_len % tq == 0 or tq % ctx_len == 0
    assert tq <= ctx_len
    gw = KV_GROUPS * HEAD_DIM
    n_kv = k.shape[1] // HEAD_DIM

    def batch(i):
        return jnp.where(i * tq < n_lat, i * tq // seq, (i * tq - n_lat) // ctx_len)

    qspec = pl.BlockSpec((tq, gw), lambda g, i: (i, g))

    def lat_spec(width):
        return pl.BlockSpec((seq, width), lambda g, i: (batch(i), g))

    def ctx_spec(width):
        return pl.BlockSpec((ctx_len, width), lambda g, i: (n_lat // ctx_len + batch(i), g))

    return pl.pallas_call(
        functools.partial(_attn_kernel, tq=tq, n_lat=n_lat),
        grid=(n_kv, rows_q // tq),
        in_specs=[qspec, lat_spec(HEAD_DIM), lat_spec(2 * HEAD_DIM), ctx_spec(HEAD_DIM), ctx_spec(2 * HEAD_DIM)],
        out_specs=qspec,
        out_shape=jax.ShapeDtypeStruct((rows_q, q.shape[1]), BF16),
        compiler_params=_cparams("parallel", "arbitrary"),
        name="attention",
    )(q, k, v, k, v)


def _merge_kernel(attn_ref, z_ref, zp_ref, zn_ref, gb_ref, ga_ref, gc_ref, cw_ref,
                  wao_ref, wco_ref, o_ref, y_scr, *, tm, n_lat, seq, ctx_len):
    @pl.when(pl.program_id(1) == 0)
    def _():
        z = z_ref[...].astype(F32)
        loc = lax.broadcasted_iota(jnp.int32, (tm, 1), 0)
        row = pl.program_id(0) * tm + loc
        seqlen = jnp.where(row < n_lat, seq, ctx_len)
        pos = row & (seqlen - 1)
        halo_prev = zp_ref[...].astype(F32)[BF16_SUBLANES - 1:BF16_SUBLANES, :]
        halo_next = zn_ref[...].astype(F32)[0:1, :]
        z_prev = jnp.where(loc == 0, halo_prev, pltpu.roll(z, 1, 0))
        z_prev = jnp.where(pos == 0, 0.0, z_prev)
        z_next = jnp.where(loc == tm - 1, halo_next, pltpu.roll(z, tm - 1, 0))
        z_next = jnp.where(pos == seqlen - 1, 0.0, z_next)
        cw = cw_ref[...]
        conv = cw[0:1, :] * z_prev + cw[1:2, :] * z + cw[2:3, :] * z_next
        y_scr[...] = (gb_ref[...].astype(F32) * conv).astype(y_scr.dtype)

    acc_a = jnp.dot(attn_ref[...], wao_ref[...], preferred_element_type=F32)
    acc_c = jnp.dot(y_scr[...], wco_ref[...], preferred_element_type=F32)
    o_ref[...] = (ga_ref[...].astype(F32) * acc_a + gc_ref[...].astype(F32) * acc_c).astype(o_ref.dtype)


def _merge(attn, z, gb, gates, conv_w, w_ao, w_co, *, rows, n_lat, seq, ctx_len, tm=512, tn=1024):
    d = attn.shape[1]
    hb = BF16_SUBLANES
    n_hblk = z.shape[0] // hb
    row_spec = pl.BlockSpec((tm, d), lambda i, j: (i, 0))
    prev_spec = pl.BlockSpec((hb, d), lambda i, j: (jnp.maximum(i * (tm // hb) - 1, 0), 0))
    next_spec = pl.BlockSpec((hb, d), lambda i, j: (jnp.minimum((i + 1) * (tm // hb), n_hblk - 1), 0))
    wspec = pl.BlockSpec((d, tn), lambda i, j: (0, j))
    return pl.pallas_call(
        functools.partial(_merge_kernel, tm=tm, n_lat=n_lat, seq=seq, ctx_len=ctx_len),
        grid=(rows // tm, d // tn),
        in_specs=[row_spec, row_spec, prev_spec, next_spec, row_spec,
                  pl.BlockSpec((tm, tn), lambda i, j: (i, j)),
                  pl.BlockSpec((tm, tn), lambda i, j: (i, d // tn + j)),
                  pl.BlockSpec(conv_w.shape, lambda i, j: (0, 0)),
                  wspec, wspec],
        out_specs=pl.BlockSpec((tm, tn), lambda i, j: (i, j)),
        out_shape=jax.ShapeDtypeStruct((rows, d), BF16),
        scratch_shapes=[pltpu.VMEM((tm, d), BF16)],
        compiler_params=_cparams("parallel", "arbitrary"),
        name="merge",
    )(attn, z, z, z, gb, gates, gates, conv_w, w_ao, w_co)


def _resid_proj_kernel(a_ref, w_ref, *refs, split, tm, n_lat, seq, n_batch, k_gate, k_shift, k_scale):
    x_ref, xc_ref = (refs[0], refs[1]) if split else (refs[0], None)
    g_ref, mod_ref, o_ref, h_ref, ht_ref = refs[2 if split else 1:]
    d = o_ref.shape[1]
    row0 = pl.program_id(0) * tm
    b = _mod_row(row0, n_lat, seq, n_batch)
    acc = jnp.dot(a_ref[...], w_ref[...], preferred_element_type=F32)
    xn = _tile_of(x_ref, xc_ref, row0, n_lat) + _mod_vec(mod_ref, b, k_gate, d) * acc
    o_ref[...] = xn
    h = _modulated_norm(xn, g_ref[...], mod_ref, b, k_shift, k_scale)
    h_ref[...] = h.astype(h_ref.dtype)
    ht_ref[...] = h.T.astype(ht_ref.dtype)


def _resid_proj(a, w, x, g_next, mod, *, rows, n_lat, seq, n_batch, k_gate, k_shift, k_scale, x_ctx=None, tm=512):
    d = a.shape[1]
    xs = [x] if x_ctx is None else [x, x_ctx]
    return pl.pallas_call(
        functools.partial(_resid_proj_kernel, split=x_ctx is not None, tm=tm, n_lat=n_lat, seq=seq,
                          n_batch=n_batch, k_gate=k_gate, k_shift=k_shift, k_scale=k_scale),
        grid=(rows // tm,),
        in_specs=[pl.BlockSpec((tm, d), lambda i: (i, 0)),
                  pl.BlockSpec((d, d), lambda i: (0, 0))]
        + _row_specs(x_ctx, (tm, d), n_lat)
        + [pl.BlockSpec((1, d), lambda i: (0, 0)),
           pl.BlockSpec(mod.shape, lambda i: (0, 0))],
        out_specs=[pl.BlockSpec((tm, d), lambda i: (i, 0)),
                   pl.BlockSpec((tm, d), lambda i: (i, 0)),
                   pl.BlockSpec((d, tm), lambda i: (0, i))],
        out_shape=[jax.ShapeDtypeStruct((rows, d), F32),
                   jax.ShapeDtypeStruct((rows, d), BF16),
                   jax.ShapeDtypeStruct((d, rows), BF16)],
        compiler_params=_cparams("parallel"),
        name="resid_proj",
    )(a, w, *xs, g_next.reshape(1, d), mod)


N_TOP = PEER_TOPK + 1
TOP_ROWS = 24
N_CAND = 96


F32_SUBLANES = 8


def _merge_exchange_pairs(n):
    pairs = []
    p = 1
    while p < n:
        k = p
        while k >= 1:
            for j in range(k % p, n - k, 2 * k):
                for i in range(min(k, n - j - k)):
                    if (i + j) // (2 * p) == (i + j + k) // (2 * p):
                        pairs.append((i + j, i + j + k))
            k //= 2
        p *= 2
    return pairs


def _top_values(s, k, out_ref):
    n = s.shape[0] // F32_SUBLANES
    r = [s[i * F32_SUBLANES:(i + 1) * F32_SUBLANES, :] for i in range(n)]
    for a, b in _merge_exchange_pairs(n):
        r[a], r[b] = jnp.maximum(r[a], r[b]), jnp.minimum(r[a], r[b])
    prev = mx = None
    for t in range(k):
        prev = mx
        mx = jnp.max(r[0], axis=0, keepdims=True)
        if out_ref is not None:
            out_ref[t:t + 1, :] = mx
        pop = r[0] == mx
        for i in range(min(n, k - 1 - t)):
            r[i] = jnp.where(pop, r[i + 1] if i + 1 < n else -jnp.inf, r[i])
    return prev, mx


def _candidate_sums(v1_ref, v2_ref, cand_ref):
    cand_ref[0:TOP_ROWS, :] = v1_ref[0:1, :] + v2_ref[...]
    for a in range(1, 8):
        cand_ref[16 + 8 * a:24 + 8 * a, :] = v1_ref[a:a + 1, :] + v2_ref[0:8, :]
    cand_ref[80:96, :] = v1_ref[8:TOP_ROWS, :] + v2_ref[0:1, :]


def _peer_score_kernel(h_ref, wq_ref, k1_ref, k2_ref, th_ref, e1_ref, e2_ref,
                       v1_scr, v2_scr, cand_scr, *, n_i):
    qp = jnp.dot(h_ref[...], wq_ref[...], preferred_element_type=F32)
    pad = jnp.full((TOP_ROWS - N_TOP, v1_scr.shape[1]), -jnp.inf, F32)
    v1_scr[N_TOP:TOP_ROWS, :] = pad
    v2_scr[N_TOP:TOP_ROWS, :] = pad
    for h in range(PEER_HEADS):
        q1 = qp[:, (2 * h) * N_KEYS:(2 * h + 1) * N_KEYS]
        q2 = qp[:, (2 * h + 1) * N_KEYS:(2 * h + 2) * N_KEYS]
        s1 = lax.dot_general(k1_ref[h], q1, _NT, preferred_element_type=F32) * LOG2_E
        s2 = lax.dot_general(k2_ref[h], q2, _NT, preferred_element_type=F32) * LOG2_E
        _top_values(s1, N_TOP, v1_scr)
        _top_values(s2, N_TOP, v2_scr)
        _candidate_sums(v1_scr, v2_scr, cand_scr)
        cand = cand_scr[...]
        c16, c17 = _top_values(cand, N_TOP, None)
        tau = 0.5 * (c16 + c17)
        top1 = v1_scr[0:1, :]
        top2 = v2_scr[0:1, :]
        z = jnp.sum(jnp.where(cand >= c16, jnp.exp2(cand - (top1 + top2)), 0.0), axis=0, keepdims=True)
        theta = jnp.exp2(tau - s1 - top2)
        e1 = jnp.exp2(s1 - (top1 + jnp.log2(z) + 1.0))
        for g in range(N_KEYS // n_i):
            th_ref[g, h * n_i:(h + 1) * n_i, :] = theta[g * n_i:(g + 1) * n_i, :]
            e1_ref[g, h * n_i:(h + 1) * n_i, :] = e1[g * n_i:(g + 1) * n_i, :]
        e2_ref[h] = jnp.exp2(s2 - top2)


def _peer_scores(h2, wq, k1, k2, *, rows, n_i, tm=256):
    d = h2.shape[1]
    n_grp = N_KEYS // n_i
    grouped = pl.BlockSpec((n_grp, PEER_HEADS * n_i, tm), lambda i: (0, 0, i))
    by_head = pl.BlockSpec((PEER_HEADS, N_KEYS, tm), lambda i: (0, 0, i))
    grouped_shape = jax.ShapeDtypeStruct((n_grp, PEER_HEADS * n_i, rows), F32)
    by_head_shape = jax.ShapeDtypeStruct((PEER_HEADS, N_KEYS, rows), F32)
    return pl.pallas_call(
        functools.partial(_peer_score_kernel, n_i=n_i),
        grid=(rows // tm,),
        in_specs=[pl.BlockSpec((tm, d), lambda i: (i, 0)),
                  pl.BlockSpec(wq.shape, lambda i: (0, 0)),
                  pl.BlockSpec(k1.shape, lambda i: (0, 0, 0)),
                  pl.BlockSpec(k2.shape, lambda i: (0, 0, 0))],
        out_specs=[grouped, grouped, by_head],
        out_shape=[grouped_shape, grouped_shape, by_head_shape],
        scratch_shapes=[pltpu.VMEM((TOP_ROWS, tm), F32), pltpu.VMEM((TOP_ROWS, tm), F32),
                        pltpu.VMEM((N_CAND, tm), F32)],
        compiler_params=_cparams("parallel"),
        name="peer_scores",
    )(h2, wq, k1, k2)


def _twice_gelu(x):
    return x * (1.0 + lax.erf(x * (0.5 ** 0.5)))


def _peer_mix_kernel(ht_ref, u_ref, vt_ref, th_ref, e1_ref, e2_ref, x_ref, mod_ref, *rest,
                     tt, te, n_lat, seq, n_batch, k_gate, final_norm):
    if final_norm:
        fg_ref, o_ref, acc_ref, a_scr, g_scr, w_scr = rest
    else:
        gn_ref, modn_ref, o_ref, hn_ref, acc_ref, a_scr, g_scr, w_scr = rest
    j = pl.program_id(1)
    d = x_ref.shape[1]
    n_i = te // N_KEYS

    @pl.when(j == 0)
    def _():
        acc_ref[...] = jnp.zeros_like(acc_ref)

    a_scr[...] = jnp.dot(u_ref[...], ht_ref[...], preferred_element_type=F32)

    tails = []
    for ii in range(n_i):
        for tc in range(tt // LANES):
            cs = slice(tc * LANES, (tc + 1) * LANES)
            for r in range(N_KEYS // F32_SUBLANES):
                ks = slice(r * F32_SUBLANES, (r + 1) * F32_SUBLANES)
                gate = tails[-GATE_CHAINS] * 0.0 if len(tails) >= GATE_CHAINS else None
                for h in range(PEER_HEADS):
                    row = slice(h * n_i + ii, h * n_i + ii + 1)
                    e2 = e2_ref[h, ks, cs]
                    wgt = jnp.where(e2 >= th_ref[0, row, cs], e2, 0.0) * e1_ref[0, row, cs]
                    gate = wgt if gate is None else gate + wgt
                tails.append(gate)
                g_scr[ii * N_KEYS + r * F32_SUBLANES:ii * N_KEYS + (r + 1) * F32_SUBLANES, cs] = gate

    w_scr[...] = (g_scr[...] * _twice_gelu(a_scr[...])).astype(w_scr.dtype)
    acc_ref[...] += jnp.dot(vt_ref[...], w_scr[...], preferred_element_type=F32)

    @pl.when(j == pl.num_programs(1) - 1)
    def _():
        b = _mod_row(pl.program_id(0) * tt, n_lat, seq, n_batch)
        xn = x_ref[...] + _mod_vec(mod_ref, b, k_gate, d) * acc_ref[...].T
        if final_norm:
            xn = xn * lax.rsqrt(jnp.mean(xn * xn, axis=-1, keepdims=True) + EPS) * fg_ref[...]
        else:
            hn_ref[...] = _modulated_norm(xn, gn_ref[...], modn_ref, b, 0, 1).astype(hn_ref.dtype)
        o_ref[...] = xn


PEER_TE = 512
GATE_CHAINS = 2


def _peer_mix(h2t, u, vt, th, e1, e2, x, mod, final_g, next_g, next_mod, *, rows, n_lat, seq, n_batch, k_gate,
              tt=512, te=PEER_TE):
    d = x.shape[1]
    n_tiles = u.shape[0] // te
    final_norm = final_g is not None
    n_i = te // N_KEYS
    grouped = pl.BlockSpec((1, PEER_HEADS * n_i, tt), lambda t, j: (j, 0, t))
    by_head = pl.BlockSpec((PEER_HEADS, N_KEYS, tt), lambda t, j: (0, 0, t))
    in_specs = [pl.BlockSpec((d, tt), lambda t, j: (0, t)),
                pl.BlockSpec((te, d), lambda t, j: (j, 0)),
                pl.BlockSpec((d, te), lambda t, j: (0, j)),
                grouped, grouped, by_head,
                pl.BlockSpec((tt, d), lambda t, j: (t, 0)),
                pl.BlockSpec(mod.shape, lambda t, j: (0, 0))]
    args = [h2t, u, vt, th, e1, e2, x, mod]
    row_spec = pl.BlockSpec((tt, d), lambda t, j: (t, 0))
    if final_norm:
        in_specs.append(pl.BlockSpec((1, d), lambda t, j: (0, 0)))
        args.append(final_g.reshape(1, d))
        out_specs, out_shape = row_spec, jax.ShapeDtypeStruct((rows, d), F32)
    else:
        in_specs += [pl.BlockSpec((1, d), lambda t, j: (0, 0)), pl.BlockSpec(next_mod.shape, lambda t, j: (0, 0))]
        args += [next_g.reshape(1, d), next_mod]
        out_specs = [row_spec, row_spec]
        out_shape = [jax.ShapeDtypeStruct((rows, d), F32), jax.ShapeDtypeStruct((rows, d), BF16)]
    return pl.pallas_call(
        functools.partial(_peer_mix_kernel, tt=tt, te=te, n_lat=n_lat, seq=seq, n_batch=n_batch,
                          k_gate=k_gate, final_norm=final_norm),
        grid=(rows // tt, n_tiles),
        in_specs=in_specs,
        out_specs=out_specs,
        out_shape=out_shape,
        scratch_shapes=[pltpu.VMEM((d, tt), F32), pltpu.VMEM((te, tt), F32), pltpu.VMEM((te, tt), F32),
                        pltpu.VMEM((te, tt), BF16)],
        compiler_params=_cparams("parallel", "arbitrary"),
        name="peer_mix",
    )(*args)


def _rope_tables(seq, pad_rows):
    half = HEAD_DIM // 2
    t = jnp.arange(seq)
    row = (t // GRID_W).astype(F32)
    col = (t % GRID_W).astype(F32)
    inv = ROPE_THETA ** (-jnp.arange(0, half, 2, dtype=F32) / half)
    ang_r = row[:, None] * inv[None, :]
    ang_c = col[:, None] * inv[None, :]
    cos = jnp.concatenate([jnp.cos(ang_r), jnp.cos(ang_r), jnp.cos(ang_c), jnp.cos(ang_c)], axis=-1)
    sin = jnp.concatenate([-jnp.sin(ang_r), jnp.sin(ang_r), -jnp.sin(ang_c), jnp.sin(ang_c)], axis=-1)
    cos = jnp.concatenate([cos, jnp.ones((pad_rows, HEAD_DIM), F32)], axis=0)
    sin = jnp.concatenate([sin, jnp.zeros((pad_rows, HEAD_DIM), F32)], axis=0)
    return cos, sin


def kernel(x, c, ctx, c_ctx, ada_w, ada_b, norm1_g, norm2_g, w_in, b_gate, q_norm_g, k_norm_g,
           conv_w, w_attn_out, w_conv_out, w_o, peer_wq, peer_k1, peer_k2, peer_u, peer_v, final_g):
    n_batch, seq, d = x.shape
    ctx_len = ctx.shape[1]
    depth = ada_w.shape[0]
    n_lat = n_batch * seq
    n_all = n_lat + n_batch * ctx_len
    assert seq & (seq - 1) == 0 and ctx_len & (ctx_len - 1) == 0
    tm = 1024
    assert seq % tm == 0 and (n_batch * ctx_len) % tm == 0 and seq % GRID_W == 0

    xa = x.reshape(n_lat, d)
    xc = ctx.reshape(n_batch * ctx_len, d)
    pad = (-(n_batch + 1)) % 8
    cvec = jnp.concatenate([c, c_ctx[None, :], jnp.zeros((pad, d), F32)], axis=0)
    mod_all = _adaln(cvec, ada_w, ada_b)
    cos_t, sin_t = _rope_tables(seq, tm)
    geom = dict(n_lat=n_lat, seq=seq, n_batch=n_batch)

    for l in range(depth):
        last = l == depth - 1
        rows = n_lat if last else n_all
        mod = mod_all[l]
        w_in_l = _to_bf16(w_in, l)

        if l == 0:
            h = _norm_mod(xa, norm1_g[l], mod, rows=n_all, k_shift=0, k_scale=1, x_ctx=xc, **geom)
        q, k, v, z, gb, gates = _in_proj(h, w_in_l, q_norm_g[l], k_norm_g[l], b_gate[l], cos_t, sin_t,
                                         rows_q=rows, rows_kv=n_all, n_lat=n_lat, seq=seq, d=d, tm=tm)
        attn = _attention(q, k, v, rows_q=rows, n_lat=n_lat, seq=seq, ctx_len=ctx_len)
        merged = _merge(attn, z, gb, gates, conv_w[l], _to_bf16(w_attn_out, l), _to_bf16(w_conv_out, l),
                        rows=rows, n_lat=n_lat, seq=seq, ctx_len=ctx_len)
        xa, h2, h2t = _resid_proj(merged, _to_bf16(w_o, l), xa, norm2_g[l], mod, rows=rows, k_gate=2,
                                  k_shift=3, k_scale=4, x_ctx=xc, **geom)
        xc = None
        th, e1, e2 = _peer_scores(h2, _to_bf16(peer_wq, l), peer_k1[l], peer_k2[l], rows=rows,
                                  n_i=PEER_TE // N_KEYS)
        mix = functools.partial(_peer_mix, h2t, _to_bf16(peer_u, l), _to_bf16(peer_v, l, transpose=True),
                                th, e1, e2, xa, mod, rows=rows, k_gate=5, **geom)
        if last:
            xa = mix(final_g, None, None)
        else:
            xa, h = mix(None, norm1_g[l + 1], mod_all[l + 1])
    return xa.reshape(n_batch, seq, d)
```
